```python
import jax, jax.numpy as jnp
from jax import lax
import numpy as np

D_MODEL = 1024
BATCH = 8
SEQ = 2048
DEPTH = 2
DEC_BATCH = 128
DEC_SEQ = 1
PAST_LEN = 16384
PAGE_SIZE = 128

D_MIX = 2 * D_MODEL
D_CONV_A = D_MIX // 4
CONV_A_W = 3
D_HG = D_MIX // 4
HG_HEADS = 4
HG_DK = D_HG // HG_HEADS
HG_DV = D_HG // HG_HEADS
HG_CHUNK = 16
D_SSD = D_MIX // 2
SSD_HEADDIM = 64
SSD_HEADS = D_SSD // SSD_HEADDIM
SSD_GROUPS = 2
SSD_STATE = 128
SSD_CONV_W = 4
SSD_CHUNK = 64
SSD_CONV_DIM = D_SSD + 2 * SSD_GROUPS * SSD_STATE
N_MEM = 256
XA_HEADS = 4
XA_HEADDIM = D_MODEL // XA_HEADS
EPS = 1e-6
IN_SIZES = (D_CONV_A,) * 4 + (D_HG,) * 4 + (D_SSD, SSD_CONV_DIM, SSD_HEADS)
D_IN_PROJ = sum(IN_SIZES)

kernel_name = 'hybrid_conv_hgrn2_ssd_memxattn_step'


def rmsnorm(x, g):
    xf = x.astype(jnp.float32)
    y = xf * lax.rsqrt(jnp.mean(xf * xf, axis=-1, keepdims=True) + EPS)
    return (y * g.astype(jnp.float32)).astype(x.dtype)


def split_cols(u, sizes):
    offs = np.cumsum(np.array(sizes))[:-1].tolist()
    return jnp.split(u, offs, axis=-1)


def causal_dwconv(x, w, prev):
    W = w.shape[0]
    L = x.shape[1]
    xp = jnp.concatenate([prev.astype(x.dtype), x], axis=1)
    y = xp[:, 0:L] * w[0]
    for k in range(1, W):
        y = y + xp[:, k:k + L] * w[k]
    return y, xp[:, -(W - 1):]


def pad_time(a, Lp):
    L = a.shape[1]
    cfg = [(0, 0)] * a.ndim
    cfg[1] = (0, Lp - L)
    return jnp.pad(a, cfg)


def hgrn2_chunked(q, k, v, logf, S0):
    b, L, h, dk = q.shape
    C = min(HG_CHUNK, L)
    nc = -(-L // C)
    Lp = nc * C
    q, k, v, logf = [pad_time(a.astype(jnp.float32), Lp) for a in (q, k, v, logf)]
    chunks = lambda a: jnp.moveaxis(a.reshape(b, nc, C, h, a.shape[-1]), 1, 0)
    mask = jnp.tril(jnp.ones((C, C), bool))[None, :, :, None, None]

    def step(S, inp):
        qc, kc, vc, gc = inp
        G = jnp.cumsum(gc, axis=1)
        diff = G[:, :, None] - G[:, None, :]
        decay = jnp.exp(jnp.where(mask, diff, -jnp.inf))
        A = jnp.einsum('bihk,bjhk,bijhk->bhij', qc, kc, decay)
        o = (jnp.einsum('bhij,bjhv->bihv', A, vc)
             + jnp.einsum('bihk,bhkv->bihv', qc * jnp.exp(G), S))
        Glast = G[:, -1]
        kd = kc * jnp.exp(Glast[:, None] - G)
        S = jnp.exp(Glast)[..., None] * S + jnp.einsum('bjhk,bjhv->bhkv', kd, vc)
        return S, o

    S, o = lax.scan(step, S0.astype(jnp.float32), (chunks(q), chunks(k), chunks(v), chunks(logf)))
    o = jnp.moveaxis(o, 0, 1).reshape(b, Lp, h, v.shape[-1])[:, :L]
    return o, S


def segsum(a):
    cs = jnp.cumsum(a, axis=-1)
    T = a.shape[-1]
    mask = jnp.tril(jnp.ones((T, T), bool))
    return jnp.where(mask, cs[..., :, None] - cs[..., None, :], -jnp.inf)


def ssd_chunked(x, dt, A, Bm, Cm, S0):
    b, L, H, P = x.shape
    G, N = Bm.shape[2], Bm.shape[3]
    R = H // G
    Cs = min(SSD_CHUNK, L)
    nc = -(-L // Cs)
    Lp = nc * Cs
    x, dt, Bm, Cm = [pad_time(a.astype(jnp.float32), Lp) for a in (x, dt, Bm, Cm)]
    X = (x * dt[..., None]).reshape(b, nc, Cs, G, R, P)
    dA = (dt * A.astype(jnp.float32)).reshape(b, nc, Cs, G, R).transpose(0, 3, 4, 1, 2)
    Bc = Bm.reshape(b, nc, Cs, G, N)
    Cc = Cm.reshape(b, nc, Cs, G, N)
    Acs = jnp.cumsum(dA, axis=-1)
    Lmat = jnp.exp(segsum(dA))
    y_diag = jnp.einsum('bclgn,bcsgn,bgrcls,bcsgrp->bclgrp', Cc, Bc, Lmat, X)
    decay_states = jnp.exp(Acs[..., -1:] - Acs)
    states = jnp.einsum('bcsgn,bgrcs,bcsgrp->bcgrpn', Bc, decay_states, X)
    chunk_decay = jnp.exp(Acs[..., -1])

    def step(S, inp):
        st, dec = inp
        return dec[..., None, None] * S + st, S

    S_fin, S_prev = lax.scan(step, S0.astype(jnp.float32).reshape(b, G, R, P, N),
                             (jnp.moveaxis(states, 1, 0), jnp.moveaxis(chunk_decay, -1, 0)))
    S_prev = jnp.moveaxis(S_prev, 0, 1)
    y_off = jnp.einsum('bclgn,bcgrpn,bgrcl->bclgrp', Cc, S_prev, jnp.exp(Acs))
    y = (y_diag + y_off).reshape(b, Lp, H, P)[:, :L]
    return y, S_fin.reshape(b, H, P, N)


def memory_kv(mem, g_mem, w_k, w_v):
    b = mem.shape[0]
    m = rmsnorm(mem, g_mem)
    k = (m @ w_k).reshape(b, N_MEM, XA_HEADS, XA_HEADDIM)
    v = (m @ w_v).reshape(b, N_MEM, XA_HEADS, XA_HEADDIM)
    return k, v


def cross_attn(h, mk, mv, w_q, w_o):
    b, L, _ = h.shape
    q = (h @ w_q).reshape(b, L, XA_HEADS, XA_HEADDIM)
    s = jnp.einsum('blhd,bmhd->bhlm', q.astype(jnp.float32), mk.astype(jnp.float32)) * (XA_HEADDIM ** -0.5)
    p = jax.nn.softmax(s, axis=-1)
    o = jnp.einsum('bhlm,bmhd->blhd', p, mv.astype(jnp.float32)).reshape(b, L, D_MODEL).astype(h.dtype)
    return o @ w_o


def decoder_layer(x, conv_a_prev, hg_S, ssd_conv_prev, ssd_S, mk, mv, lb, p):
    (w_in, conv_a_w, hg_gnorm, ssd_conv_w, ssd_conv_b, dt_bias, A_log, D_skip, ssd_norm,
     w_out, g_pre_mix, g_post_mix, g_pre_x, g_post_x, w_q, w_o) = p
    b, L, _ = x.shape
    h = rmsnorm(x, g_pre_mix)
    u = h @ w_in
    a_h, a_b, a_c, a_z, g_q, g_f, g_i, g_z, s_z, s_xbc, s_dt = split_cols(u, IN_SIZES)
    a_conv, conv_a_new = causal_dwconv(a_c * a_h, conv_a_w, conv_a_prev)
    y_a = a_b * a_conv * jax.nn.silu(a_z)
    f = lb + (1.0 - lb) * jax.nn.sigmoid(g_f.astype(jnp.float32))
    logf = jnp.log(f)
    kk = 1.0 - f
    hs = lambda t: t.reshape(b, L, HG_HEADS, HG_DK)
    o, hg_S_new = hgrn2_chunked(hs(g_q), hs(kk), hs(g_i), hs(logf), hg_S)
    o = rmsnorm(o, hg_gnorm.reshape(HG_HEADS, HG_DV)).reshape(b, L, D_HG).astype(x.dtype)
    y_b = o * jax.nn.silu(g_z)
    xbc, ssd_conv_new = causal_dwconv(s_xbc, ssd_conv_w, ssd_conv_prev)
    xbc = jax.nn.silu(xbc + ssd_conv_b)
    sx, sB, sC = split_cols(xbc, (D_SSD, SSD_GROUPS * SSD_STATE, SSD_GROUPS * SSD_STATE))
    dt = jax.nn.softplus(s_dt.astype(jnp.float32) + dt_bias.astype(jnp.float32))
    A = -jnp.exp(A_log.astype(jnp.float32))
    sxh = sx.reshape(b, L, SSD_HEADS, SSD_HEADDIM)
    y, ssd_S_new = ssd_chunked(sxh, dt, A, sB.reshape(b, L, SSD_GROUPS, SSD_STATE),
                               sC.reshape(b, L, SSD_GROUPS, SSD_STATE), ssd_S)
    y = y + D_skip.astype(jnp.float32)[:, None] * sxh.astype(jnp.float32)
    y_c = rmsnorm(y.reshape(b, L, D_SSD).astype(x.dtype) * jax.nn.silu(s_z), ssd_norm)
    mix = jnp.concatenate([y_a, y_b, y_c], axis=-1) @ w_out
    x = x + rmsnorm(mix, g_post_mix)
    x = x + rmsnorm(cross_attn(rmsnorm(x, g_pre_x), mk, mv, w_q, w_o), g_post_x)
    return x, conv_a_new, hg_S_new, ssd_conv_new, ssd_S_new


def setup_inputs(seed: int = 0) -> dict:
    key = jax.random.key(seed)
    ks = jax.random.split(key, 32)
    nrm = lambda k, s, sc: jax.random.normal(k, s, jnp.float32) * sc
    gain = lambda k, s: 1.0 + 0.02 * jax.random.normal(k, s, jnp.float32)
    dt0 = jnp.exp(jax.random.uniform(ks[12], (DEPTH, SSD_HEADS), jnp.float32)
                  * (np.log(0.1) - np.log(0.001)) + np.log(0.001))
    return {
        'x_prompt': nrm(ks[0], (BATCH, SEQ, D_MODEL), 1.0),
        'x_sample': nrm(ks[1], (DEC_BATCH, DEC_SEQ, D_MODEL), 1.0),
        'mem_prompt': nrm(ks[2], (BATCH, N_MEM, D_MODEL), 1.0),
        'state_conv_a': nrm(ks[3], (DEPTH, DEC_BATCH, CONV_A_W - 1, D_CONV_A), 1.0),
        'state_hgrn': nrm(ks[4], (DEPTH, DEC_BATCH, HG_HEADS, HG_DK, HG_DV), 0.1),
        'state_ssd_conv': nrm(ks[5], (DEPTH, DEC_BATCH, SSD_CONV_W - 1, SSD_CONV_DIM), 1.0),
        'state_ssd': nrm(ks[6], (DEPTH, DEC_BATCH, SSD_HEADS, SSD_HEADDIM, SSD_STATE), 0.1),
        'cache_mem_k': nrm(ks[7], (DEPTH, DEC_BATCH, N_MEM, XA_HEADS, XA_HEADDIM), 1.0),
        'cache_mem_v': nrm(ks[8], (DEPTH, DEC_BATCH, N_MEM, XA_HEADS, XA_HEADDIM), 1.0),
        'w_in': nrm(ks[9], (DEPTH, D_MODEL, D_IN_PROJ), D_MODEL ** -0.5),
        'conv_a_w': nrm(ks[10], (DEPTH, CONV_A_W, D_CONV_A), CONV_A_W ** -0.5),
        'hgrn_lb': nrm(ks[11], (DEPTH, D_HG), 1.0),
        'hgrn_gnorm': gain(ks[13], (DEPTH, D_HG)),
        'ssd_conv_w': nrm(ks[14], (DEPTH, SSD_CONV_W, SSD_CONV_DIM), SSD_CONV_W ** -0.5),
        'ssd_conv_b': nrm(ks[15], (DEPTH, SSD_CONV_DIM), 0.02),
        'ssd_dt_bias': dt0 + jnp.log(-jnp.expm1(-dt0)),
        'ssd_A_log': jnp.log(jax.random.uniform(ks[16], (DEPTH, SSD_HEADS), jnp.float32, 1.0, 16.0)),
        'ssd_D': gain(ks[17], (DEPTH, SSD_HEADS)),
        'ssd_norm': gain(ks[18], (DEPTH, D_SSD)),
        'w_out': nrm(ks[19], (DEPTH, D_MIX, D_MODEL), D_MIX ** -0.5),
        'g_pre_mix': gain(ks[20], (DEPTH, D_MODEL)),
        'g_post_mix': gain(ks[21], (DEPTH, D_MODEL)),
        'g_pre_x': gain(ks[22], (DEPTH, D_MODEL)),
        'g_post_x': gain(ks[23], (DEPTH, D_MODEL)),
        'g_mem': gain(ks[24], (DEPTH, D_MODEL)),
        'w_q': nrm(ks[25], (DEPTH, D_MODEL, D_MODEL), D_MODEL ** -0.5),
        'w_k': nrm(ks[26], (DEPTH, D_MODEL, D_MODEL), D_MODEL ** -0.5),
        'w_v': nrm(ks[27], (DEPTH, D_MODEL, D_MODEL), D_MODEL ** -0.5),
        'w_o': nrm(ks[28], (DEPTH, D_MODEL, D_MODEL), D_MODEL ** -0.5),
    }


def reference(x_prompt, x_sample, mem_prompt, state_conv_a, state_hgrn, state_ssd_conv, state_ssd,
              cache_mem_k, cache_mem_v, w_in, conv_a_w, hgrn_lb, hgrn_gnorm, ssd_conv_w, ssd_conv_b,
              ssd_dt_bias, ssd_A_log, ssd_D, ssd_norm, w_out, g_pre_mix, g_post_mix, g_pre_x, g_post_x,
              g_mem, w_q, w_k, w_v, w_o):
    sm = jax.nn.softmax(hgrn_lb.astype(jnp.float32), axis=0)
    lb_all = jnp.cumsum(sm, axis=0) - sm[0:1]
    yp, ys = x_prompt, x_sample
    p_ca, p_hg, p_sc, p_ss, p_mk, p_mv = [], [], [], [], [], []
    s_ca, s_hg, s_sc, s_ss = [], [], [], []
    for l in range(DEPTH):
        p = (w_in[l], conv_a_w[l], hgrn_gnorm[l], ssd_conv_w[l], ssd_conv_b[l], ssd_dt_bias[l],
             ssd_A_log[l], ssd_D[l], ssd_norm[l], w_out[l], g_pre_mix[l], g_post_mix[l],
             g_pre_x[l], g_post_x[l], w_q[l], w_o[l])
        lb = lb_all[l]
        mk, mv = memory_kv(mem_prompt, g_mem[l], w_k[l], w_v[l])
        yp, ca, hg, sc, ss = decoder_layer(
            yp,
            jnp.zeros((BATCH, CONV_A_W - 1, D_CONV_A), x_prompt.dtype),
            jnp.zeros((BATCH, HG_HEADS, HG_DK, HG_DV), jnp.float32),
            jnp.zeros((BATCH, SSD_CONV_W - 1, SSD_CONV_DIM), x_prompt.dtype),
            jnp.zeros((BATCH, SSD_HEADS, SSD_HEADDIM, SSD_STATE), jnp.float32),
            mk, mv, lb, p)
        p_ca.append(ca); p_hg.append(hg); p_sc.append(sc); p_ss.append(ss); p_mk.append(mk); p_mv.append(mv)
        ys, ca, hg, sc, ss = decoder_layer(
            ys, state_conv_a[l], state_hgrn[l], state_ssd_conv[l], state_ssd[l],
            cache_mem_k[l], cache_mem_v[l], lb, p)
        s_ca.append(ca); s_hg.append(hg); s_sc.append(sc); s_ss.append(ss)
    return (yp, ys, jnp.stack(p_ca), jnp.stack(p_hg), jnp.stack(p_sc), jnp.stack(p_ss),
            jnp.stack(p_mk), jnp.stack(p_mv), jnp.stack(s_ca), jnp.stack(s_hg), jnp.stack(s_sc),
            jnp.stack(s_ss))
```

```python
import functools

import numpy as np
import jax
import jax.numpy as jnp
from jax import lax
from jax.experimental import pallas as pl
from jax.experimental.pallas import tpu as pltpu

F32 = jnp.float32
BF16 = jnp.bfloat16

D_MODEL = 1024
D_A = 512
CONV_A_W = 3
D_HG = 512
HG_HEADS = 4
HG_DK = 128
D_SSD = 1024
SSD_P = 64
SSD_HEADS = 16
SSD_GROUPS = 2
SSD_N = 128
SSD_CONV_W = 4
SSD_CONV_DIM = D_SSD + 2 * SSD_GROUPS * SSD_N
N_MEM = 256
XA_HEADS = 4
XA_HD = 256
EPS = 1e-6

OFF_A = 0
OFF_G = 2048
OFF_SZ = 4096
OFF_XBC = 5120
OFF_DT = 6656
D_IN = 6672
D_IN_PAD = 6784

CH = 128
HG_LEVELS = (1, 2, 4, 8, 16, 32, 64)
VMEM_LIMIT = 56 * 1024 * 1024


def _rms(x, g):
    ms = jnp.mean(x * x, axis=-1, keepdims=True)
    return x * lax.rsqrt(ms + EPS) * g


def _silu(x):
    return x * (1.0 / (1.0 + jnp.exp(-x)))


def _sigmoid(x):
    return 1.0 / (1.0 + jnp.exp(-x))


def _softplus(x):
    return jnp.maximum(x, 0.0) + jnp.log(1.0 + jnp.exp(-jnp.abs(x)))


def _dot(a, b):
    return jnp.dot(a, b, preferred_element_type=F32)


def _dot_nt(a, b):
    return lax.dot_general(a, b, (((1,), (1,)), ((), ())), preferred_element_type=F32)


def _dot_tn(a, b):
    return lax.dot_general(a, b, (((0,), (0,)), ((), ())), preferred_element_type=F32)


def _split3(x):
    hi = x.astype(BF16)
    r = x - hi.astype(F32)
    mid = r.astype(BF16)
    lo = (r - mid.astype(F32)).astype(BF16)
    return hi, mid, lo


def _split3_rows(x):
    return jnp.concatenate(_split3(x), axis=0)


def _split3_cols(x):
    return jnp.concatenate(_split3(x), axis=1)


@functools.lru_cache(maxsize=None)
def _consts():
    r = np.arange(CH)
    i, t = r[:, None], r[None, :]
    blocks = [(t <= i), (t > i)]
    masks = [np.eye(CH, dtype=bool)]
    upsel = []
    for s in HG_LEVELS:
        up = ((r // s) % 2 == 1)
        m = (r // (2 * s)) * (2 * s) + s - 1
        blk = np.where(up[:, None], (t > m[:, None]) & (t <= i), (t > i) & (t <= m[:, None]))
        blocks.append(blk)
        same = (i // (2 * s)) == (t // (2 * s))
        masks.append(same & up[:, None] & (~up)[None, :])
        upsel.append(up)
    mstack = np.concatenate([np.tile(b.astype(np.float32), (1, 3)) for b in blocks], axis=0)
    masks = np.stack(masks).astype(np.float32)
    upsel = np.stack(upsel).astype(np.float32)[:, :, None]
    tril = (t <= i).astype(np.float32)
    tril3 = np.tile(tril, (1, 3))
    triu3 = np.tile(tril.T, (3, 1))
    e = (np.arange(D_SSD)[None, :] // SSD_P == np.arange(SSD_HEADS)[:, None]).astype(np.float32)
    expand3 = np.tile(e, (3, 1))
    lane = np.arange(D_SSD) % (2 * SSD_P)
    pairsel = np.stack([(lane < SSD_P), (lane >= SSD_P)]).astype(np.float32)
    return dict(
        mstack=jnp.asarray(mstack, BF16), masks=jnp.asarray(masks, F32), upsel=jnp.asarray(upsel, F32),
        tril=jnp.asarray(tril, F32), tril3=jnp.asarray(tril3, BF16), triu3=jnp.asarray(triu3, BF16),
        expand3=jnp.asarray(expand3, BF16), pairsel=jnp.asarray(pairsel, F32))


def _memkv_kernel(mem_ref, g_ref, wk_ref, wv_ref, k_ref, v_ref):
    m = _rms(mem_ref[0], g_ref[...]).astype(BF16)
    k_ref[0] = _dot(m, wk_ref[...])
    v_ref[0] = _dot(m, wv_ref[...])


def _memory_kv(mem, g_mem, w_k, w_v):
    b = mem.shape[0]
    full = lambda shape: pl.BlockSpec(shape, lambda i: (0,) * len(shape))
    blk = pl.BlockSpec((1, N_MEM, D_MODEL), lambda i: (i, 0, 0))
    return pl.pallas_call(
        _memkv_kernel,
        grid=(b,),
        in_specs=[blk, full((1, D_MODEL)), full((D_MODEL, D_MODEL)), full((D_MODEL, D_MODEL))],
        out_specs=[blk, blk],
        out_shape=[jax.ShapeDtypeStruct((b, N_MEM, D_MODEL), F32)] * 2,
        compiler_params=pltpu.CompilerParams(dimension_semantics=("arbitrary",), vmem_limit_bytes=VMEM_LIMIT),
        name="memory_kv",
    )(mem, g_mem.reshape(1, D_MODEL), w_k, w_v)


def _hgrn_lower_bound(lb_all, layer):
    depth = lb_all.shape[0]
    rows = [lb_all[j:j + 1, :] for j in range(depth)]
    mx = functools.reduce(jnp.maximum, rows)
    ex = [jnp.exp(rw - mx) for rw in rows]
    tot = functools.reduce(lambda a, b: a + b, ex)
    acc = jnp.zeros_like(tot)
    for j in range(1, layer + 1):
        acc = acc + ex[j]
    return acc / tot


def _cross_attention(q, mk, mv):
    outs = []
    for hd in range(XA_HEADS):
        sl = slice(hd * XA_HD, (hd + 1) * XA_HD)
        s = _dot_nt(q[:, sl].astype(BF16), mk[:, sl]) * (XA_HD ** -0.5)
        s = s - jnp.max(s, axis=-1, keepdims=True)
        e = jnp.exp(s)
        p = e * (1.0 / jnp.sum(e, axis=-1, keepdims=True))
        outs.append(_dot(p.astype(BF16), mv[:, sl]))
    return jnp.concatenate(outs, axis=1)


def _prompt_kernel(x_ref, mk_ref, mv_ref, win_ref, wdtT_ref, wout_ref, wq_ref, wo_ref,
                   caw_ref, lb_ref, gn_ref, scw_ref, scb_ref, dtb_ref, dtbc_ref, al_ref, alc_ref, dx_ref,
                   snorm_ref, gpre_ref, gpost_ref, gprex_ref, gpostx_ref,
                   mstack_ref, masks_ref, upsel_ref, tril_ref, tril3_ref, triu3_ref, expand_ref, pairsel_ref,
                   y_ref, ca_ref, hg_ref, sc_ref, ss_ref,
                   bufa, bufc, ug_s, z_s, xbc_s, dt_s, dtT_s, mix_s, sthg, stssd, *, T, layer):
    ti = pl.program_id(1)
    n_chunks = T // CH

    @pl.when(ti == 0)
    def _():
        bufa[0:8, :] = jnp.zeros((8, D_A), F32)
        bufc[0:8, :] = jnp.zeros((8, SSD_CONV_DIM), F32)
        sthg[...] = jnp.zeros(sthg.shape, F32)
        stssd[...] = jnp.zeros(stssd.shape, F32)

    x = x_ref[0]
    h = _rms(x, gpre_ref[...]).astype(BF16)

    ua = _dot(h, win_ref[:, OFF_A:OFF_A + 4 * D_A])
    a_h, a_b, a_c, a_z = (ua[:, k * D_A:(k + 1) * D_A] for k in range(4))
    va = a_c * a_h
    bufa[8:8 + T, :] = va
    caw = caw_ref[...]
    conv = va * caw[2:3, :] + bufa[7:7 + T, :] * caw[1:2, :] + bufa[6:6 + T, :] * caw[0:1, :]
    mix_s[:, 0:D_A] = (a_b * conv * _silu(a_z)).astype(BF16)
    ca_ref[0] = va[T - 2:T, :]
    bufa[0:8, :] = va[T - 8:T, :]

    ug_s[...] = _dot(h, win_ref[:, OFF_G:OFF_G + 4 * D_HG])
    z_s[...] = _dot(h, win_ref[:, OFF_SZ:OFF_SZ + D_SSD])
    sxbc = _dot(h, win_ref[:, OFF_XBC:OFF_XBC + SSD_CONV_DIM])
    bufc[8:8 + T, :] = sxbc
    scw = scw_ref[...]
    xbc = (sxbc * scw[3:4, :] + bufc[7:7 + T, :] * scw[2:3, :] + bufc[6:6 + T, :] * scw[1:2, :]
           + bufc[5:5 + T, :] * scw[0:1, :] + scb_ref[...])
    xbc_s[...] = _silu(xbc)
    sc_ref[0] = sxbc[T - 3:T, :]
    bufc[0:8, :] = sxbc[T - 8:T, :]
    sdt = _dot(h, win_ref[:, OFF_DT:OFF_DT + 128])[:, 0:SSD_HEADS]
    dt_s[...] = _softplus(sdt + dtb_ref[...])
    dtT = _softplus(_dot_nt(wdtT_ref[...], h) + dtbc_ref[...])
    for c in range(n_chunks):
        dtT_s[c] = dtT[:, c * CH:(c + 1) * CH]

    lb = _hgrn_lower_bound(lb_ref[...], layer)
    a_row = -jnp.exp(al_ref[...])
    a_col = -jnp.exp(alc_ref[...])
    tril = tril_ref[...]
    pairsel = pairsel_ref[...]

    def chunk(c, carry):
        r0 = pl.multiple_of(c * CH, CH)
        rows = pl.ds(r0, CH)

        ug = ug_s[rows, :]
        gq, gf, gi, gz = (ug[:, k * D_HG:(k + 1) * D_HG] for k in range(4))
        f = lb + (1.0 - lb) * _sigmoid(gf)
        logf = jnp.log(f)
        kk = 1.0 - f
        lf3 = _split3_rows(logf)
        G = _dot(mstack_ref[0:CH, :], lf3)
        Grev = _dot(mstack_ref[CH:2 * CH, :], lf3)
        qg = (gq * jnp.exp(G)).astype(BF16)
        kd = (kk * jnp.exp(Grev)).astype(BF16)
        dec = jnp.exp(G[CH - 1:CH, :])
        q_b, k_b, v_b = gq.astype(BF16), kk.astype(BF16), gi.astype(BF16)
        hs = [slice(hd * HG_DK, (hd + 1) * HG_DK) for hd in range(HG_HEADS)]
        A = [masks_ref[0] * _dot_nt(q_b[:, s_], k_b[:, s_]) for s_ in hs]
        for li in range(len(HG_LEVELS)):
            e = jnp.exp(_dot(mstack_ref[(2 + li) * CH:(3 + li) * CH, :], lf3))
            up = upsel_ref[li]
            qs = (gq * e * up).astype(BF16)
            ks = (kk * e * (1.0 - up)).astype(BF16)
            m = masks_ref[li + 1]
            A = [A[hd] + m * _dot_nt(qs[:, hs[hd]], ks[:, hs[hd]]) for hd in range(HG_HEADS)]
        o_heads = []
        for hd in range(HG_HEADS):
            s_ = hs[hd]
            st = sthg[hd]
            o = _dot_nt(qg[:, s_], st.astype(BF16)) + _dot(A[hd].astype(BF16), v_b[:, s_])
            sthg[hd] = st * dec[:, s_] + _dot_tn(v_b[:, s_], kd[:, s_])
            o_heads.append(_rms(o, gn_ref[:, s_]))
        yb = jnp.concatenate(o_heads, axis=1) * _silu(gz)
        mix_s[rows, D_A:D_A + D_HG] = yb.astype(BF16)

        xbc_c = xbc_s[rows, :]
        xs = xbc_c[:, 0:D_SSD]
        Bm = xbc_c[:, D_SSD:D_SSD + SSD_GROUPS * SSD_N].astype(BF16)
        Cm = xbc_c[:, D_SSD + SSD_GROUPS * SSD_N:].astype(BF16)
        dt = dt_s[rows, :]
        dtT_c = dtT_s[c]
        cs = _dot(tril3_ref[...], _split3_rows(dt * a_row))
        csT = _dot(_split3_cols(dtT_c * a_col), triu3_ref[...])
        cs_last = cs[CH - 1:CH, :]
        w_all = jnp.concatenate([dt * jnp.exp(cs_last - cs), jnp.exp(cs), dt,
                                 jnp.broadcast_to(jnp.exp(cs_last), (8, SSD_HEADS))], axis=0)
        e_all = _dot(_split3_cols(w_all), expand_ref[...])
        e_dec, e_cs, e_dt, e_last = e_all[0:CH], e_all[CH:2 * CH], e_all[2 * CH:3 * CH], e_all[3 * CH:3 * CH + 1]
        xdt = xs * e_dt
        xdt_lo = (xdt * pairsel[0:1, :]).astype(BF16)
        xdt_hi = (xdt * pairsel[1:2, :]).astype(BF16)
        xdec = (xs * e_dec).astype(BF16)
        y_groups = []
        hpg = SSD_HEADS // SSD_GROUPS
        gw = hpg * SSD_P
        for g in range(SSD_GROUPS):
            Cg = Cm[:, g * SSD_N:(g + 1) * SSD_N]
            Bg = Bm[:, g * SSD_N:(g + 1) * SSD_N]
            cb = _dot_nt(Cg, Bg)
            st = stssd[g]
            gcols = slice(g * gw, (g + 1) * gw)
            y_off = _dot(Cg, st.astype(BF16)) * e_cs[:, gcols]
            stssd[g] = st * e_last[:, gcols] + _dot_tn(Bg, xdec[:, gcols])
            pair_out = []
            for pr in range(hpg // 2):
                h0 = g * hpg + 2 * pr
                ms = []
                for hh in (h0, h0 + 1):
                    diff = cs[:, hh:hh + 1] - csT[hh:hh + 1, :]
                    ms.append((cb * (jnp.exp(jnp.minimum(diff, 0.0)) * tril)).astype(BF16))
                lhs = jnp.concatenate(ms, axis=1)
                pc = slice(h0 * SSD_P, (h0 + 2) * SSD_P)
                rhs = jnp.concatenate([xdt_lo[:, pc], xdt_hi[:, pc]], axis=0)
                pair_out.append(_dot(lhs, rhs))
            y_groups.append(y_off + jnp.concatenate(pair_out, axis=1))
        y = jnp.concatenate(y_groups, axis=1) + dx_ref[...] * xs
        yc = _rms(y * _silu(z_s[rows, :]), snorm_ref[...])
        mix_s[rows, D_A + D_HG:] = yc.astype(BF16)
        return carry

    lax.fori_loop(0, n_chunks, chunk, 0)

    x1 = x + _rms(_dot(mix_s[...], wout_ref[...]), gpost_ref[...])
    hx = _rms(x1, gprex_ref[...]).astype(BF16)
    q = _dot(hx, wq_ref[...])
    att = _cross_attention(q, mk_ref[0].astype(BF16), mv_ref[0].astype(BF16))
    y_ref[0] = x1 + _rms(_dot(att.astype(BF16), wo_ref[...]), gpostx_ref[...])

    @pl.when(ti == pl.num_programs(1) - 1)
    def _():
        for hd in range(HG_HEADS):
            hg_ref[0, hd] = sthg[hd].T
        hpg = SSD_HEADS // SSD_GROUPS
        for g in range(SSD_GROUPS):
            sg = stssd[g].T
            for hh in range(hpg):
                ss_ref[0, g * hpg + hh] = sg[hh * SSD_P:(hh + 1) * SSD_P, :]


def _prompt_layer(x, mk, mv, wts, layer, T):
    b, L, _ = x.shape
    c = _consts()
    n_chunks = T // CH
    const_names = ("mstack", "masks", "upsel", "tril", "tril3", "triu3", "expand3", "pairsel")
    consts = [c[k] for k in const_names]
    small = [wts[k] for k in ("caw", "lb", "gn", "scw", "scb", "dtb", "dtbc", "al", "alc", "dx", "snorm",
                              "gpre", "gpost", "gprex", "gpostx")]
    big = [wts[k] for k in ("win", "wdtT", "wout", "wq", "wo")]

    def full(a):
        nd = a.ndim
        return pl.BlockSpec(a.shape, lambda bi, ti, _n=nd: (0,) * _n, pipeline_mode=pl.Buffered(1))

    in_specs = ([pl.BlockSpec((1, T, D_MODEL), lambda bi, ti: (bi, ti, 0)),
                 pl.BlockSpec((1, N_MEM, D_MODEL), lambda bi, ti: (bi, 0, 0)),
                 pl.BlockSpec((1, N_MEM, D_MODEL), lambda bi, ti: (bi, 0, 0))]
                + [full(a) for a in big] + [full(a) for a in small] + [full(a) for a in consts])
    out_shape = [jax.ShapeDtypeStruct((b, L, D_MODEL), F32),
                 jax.ShapeDtypeStruct((b, CONV_A_W - 1, D_A), F32),
                 jax.ShapeDtypeStruct((b, HG_HEADS, HG_DK, HG_DK), F32),
                 jax.ShapeDtypeStruct((b, SSD_CONV_W - 1, SSD_CONV_DIM), F32),
                 jax.ShapeDtypeStruct((b, SSD_HEADS, SSD_P, SSD_N), F32)]
    out_specs = [pl.BlockSpec((1, T, D_MODEL), lambda bi, ti: (bi, ti, 0)),
                 pl.BlockSpec((1, CONV_A_W - 1, D_A), lambda bi, ti: (bi, 0, 0)),
                 pl.BlockSpec((1, HG_HEADS, HG_DK, HG_DK), lambda bi, ti: (bi, 0, 0, 0)),
                 pl.BlockSpec((1, SSD_CONV_W - 1, SSD_CONV_DIM), lambda bi, ti: (bi, 0, 0)),
                 pl.BlockSpec((1, SSD_HEADS, SSD_P, SSD_N), lambda bi, ti: (bi, 0, 0, 0))]
    scratch = [pltpu.VMEM((8 + T, D_A), F32), pltpu.VMEM((8 + T, SSD_CONV_DIM), F32),
               pltpu.VMEM((T, 4 * D_HG), F32), pltpu.VMEM((T, D_SSD), F32), pltpu.VMEM((T, SSD_CONV_DIM), F32),
               pltpu.VMEM((T, SSD_HEADS), F32), pltpu.VMEM((n_chunks, SSD_HEADS, CH), F32),
               pltpu.VMEM((T, 2 * D_MODEL), BF16),
               pltpu.VMEM((HG_HEADS, HG_DK, HG_DK), F32),
               pltpu.VMEM((SSD_GROUPS, SSD_N, (SSD_HEADS // SSD_GROUPS) * SSD_P), F32)]
    return pl.pallas_call(
        functools.partial(_prompt_kernel, T=T, layer=layer),
        grid=(b, L // T),
        in_specs=in_specs, out_specs=out_specs, out_shape=out_shape, scratch_shapes=scratch,
        compiler_params=pltpu.CompilerParams(dimension_semantics=("arbitrary", "arbitrary"),
                                             vmem_limit_bytes=VMEM_LIMIT),
        name=f"prompt_layer{layer}",
    )(x, mk, mv, *big, *small, *consts)


def _layer_weights(l, w_in, conv_a_w, hgrn_lb, hgrn_gnorm, ssd_conv_w, ssd_conv_b, ssd_dt_bias, ssd_A_log, ssd_D,
                   ssd_norm, w_out, g_pre_mix, g_post_mix, g_pre_x, g_post_x, w_q, w_o):
    win = jnp.pad(w_in[l].astype(BF16), ((0, 0), (0, D_IN_PAD - D_IN)))
    row = lambda a: a.reshape(1, -1)
    return dict(
        win=win, wdtT=w_in[l][:, OFF_DT:OFF_DT + SSD_HEADS].T.astype(BF16),
        wout=w_out[l].astype(BF16), wq=w_q[l].astype(BF16), wo=w_o[l].astype(BF16),
        caw=conv_a_w[l], lb=hgrn_lb, gn=row(hgrn_gnorm[l]), scw=ssd_conv_w[l], scb=row(ssd_conv_b[l]),
        dtb=row(ssd_dt_bias[l]), dtbc=ssd_dt_bias[l].reshape(-1, 1), al=row(ssd_A_log[l]),
        alc=ssd_A_log[l].reshape(-1, 1), dx=row(jnp.repeat(ssd_D[l], SSD_P)), snorm=row(ssd_norm[l]),
        gpre=row(g_pre_mix[l]), gpost=row(g_post_mix[l]), gprex=row(g_pre_x[l]), gpostx=row(g_post_x[l]))


SB = 8
D_HGP = 4 * D_HG
D_SSP = 4 * D_SSD + 2 * SSD_GROUPS * SSD_N


def _full_spec(a):
    nd = a.ndim
    return pl.BlockSpec(a.shape, lambda *_, _n=nd: (0,) * _n)


def _sample_pre_kernel(x_ref, ca_ref, sc_ref, win_ref, caw_ref, lb_ref, scw_ref, scb_ref, dtb_ref, al_ref,
                       gpre_ref, expand_ref,
                       ya_ref, canew_ref, hgp_ref, ssp_ref, scnew_ref, *, layer):
    h = _rms(x_ref[...], gpre_ref[...]).astype(BF16)
    u = _dot(h, win_ref[...])
    a_h, a_b, a_c, a_z = (u[:, OFF_A + k * D_A:OFF_A + (k + 1) * D_A] for k in range(4))
    va = a_c * a_h
    p0, p1 = ca_ref[:, 0:D_A], ca_ref[:, D_A:2 * D_A]
    caw = caw_ref[...]
    conv = va * caw[2:3, :] + p1 * caw[1:2, :] + p0 * caw[0:1, :]
    ya_ref[...] = a_b * conv * _silu(a_z)
    canew_ref[:, 0:D_A] = p1
    canew_ref[:, D_A:2 * D_A] = va
    lb = _hgrn_lower_bound(lb_ref[...], layer)
    gq, gf, gi, gz = (u[:, OFF_G + k * D_HG:OFF_G + (k + 1) * D_HG] for k in range(4))
    hgp_ref[:, 0:D_HG] = gq
    hgp_ref[:, D_HG:2 * D_HG] = lb + (1.0 - lb) * _sigmoid(gf)
    hgp_ref[:, 2 * D_HG:3 * D_HG] = gi
    hgp_ref[:, 3 * D_HG:] = gz
    sxbc = u[:, OFF_XBC:OFF_XBC + SSD_CONV_DIM]
    W = SSD_CONV_DIM
    q0, q1, q2 = sc_ref[:, 0:W], sc_ref[:, W:2 * W], sc_ref[:, 2 * W:3 * W]
    scw = scw_ref[...]
    xbc = _silu(sxbc * scw[3:4, :] + q2 * scw[2:3, :] + q1 * scw[1:2, :] + q0 * scw[0:1, :] + scb_ref[...])
    scnew_ref[:, 0:W] = q1
    scnew_ref[:, W:2 * W] = q2
    scnew_ref[:, 2 * W:3 * W] = sxbc
    xs = xbc[:, 0:D_SSD]
    dt = _softplus(u[:, OFF_DT:OFF_DT + SSD_HEADS] + dtb_ref[...])
    dec = jnp.exp(dt * -jnp.exp(al_ref[...]))
    n = dt.shape[0]
    e_all = _dot(_split3_cols(jnp.concatenate([dt, dec], axis=0)), expand_ref[...])
    ssp_ref[:, 0:D_SSD] = xs
    ssp_ref[:, D_SSD:2 * D_SSD] = xs * e_all[0:n]
    ssp_ref[:, 2 * D_SSD:3 * D_SSD] = e_all[n:2 * n]
    ssp_ref[:, 3 * D_SSD:4 * D_SSD] = u[:, OFF_SZ:OFF_SZ + D_SSD]
    ssp_ref[:, 4 * D_SSD:] = xbc[:, D_SSD:]


def _sample_pre(x, ca, sc, wts, layer):
    n = x.shape[0]
    args = [x, ca, sc, wts["win"], wts["caw"], wts["lb"], wts["scw"], wts["scb"], wts["dtb"], wts["al"],
            wts["gpre"], _consts()["expand3"]]
    out_shape = [jax.ShapeDtypeStruct((n, D_A), F32), jax.ShapeDtypeStruct((n, 2 * D_A), F32),
                 jax.ShapeDtypeStruct((n, D_HGP), F32), jax.ShapeDtypeStruct((n, D_SSP), F32),
                 jax.ShapeDtypeStruct((n, 3 * SSD_CONV_DIM), F32)]
    return pl.pallas_call(
        functools.partial(_sample_pre_kernel, layer=layer),
        in_specs=[_full_spec(a) for a in args],
        out_specs=[_full_spec(s) for s in out_shape],
        out_shape=out_shape, grid=(1,),
        compiler_params=pltpu.CompilerParams(dimension_semantics=("arbitrary",), vmem_limit_bytes=VMEM_LIMIT),
        name=f"sample_pre{layer}",
    )(*args)


def _pad_rows_T(blk):
    w = blk.shape[1]
    return jnp.concatenate([blk, jnp.zeros((128 - blk.shape[0], w), blk.dtype)], axis=0).T


def _sample_state_kernel(hgp_ref, ssp_ref, shg_ref, sss_ref, o_ref, y_ref, shg_out, sss_out):
    rid_hg = lax.broadcasted_iota(jnp.int32, (SB, HG_DK), 0)
    for hd in range(HG_HEADS):
        cols = slice(hd * HG_DK, (hd + 1) * HG_DK)
        q_b = hgp_ref[:, cols].astype(BF16)
        fT = _pad_rows_T(hgp_ref[:, D_HG + hd * HG_DK:D_HG + (hd + 1) * HG_DK])
        o = jnp.zeros((SB, HG_DK), F32)
        for j in range(SB):
            fcol = fT[:, j:j + 1]
            vrow = hgp_ref[j:j + 1, 2 * D_HG + hd * HG_DK:2 * D_HG + (hd + 1) * HG_DK]
            s_new = fcol * shg_ref[j, hd] + (1.0 - fcol) * vrow
            shg_out[j, hd] = s_new
            o = jnp.where(rid_hg == j, _dot(q_b, s_new.astype(BF16)), o)
        o_ref[:, cols] = o
    gw = (SSD_HEADS // SSD_GROUPS) * SSD_P
    rid_ss = lax.broadcasted_iota(jnp.int32, (SB, gw), 0)
    xdtT = _pad_rows_T(ssp_ref[:, D_SSD:2 * D_SSD])
    decT = _pad_rows_T(ssp_ref[:, 2 * D_SSD:3 * D_SSD])
    for g in range(SSD_GROUPS):
        rows = slice(g * gw, (g + 1) * gw)
        c_b = ssp_ref[:, 4 * D_SSD + (SSD_GROUPS + g) * SSD_N:4 * D_SSD + (SSD_GROUPS + g + 1) * SSD_N].astype(BF16)
        y = jnp.zeros((SB, gw), F32)
        for j in range(SB):
            brow = ssp_ref[j:j + 1, 4 * D_SSD + g * SSD_N:4 * D_SSD + (g + 1) * SSD_N]
            s_new = decT[rows, j:j + 1] * sss_ref[j, rows, :] + xdtT[rows, j:j + 1] * brow
            sss_out[j, rows, :] = s_new
            y = jnp.where(rid_ss == j, _dot_nt(c_b, s_new.astype(BF16)), y)
        y_ref[:, rows] = y


def _sample_state(hgp, ssp, shg_all, sss_all, layer):
    n = hgp.shape[0]
    rowblk = lambda w: pl.BlockSpec((SB, w), lambda i: (i, 0))
    hg_in = pl.BlockSpec((None, SB, HG_HEADS, HG_DK, HG_DK), lambda i: (layer, i, 0, 0, 0))
    ss_in = pl.BlockSpec((None, SB, SSD_HEADS * SSD_P, SSD_N), lambda i: (layer, i, 0, 0))
    hg_blk = pl.BlockSpec((SB, HG_HEADS, HG_DK, HG_DK), lambda i: (i, 0, 0, 0))
    ss_blk = pl.BlockSpec((SB, SSD_HEADS * SSD_P, SSD_N), lambda i: (i, 0, 0))
    return pl.pallas_call(
        _sample_state_kernel,
        grid=(n // SB,),
        in_specs=[rowblk(D_HGP), rowblk(D_SSP), hg_in, ss_in],
        out_specs=[rowblk(D_HG), rowblk(D_SSD), hg_blk, ss_blk],
        out_shape=[jax.ShapeDtypeStruct((n, D_HG), F32), jax.ShapeDtypeStruct((n, D_SSD), F32),
                   jax.ShapeDtypeStruct(shg_all.shape[1:], F32), jax.ShapeDtypeStruct(sss_all.shape[1:], F32)],
        compiler_params=pltpu.CompilerParams(dimension_semantics=("arbitrary",), vmem_limit_bytes=VMEM_LIMIT),
        name="sample_state",
    )(hgp, ssp, shg_all, sss_all)


def _sample_mid_kernel(x_ref, ya_ref, o_ref, y_ref, hgp_ref, ssp_ref, wout_ref, wq_ref, gn_ref, dx_ref, snorm_ref,
                       gpost_ref, gprex_ref, x1_ref, q_ref):
    gz = hgp_ref[:, 3 * D_HG:]
    o = o_ref[...]
    yb = jnp.concatenate([_rms(o[:, hd * HG_DK:(hd + 1) * HG_DK], gn_ref[:, hd * HG_DK:(hd + 1) * HG_DK])
                          for hd in range(HG_HEADS)], axis=1) * _silu(gz)
    y = y_ref[...] + dx_ref[...] * ssp_ref[:, 0:D_SSD]
    yc = _rms(y * _silu(ssp_ref[:, 3 * D_SSD:4 * D_SSD]), snorm_ref[...])
    mix = jnp.concatenate([ya_ref[...], yb, yc], axis=1).astype(BF16)
    x1 = x_ref[...] + _rms(_dot(mix, wout_ref[...]), gpost_ref[...])
    x1_ref[...] = x1
    q = _dot(_rms(x1, gprex_ref[...]).astype(BF16), wq_ref[...])
    for hd in range(XA_HEADS):
        for k in range(XA_HD // 128):
            q_ref[:, k * XA_HEADS + hd, :] = q[:, hd * XA_HD + k * 128:hd * XA_HD + (k + 1) * 128]


def _sample_mid(x, ya, o, y, hgp, ssp, wts):
    n = x.shape[0]
    args = [x, ya, o, y, hgp, ssp, wts["wout"], wts["wq"], wts["gn"], wts["dx"], wts["snorm"], wts["gpost"],
            wts["gprex"]]
    out_shape = [jax.ShapeDtypeStruct((n, D_MODEL), F32),
                 jax.ShapeDtypeStruct((n, XA_HEADS * (XA_HD // 128), 128), F32)]
    return pl.pallas_call(
        _sample_mid_kernel, grid=(1,),
        in_specs=[_full_spec(a) for a in args], out_specs=[_full_spec(s) for s in out_shape], out_shape=out_shape,
        compiler_params=pltpu.CompilerParams(dimension_semantics=("arbitrary",), vmem_limit_bytes=VMEM_LIMIT),
        name="sample_mid",
    )(*args)


KV_SPLIT = XA_HD // 128
KV_SUB = XA_HEADS * KV_SPLIT
KV_ROWS = N_MEM * KV_SUB


def _cache_rows_view(c_all):
    depth, n = c_all.shape[:2]
    c = c_all.reshape(depth, n, N_MEM, XA_HEADS, KV_SPLIT, 128)
    return jnp.transpose(c, (0, 1, 2, 4, 3, 5)).reshape(depth, n, KV_ROWS, 128)


def _lane_class_reduce(x, op):
    sh = KV_SUB
    while sh < 128:
        x = op(x, pltpu.roll(x, sh, axis=1))
        sh *= 2
    return x


def _sample_attn_kernel(x1_ref, q_ref, k_ref, v_ref, wo_ref, gpostx_ref, x2_ref):
    lane = lax.broadcasted_iota(jnp.int32, (KV_SUB, KV_ROWS), 1)
    sub = lax.broadcasted_iota(jnp.int32, (KV_SUB, KV_ROWS), 0)
    own = ((lane & (KV_SUB - 1)) == sub).astype(F32)
    rid = lax.broadcasted_iota(jnp.int32, (SB, KV_ROWS), 0)
    t_all = jnp.zeros((SB, KV_ROWS), F32)
    for j in range(SB):
        r = _dot_nt(q_ref[j].astype(BF16), k_ref[j].astype(BF16))
        t = jnp.sum(r * own, axis=0, keepdims=True)
        t_all = jnp.where(rid == j, t, t_all)
    n_tiles = KV_ROWS // 128
    lane1 = lax.broadcasted_iota(jnp.int32, (SB, 128), 1)
    piece = (lane1 // XA_HEADS) % KV_SPLIT
    chunks = []
    for c in range(n_tiles):
        x = t_all[:, c * 128:(c + 1) * 128]
        tot = x
        for k in range(1, KV_SPLIT):
            fwd = pltpu.roll(x, 128 - k * XA_HEADS, axis=1)
            bwd = pltpu.roll(x, (KV_SPLIT - k) * XA_HEADS, axis=1)
            tot = tot + jnp.where(piece + k < KV_SPLIT, fwd, bwd)
        chunks.append(tot * (XA_HD ** -0.5))
    mx = _lane_class_reduce(functools.reduce(jnp.maximum, chunks), jnp.maximum)
    es = [jnp.exp(ch - mx) for ch in chunks]
    den = _lane_class_reduce(functools.reduce(lambda a, b: a + b, es), lambda a, b: a + b)
    p_all = jnp.concatenate([e * (1.0 / den) for e in es], axis=1)
    rid_o = lax.broadcasted_iota(jnp.int32, (SB, D_MODEL), 0)
    att = jnp.zeros((SB, D_MODEL), F32)
    for j in range(SB):
        p8 = (own * p_all[j:j + 1, :]).astype(BF16)
        o = _dot(p8, v_ref[j].astype(BF16))
        row = jnp.concatenate([o[k * XA_HEADS + hd:k * XA_HEADS + hd + 1, :]
                               for hd in range(XA_HEADS) for k in range(KV_SPLIT)], axis=1)
        att = jnp.where(rid_o == j, row, att)
    x2_ref[...] = x1_ref[...] + _rms(_dot(att.astype(BF16), wo_ref[...]), gpostx_ref[...])


def _sample_attn(x1, q8, ck_rows, cv_rows, wts, layer):
    n = x1.shape[0]
    rowblk = pl.BlockSpec((SB, D_MODEL), lambda i: (i, 0))
    qblk = pl.BlockSpec((SB, KV_SUB, 128), lambda i: (i, 0, 0))
    kvblk = pl.BlockSpec((None, SB, KV_ROWS, 128), lambda i: (layer, i, 0, 0))
    return pl.pallas_call(
        _sample_attn_kernel,
        grid=(n // SB,),
        in_specs=[rowblk, qblk, kvblk, kvblk, _full_spec(wts["wo"]), _full_spec(wts["gpostx"])],
        out_specs=rowblk,
        out_shape=jax.ShapeDtypeStruct((n, D_MODEL), F32),
        compiler_params=pltpu.CompilerParams(dimension_semantics=("arbitrary",), vmem_limit_bytes=VMEM_LIMIT),
        name="sample_attn",
    )(x1, q8, ck_rows, cv_rows, wts["wo"], wts["gpostx"])


def _sample_layer(x, ca, shg_all, sc, sss_all, ck_rows, cv_rows, wts, layer):
    n = x.shape[0]
    depth = shg_all.shape[0]
    ya, ca_new, hgp, ssp, sc_new = _sample_pre(x, ca.reshape(n, -1), sc.reshape(n, -1), wts, layer)
    o, y, shg_new, sss_new = _sample_state(hgp, ssp, shg_all,
                                           sss_all.reshape(depth, n, SSD_HEADS * SSD_P, SSD_N), layer)
    x1, q8 = _sample_mid(x, ya, o, y, hgp, ssp, wts)
    x2 = _sample_attn(x1, q8, ck_rows, cv_rows, wts, layer)
    return (x2, ca_new.reshape(ca.shape), shg_new, sc_new.reshape(sc.shape), sss_new.reshape(sss_all.shape[1:]))


PROMPT_TILE = 256


def kernel(x_prompt, x_sample, mem_prompt, state_conv_a, state_hgrn, state_ssd_conv, state_ssd, cache_mem_k,
           cache_mem_v, w_in, conv_a_w, hgrn_lb, hgrn_gnorm, ssd_conv_w, ssd_conv_b, ssd_dt_bias, ssd_A_log, ssd_D,
           ssd_norm, w_out, g_pre_mix, g_post_mix, g_pre_x, g_post_x, g_mem, w_q, w_k, w_v, w_o):
    depth = w_in.shape[0]
    b = x_prompt.shape[0]
    n = x_sample.shape[0]
    yp = x_prompt
    ys = x_sample.reshape(n, D_MODEL)
    outs = [[] for _ in range(10)]
    ck_rows, cv_rows = _cache_rows_view(cache_mem_k), _cache_rows_view(cache_mem_v)
    for l in range(depth):
        wts = _layer_weights(l, w_in, conv_a_w, hgrn_lb, hgrn_gnorm, ssd_conv_w, ssd_conv_b, ssd_dt_bias, ssd_A_log,
                             ssd_D, ssd_norm, w_out, g_pre_mix, g_post_mix, g_pre_x, g_post_x, w_q, w_o)
        mk, mv = _memory_kv(mem_prompt, g_mem[l], w_k[l].astype(BF16), w_v[l].astype(BF16))
        yp, ca, hg, sc, ss = _prompt_layer(yp, mk, mv, wts, l, PROMPT_TILE)
        ys, s_ca, s_hg, s_sc, s_ss = _sample_layer(ys, state_conv_a[l], state_hgrn, state_ssd_conv[l],
                                                   state_ssd, ck_rows, cv_rows, wts, l)
        kv_shape = (b, N_MEM, XA_HEADS, XA_HD)
        for lst, val in zip(outs, (ca, hg, sc, ss, mk.reshape(kv_shape), mv.reshape(kv_shape), s_ca, s_hg, s_sc, s_ss)):
            lst.append(val)
    return (yp, ys.reshape(x_sample.shape)) + tuple(jnp.stack(o) for o in outs)
```

```python
import functools

import numpy as np
import jax
import jax.numpy as jnp
from jax import lax
from jax.experimental import pallas as pl
from jax.experimental.pallas import tpu as pltpu

F32 = jnp.float32
BF16 = jnp.bfloat16

D_MODEL = 1024
D_A = 512
CONV_A_W = 3
D_HG = 512
HG_HEADS = 4
HG_DK = 128
D_SSD = 1024
SSD_P = 64
SSD_HEADS = 16
SSD_GROUPS = 2
SSD_N = 128
SSD_CONV_W = 4
SSD_CONV_DIM = D_SSD + 2 * SSD_GROUPS * SSD_N
N_MEM = 256
XA_HEADS = 4
XA_HD = 256
EPS = 1e-6

OFF_A = 0
OFF_G = 2048
OFF_SZ = 4096
OFF_XBC = 5120
OFF_DT = 6656
D_IN = 6672
D_IN_PAD = 6784

CH = 128
HG_LEVELS = (1, 2, 4, 8, 16, 32, 64)
VMEM_LIMIT = 56 * 1024 * 1024


def _rms(x, g):
    ms = jnp.mean(x * x, axis=-1, keepdims=True)
    return x * lax.rsqrt(ms + EPS) * g


def _silu(x):
    return x * (1.0 / (1.0 + jnp.exp(-x)))


def _sigmoid(x):
    return 1.0 / (1.0 + jnp.exp(-x))


def _softplus(x):
    return jnp.maximum(x, 0.0) + jnp.log(1.0 + jnp.exp(-jnp.abs(x)))


def _dot(a, b):
    return jnp.dot(a, b, preferred_element_type=F32)


def _dot_nt(a, b):
    return lax.dot_general(a, b, (((1,), (1,)), ((), ())), preferred_element_type=F32)


def _dot_tn(a, b):
    return lax.dot_general(a, b, (((0,), (0,)), ((), ())), preferred_element_type=F32)


def _split3(x):
    hi = x.astype(BF16)
    r = x - hi.astype(F32)
    mid = r.astype(BF16)
    lo = (r - mid.astype(F32)).astype(BF16)
    return hi, mid, lo


def _split3_rows(x):
    return jnp.concatenate(_split3(x), axis=0)


def _split3_cols(x):
    return jnp.concatenate(_split3(x), axis=1)


@functools.lru_cache(maxsize=None)
def _consts():
    r = np.arange(CH)
    i, t = r[:, None], r[None, :]
    masks = [np.eye(CH, dtype=bool)]
    for s in HG_LEVELS:
        up = ((r // s) % 2 == 1)
        same = (i // (2 * s)) == (t // (2 * s))
        masks.append(same & up[:, None] & (~up)[None, :])
    masks = np.stack(masks).astype(np.float32)
    tril = (t <= i).astype(np.float32)
    tril3 = np.tile(tril, (1, 3))
    triu3 = np.tile(tril.T, (3, 1))
    e = (np.arange(D_SSD)[None, :] // SSD_P == np.arange(SSD_HEADS)[:, None]).astype(np.float32)
    expand3 = np.tile(e, (3, 1))
    lane = np.arange(D_SSD) % (2 * SSD_P)
    pairsel = np.stack([(lane < SSD_P), (lane >= SSD_P)]).astype(np.float32)
    return dict(
        masks=jnp.asarray(masks, F32),
        tril=jnp.asarray(tril, F32), tril3=jnp.asarray(tril3, BF16), triu3=jnp.asarray(triu3, BF16),
        expand3=jnp.asarray(expand3, BF16), pairsel=jnp.asarray(pairsel, F32))


def _memkv_kernel(mem_ref, g_ref, wk_ref, wv_ref, k_ref, v_ref):
    m = _rms(mem_ref[0], g_ref[...]).astype(BF16)
    k_ref[0] = _dot(m, wk_ref[...])
    v_ref[0] = _dot(m, wv_ref[...])


def _memory_kv(mem, g_mem, w_k, w_v):
    b = mem.shape[0]
    full = lambda shape: pl.BlockSpec(shape, lambda i: (0,) * len(shape))
    blk = pl.BlockSpec((1, N_MEM, D_MODEL), lambda i: (i, 0, 0))
    return pl.pallas_call(
        _memkv_kernel,
        grid=(b,),
        in_specs=[blk, full((1, D_MODEL)), full((D_MODEL, D_MODEL)), full((D_MODEL, D_MODEL))],
        out_specs=[blk, blk],
        out_shape=[jax.ShapeDtypeStruct((b, N_MEM, D_MODEL), F32)] * 2,
        compiler_params=pltpu.CompilerParams(dimension_semantics=("arbitrary",), vmem_limit_bytes=VMEM_LIMIT),
        name="memory_kv",
    )(mem, g_mem.reshape(1, D_MODEL), w_k, w_v)


def _hgrn_lower_bound(lb_all, layer):
    depth = lb_all.shape[0]
    rows = [lb_all[j:j + 1, :] for j in range(depth)]
    mx = functools.reduce(jnp.maximum, rows)
    ex = [jnp.exp(rw - mx) for rw in rows]
    tot = functools.reduce(lambda a, b: a + b, ex)
    acc = jnp.zeros_like(tot)
    for j in range(1, layer + 1):
        acc = acc + ex[j]
    return acc / tot


def _hgrn_level(c, f, q, k, s):
    n, w = c.shape
    if s >= 8:
        nb = n // (2 * s)
        c4, q4, k4 = (a.reshape(nb, 2, s, w) for a in (c, q, k))
        lower, upper = c4[:, 0], c4[:, 1]
        tot = lower[:, s - 1:s, :]
        w_lower = k4[:, 0] * jnp.exp(tot - lower)
        w_upper = q4[:, 1] * jnp.exp(upper)
        wv = jnp.stack([w_lower, w_upper], axis=1).reshape(n, w)
        c_next = jnp.stack([lower, upper + tot], axis=1).reshape(n, w)
        return wv, c_next
    sub = lax.broadcasted_iota(jnp.int32, (1, 8, w), 1)
    c3, f3, q3, k3 = (a.reshape(n // 8, 8, w) for a in (c, f, q, k))
    up = (sub // s) % 2 == 1
    tot = None
    for gi in reversed(range(8 // (2 * s))):
        r = gi * 2 * s + s - 1
        tg = jnp.broadcast_to(c3[:, r:r + 1, :], c3.shape)
        tot = tg if tot is None else jnp.where(sub < (gi + 1) * 2 * s, tg, tot)
    if s == 1:
        e = jnp.where(up, f3, 1.0)
    else:
        e = jnp.exp(jnp.where(up, c3, tot - c3))
    wv = jnp.where(up, q3, k3) * e
    c_next = c3 + jnp.where(up, tot, 0.0)
    return wv.reshape(n, w), c_next.reshape(n, w)


def _cross_attention(q, mk, mv):
    outs = []
    for hd in range(XA_HEADS):
        sl = slice(hd * XA_HD, (hd + 1) * XA_HD)
        s = _dot_nt(q[:, sl].astype(BF16), mk[:, sl]) * (XA_HD ** -0.5)
        s = s - jnp.max(s, axis=-1, keepdims=True)
        e = jnp.exp(s)
        p = e * (1.0 / jnp.sum(e, axis=-1, keepdims=True))
        outs.append(_dot(p.astype(BF16), mv[:, sl]))
    return jnp.concatenate(outs, axis=1)


def _prompt_kernel(x_ref, mk_ref, mv_ref, win_ref, wdtT_ref, wout_ref, wq_ref, wo_ref,
                   caw_ref, lb_ref, gn_ref, scw_ref, scb_ref, dtb_ref, dtbc_ref, al_ref, alc_ref, dx_ref,
                   snorm_ref, gpre_ref, gpost_ref, gprex_ref, gpostx_ref,
                   masks_ref, tril_ref, tril3_ref, triu3_ref, expand_ref, pairsel_ref,
                   y_ref, ca_ref, hg_ref, sc_ref, ss_ref,
                   bufa, bufc, ug_s, z_s, xbc_s, dt_s, dtT_s, mix_s, sthg, stssd, *, T, layer):
    ti = pl.program_id(1)
    n_chunks = T // CH

    @pl.when(ti == 0)
    def _():
        bufa[0:8, :] = jnp.zeros((8, D_A), F32)
        bufc[0:8, :] = jnp.zeros((8, SSD_CONV_DIM), F32)
        sthg[...] = jnp.zeros(sthg.shape, F32)
        stssd[...] = jnp.zeros(stssd.shape, F32)

    x = x_ref[0]
    h = _rms(x, gpre_ref[...]).astype(BF16)

    ua = _dot(h, win_ref[:, OFF_A:OFF_A + 4 * D_A])
    a_h, a_b, a_c, a_z = (ua[:, k * D_A:(k + 1) * D_A] for k in range(4))
    va = a_c * a_h
    bufa[8:8 + T, :] = va
    caw = caw_ref[...]
    conv = va * caw[2:3, :] + bufa[7:7 + T, :] * caw[1:2, :] + bufa[6:6 + T, :] * caw[0:1, :]
    mix_s[:, 0:D_A] = (a_b * conv * _silu(a_z)).astype(BF16)
    ca_ref[0] = va[T - 2:T, :]
    bufa[0:8, :] = va[T - 8:T, :]

    ug_s[...] = _dot(h, win_ref[:, OFF_G:OFF_G + 4 * D_HG])
    z_s[...] = _dot(h, win_ref[:, OFF_SZ:OFF_SZ + D_SSD])
    sxbc = _dot(h, win_ref[:, OFF_XBC:OFF_XBC + SSD_CONV_DIM])
    bufc[8:8 + T, :] = sxbc
    scw = scw_ref[...]
    xbc = (sxbc * scw[3:4, :] + bufc[7:7 + T, :] * scw[2:3, :] + bufc[6:6 + T, :] * scw[1:2, :]
           + bufc[5:5 + T, :] * scw[0:1, :] + scb_ref[...])
    xbc_s[...] = _silu(xbc)
    sc_ref[0] = sxbc[T - 3:T, :]
    bufc[0:8, :] = sxbc[T - 8:T, :]
    sdt = _dot(h, win_ref[:, OFF_DT:OFF_DT + 128])[:, 0:SSD_HEADS]
    dt_s[...] = _softplus(sdt + dtb_ref[...])
    dtT = _softplus(_dot_nt(wdtT_ref[...], h) + dtbc_ref[...])
    for c in range(n_chunks):
        dtT_s[c] = dtT[:, c * CH:(c + 1) * CH]

    lb = _hgrn_lower_bound(lb_ref[...], layer)
    a_row = -jnp.exp(al_ref[...])
    a_col = -jnp.exp(alc_ref[...])
    tril = tril_ref[...]
    pairsel = pairsel_ref[...]

    def chunk(c, carry):
        r0 = pl.multiple_of(c * CH, CH)
        rows = pl.ds(r0, CH)

        ug = ug_s[rows, :]
        gq, gf, gi, gz = (ug[:, k * D_HG:(k + 1) * D_HG] for k in range(4))
        f = lb + (1.0 - lb) * _sigmoid(gf)
        logf = jnp.log(f)
        kk = 1.0 - f
        q_b, k_b, v_b = gq.astype(BF16), kk.astype(BF16), gi.astype(BF16)
        hs = [slice(hd * HG_DK, (hd + 1) * HG_DK) for hd in range(HG_HEADS)]
        A = [masks_ref[0] * _dot_nt(q_b[:, s_], k_b[:, s_]) for s_ in hs]
        G = logf
        for li, s in enumerate(HG_LEVELS):
            w, G = _hgrn_level(G, f, gq, kk, s)
            w = w.astype(BF16)
            m = masks_ref[li + 1]
            A = [A[hd] + m * _dot_nt(w[:, hs[hd]], w[:, hs[hd]]) for hd in range(HG_HEADS)]
        g_last = G[CH - 1:CH, :]
        qg = (gq * jnp.exp(G)).astype(BF16)
        kd = (kk * jnp.exp(g_last - G)).astype(BF16)
        dec = jnp.exp(g_last)
        o_heads = []
        for hd in range(HG_HEADS):
            s_ = hs[hd]
            st = sthg[hd]
            o = _dot_nt(qg[:, s_], st.astype(BF16)) + _dot(A[hd].astype(BF16), v_b[:, s_])
            sthg[hd] = st * dec[:, s_] + _dot_tn(v_b[:, s_], kd[:, s_])
            o_heads.append(_rms(o, gn_ref[:, s_]))
        yb = jnp.concatenate(o_heads, axis=1) * _silu(gz)
        mix_s[rows, D_A:D_A + D_HG] = yb.astype(BF16)

        xbc_c = xbc_s[rows, :]
        xs = xbc_c[:, 0:D_SSD]
        Bm = xbc_c[:, D_SSD:D_SSD + SSD_GROUPS * SSD_N].astype(BF16)
        Cm = xbc_c[:, D_SSD + SSD_GROUPS * SSD_N:].astype(BF16)
        dt = dt_s[rows, :]
        dtT_c = dtT_s[c]
        cs = _dot(tril3_ref[...], _split3_rows(dt * a_row))
        csT = _dot(_split3_cols(dtT_c * a_col), triu3_ref[...])
        cs_last = cs[CH - 1:CH, :]
        w_all = jnp.concatenate([dt * jnp.exp(cs_last - cs), jnp.exp(cs), dt,
                                 jnp.broadcast_to(jnp.exp(cs_last), (8, SSD_HEADS))], axis=0)
        e_all = _dot(_split3_cols(w_all), expand_ref[...])
        e_dec, e_cs, e_dt, e_last = e_all[0:CH], e_all[CH:2 * CH], e_all[2 * CH:3 * CH], e_all[3 * CH:3 * CH + 1]
        xdt = xs * e_dt
        xdt_lo = (xdt * pairsel[0:1, :]).astype(BF16)
        xdt_hi = (xdt * pairsel[1:2, :]).astype(BF16)
        xdec = (xs * e_dec).astype(BF16)
        y_groups = []
        hpg = SSD_HEADS // SSD_GROUPS
        gw = hpg * SSD_P
        for g in range(SSD_GROUPS):
            Cg = Cm[:, g * SSD_N:(g + 1) * SSD_N]
            Bg = Bm[:, g * SSD_N:(g + 1) * SSD_N]
            cb = _dot_nt(Cg, Bg)
            st = stssd[g]
            gcols = slice(g * gw, (g + 1) * gw)
            y_off = _dot(Cg, st.astype(BF16)) * e_cs[:, gcols]
            stssd[g] = st * e_last[:, gcols] + _dot_tn(Bg, xdec[:, gcols])
            pair_out = []
            for pr in range(hpg // 2):
                h0 = g * hpg + 2 * pr
                ms = []
                for hh in (h0, h0 + 1):
                    diff = cs[:, hh:hh + 1] - csT[hh:hh + 1, :]
                    ms.append((cb * (jnp.exp(jnp.minimum(diff, 0.0)) * tril)).astype(BF16))
                lhs = jnp.concatenate(ms, axis=1)
                pc = slice(h0 * SSD_P, (h0 + 2) * SSD_P)
                rhs = jnp.concatenate([xdt_lo[:, pc], xdt_hi[:, pc]], axis=0)
                pair_out.append(_dot(lhs, rhs))
            y_groups.append(y_off + jnp.concatenate(pair_out, axis=1))
        y = jnp.concatenate(y_groups, axis=1) + dx_ref[...] * xs
        yc = _rms(y * _silu(z_s[rows, :]), snorm_ref[...])
        mix_s[rows, D_A + D_HG:] = yc.astype(BF16)
        return carry

    lax.fori_loop(0, n_chunks, chunk, 0, unroll=True)

    x1 = x + _rms(_dot(mix_s[...], wout_ref[...]), gpost_ref[...])
    hx = _rms(x1, gprex_ref[...]).astype(BF16)
    q = _dot(hx, wq_ref[...])
    att = _cross_attention(q, mk_ref[0].astype(BF16), mv_ref[0].astype(BF16))
    y_ref[0] = x1 + _rms(_dot(att.astype(BF16), wo_ref[...]), gpostx_ref[...])

    @pl.when(ti == pl.num_programs(1) - 1)
    def _():
        for hd in range(HG_HEADS):
            hg_ref[0, hd] = sthg[hd].T
        hpg = SSD_HEADS // SSD_GROUPS
        for g in range(SSD_GROUPS):
            sg = stssd[g].T
            for hh in range(hpg):
                ss_ref[0, g * hpg + hh] = sg[hh * SSD_P:(hh + 1) * SSD_P, :]


def _prompt_layer(x, mk, mv, wts, layer, T):
    b, L, _ = x.shape
    c = _consts()
    n_chunks = T // CH
    const_names = ("masks", "tril", "tril3", "triu3", "expand3", "pairsel")
    consts = [c[k] for k in const_names]
    small = [wts[k] for k in ("caw", "lb", "gn", "scw", "scb", "dtb", "dtbc", "al", "alc", "dx", "snorm",
                              "gpre", "gpost", "gprex", "gpostx")]
    big = [wts[k] for k in ("win", "wdtT", "wout", "wq", "wo")]

    def full(a):
        nd = a.ndim
        return pl.BlockSpec(a.shape, lambda bi, ti, _n=nd: (0,) * _n, pipeline_mode=pl.Buffered(1))

    in_specs = ([pl.BlockSpec((1, T, D_MODEL), lambda bi, ti: (bi, ti, 0)),
                 pl.BlockSpec((1, N_MEM, D_MODEL), lambda bi, ti: (bi, 0, 0)),
                 pl.BlockSpec((1, N_MEM, D_MODEL), lambda bi, ti: (bi, 0, 0))]
                + [full(a) for a in big] + [full(a) for a in small] + [full(a) for a in consts])
    out_shape = [jax.ShapeDtypeStruct((b, L, D_MODEL), F32),
                 jax.ShapeDtypeStruct((b, CONV_A_W - 1, D_A), F32),
                 jax.ShapeDtypeStruct((b, HG_HEADS, HG_DK, HG_DK), F32),
                 jax.ShapeDtypeStruct((b, SSD_CONV_W - 1, SSD_CONV_DIM), F32),
                 jax.ShapeDtypeStruct((b, SSD_HEADS, SSD_P, SSD_N), F32)]
    out_specs = [pl.BlockSpec((1, T, D_MODEL), lambda bi, ti: (bi, ti, 0)),
                 pl.BlockSpec((1, CONV_A_W - 1, D_A), lambda bi, ti: (bi, 0, 0)),
                 pl.BlockSpec((1, HG_HEADS, HG_DK, HG_DK), lambda bi, ti: (bi, 0, 0, 0)),
                 pl.BlockSpec((1, SSD_CONV_W - 1, SSD_CONV_DIM), lambda bi, ti: (bi, 0, 0)),
                 pl.BlockSpec((1, SSD_HEADS, SSD_P, SSD_N), lambda bi, ti: (bi, 0, 0, 0))]
    scratch = [pltpu.VMEM((8 + T, D_A), F32), pltpu.VMEM((8 + T, SSD_CONV_DIM), F32),
               pltpu.VMEM((T, 4 * D_HG), F32), pltpu.VMEM((T, D_SSD), F32), pltpu.VMEM((T, SSD_CONV_DIM), F32),
               pltpu.VMEM((T, SSD_HEADS), F32), pltpu.VMEM((n_chunks, SSD_HEADS, CH), F32),
               pltpu.VMEM((T, 2 * D_MODEL), BF16),
               pltpu.VMEM((HG_HEADS, HG_DK, HG_DK), F32),
               pltpu.VMEM((SSD_GROUPS, SSD_N, (SSD_HEADS // SSD_GROUPS) * SSD_P), F32)]
    return pl.pallas_call(
        functools.partial(_prompt_kernel, T=T, layer=layer),
        grid=(b, L // T),
        in_specs=in_specs, out_specs=out_specs, out_shape=out_shape, scratch_shapes=scratch,
        compiler_params=pltpu.CompilerParams(dimension_semantics=("arbitrary", "arbitrary"),
                                             vmem_limit_bytes=VMEM_LIMIT),
        name=f"prompt_layer{layer}",
    )(x, mk, mv, *big, *small, *consts)


def _layer_weights(l, w_in, conv_a_w, hgrn_lb, hgrn_gnorm, ssd_conv_w, ssd_conv_b, ssd_dt_bias, ssd_A_log, ssd_D,
                   ssd_norm, w_out, g_pre_mix, g_post_mix, g_pre_x, g_post_x, w_q, w_o):
    win = jnp.pad(w_in[l].astype(BF16), ((0, 0), (0, D_IN_PAD - D_IN)))
    row = lambda a: a.reshape(1, -1)
    return dict(
        win=win, wdtT=w_in[l][:, OFF_DT:OFF_DT + SSD_HEADS].T.astype(BF16),
        wout=w_out[l].astype(BF16), wq=w_q[l].astype(BF16), wo=w_o[l].astype(BF16),
        caw=conv_a_w[l], lb=hgrn_lb, gn=row(hgrn_gnorm[l]), scw=ssd_conv_w[l], scb=row(ssd_conv_b[l]),
        dtb=row(ssd_dt_bias[l]), dtbc=ssd_dt_bias[l].reshape(-1, 1), al=row(ssd_A_log[l]),
        alc=ssd_A_log[l].reshape(-1, 1), dx=row(jnp.repeat(ssd_D[l], SSD_P)), snorm=row(ssd_norm[l]),
        gpre=row(g_pre_mix[l]), gpost=row(g_post_mix[l]), gprex=row(g_pre_x[l]), gpostx=row(g_post_x[l]))


SB = 8
D_HGP = 4 * D_HG
D_SSP = 4 * D_SSD + 2 * SSD_GROUPS * SSD_N


def _full_spec(a):
    nd = a.ndim
    return pl.BlockSpec(a.shape, lambda *_, _n=nd: (0,) * _n)


def _sample_pre_kernel(x_ref, ca_ref, sc_ref, win_ref, caw_ref, lb_ref, scw_ref, scb_ref, dtb_ref, al_ref,
                       gpre_ref, expand_ref,
                       ya_ref, canew_ref, hgp_ref, ssp_ref, scnew_ref, *, layer):
    h = _rms(x_ref[...], gpre_ref[...]).astype(BF16)
    u = _dot(h, win_ref[...])
    a_h, a_b, a_c, a_z = (u[:, OFF_A + k * D_A:OFF_A + (k + 1) * D_A] for k in range(4))
    va = a_c * a_h
    p0, p1 = ca_ref[:, 0:D_A], ca_ref[:, D_A:2 * D_A]
    caw = caw_ref[...]
    conv = va * caw[2:3, :] + p1 * caw[1:2, :] + p0 * caw[0:1, :]
    ya_ref[...] = a_b * conv * _silu(a_z)
    canew_ref[:, 0:D_A] = p1
    canew_ref[:, D_A:2 * D_A] = va
    lb = _hgrn_lower_bound(lb_ref[...], layer)
    gq, gf, gi, gz = (u[:, OFF_G + k * D_HG:OFF_G + (k + 1) * D_HG] for k in range(4))
    hgp_ref[:, 0:D_HG] = gq
    hgp_ref[:, D_HG:2 * D_HG] = lb + (1.0 - lb) * _sigmoid(gf)
    hgp_ref[:, 2 * D_HG:3 * D_HG] = gi
    hgp_ref[:, 3 * D_HG:] = gz
    sxbc = u[:, OFF_XBC:OFF_XBC + SSD_CONV_DIM]
    W = SSD_CONV_DIM
    q0, q1, q2 = sc_ref[:, 0:W], sc_ref[:, W:2 * W], sc_ref[:, 2 * W:3 * W]
    scw = scw_ref[...]
    xbc = _silu(sxbc * scw[3:4, :] + q2 * scw[2:3, :] + q1 * scw[1:2, :] + q0 * scw[0:1, :] + scb_ref[...])
    scnew_ref[:, 0:W] = q1
    scnew_ref[:, W:2 * W] = q2
    scnew_ref[:, 2 * W:3 * W] = sxbc
    xs = xbc[:, 0:D_SSD]
    dt = _softplus(u[:, OFF_DT:OFF_DT + SSD_HEADS] + dtb_ref[...])
    dec = jnp.exp(dt * -jnp.exp(al_ref[...]))
    n = dt.shape[0]
    e_all = _dot(_split3_cols(jnp.concatenate([dt, dec], axis=0)), expand_ref[...])
    ssp_ref[:, 0:D_SSD] = xs
    ssp_ref[:, D_SSD:2 * D_SSD] = xs * e_all[0:n]
    ssp_ref[:, 2 * D_SSD:3 * D_SSD] = e_all[n:2 * n]
    ssp_ref[:, 3 * D_SSD:4 * D_SSD] = u[:, OFF_SZ:OFF_SZ + D_SSD]
    ssp_ref[:, 4 * D_SSD:] = xbc[:, D_SSD:]


def _sample_pre(x, ca, sc, wts, layer):
    n = x.shape[0]
    args = [x, ca, sc, wts["win"], wts["caw"], wts["lb"], wts["scw"], wts["scb"], wts["dtb"], wts["al"],
            wts["gpre"], _consts()["expand3"]]
    out_shape = [jax.ShapeDtypeStruct((n, D_A), F32), jax.ShapeDtypeStruct((n, 2 * D_A), F32),
                 jax.ShapeDtypeStruct((n, D_HGP), F32), jax.ShapeDtypeStruct((n, D_SSP), F32),
                 jax.ShapeDtypeStruct((n, 3 * SSD_CONV_DIM), F32)]
    return pl.pallas_call(
        functools.partial(_sample_pre_kernel, layer=layer),
        in_specs=[_full_spec(a) for a in args],
        out_specs=[_full_spec(s) for s in out_shape],
        out_shape=out_shape, grid=(1,),
        compiler_params=pltpu.CompilerParams(dimension_semantics=("arbitrary",), vmem_limit_bytes=VMEM_LIMIT),
        name=f"sample_pre{layer}",
    )(*args)


def _pad_rows_T(blk):
    w = blk.shape[1]
    return jnp.concatenate([blk, jnp.zeros((128 - blk.shape[0], w), blk.dtype)], axis=0).T


def _sample_state_kernel(hgp_ref, ssp_ref, shg_ref, sss_ref, o_ref, y_ref, shg_out, sss_out):
    rid_hg = lax.broadcasted_iota(jnp.int32, (SB, HG_DK), 0)
    for hd in range(HG_HEADS):
        cols = slice(hd * HG_DK, (hd + 1) * HG_DK)
        q_b = hgp_ref[:, cols].astype(BF16)
        fT = _pad_rows_T(hgp_ref[:, D_HG + hd * HG_DK:D_HG + (hd + 1) * HG_DK])
        o = jnp.zeros((SB, HG_DK), F32)
        for j in range(SB):
            fcol = fT[:, j:j + 1]
            vrow = hgp_ref[j:j + 1, 2 * D_HG + hd * HG_DK:2 * D_HG + (hd + 1) * HG_DK]
            s_new = fcol * shg_ref[j, hd] + (1.0 - fcol) * vrow
            shg_out[j, hd] = s_new
            o = jnp.where(rid_hg == j, _dot(q_b, s_new.astype(BF16)), o)
        o_ref[:, cols] = o
    gw = (SSD_HEADS // SSD_GROUPS) * SSD_P
    rid_ss = lax.broadcasted_iota(jnp.int32, (SB, gw), 0)
    xdtT = _pad_rows_T(ssp_ref[:, D_SSD:2 * D_SSD])
    decT = _pad_rows_T(ssp_ref[:, 2 * D_SSD:3 * D_SSD])
    for g in range(SSD_GROUPS):
        rows = slice(g * gw, (g + 1) * gw)
        c_b = ssp_ref[:, 4 * D_SSD + (SSD_GROUPS + g) * SSD_N:4 * D_SSD + (SSD_GROUPS + g + 1) * SSD_N].astype(BF16)
        y = jnp.zeros((SB, gw), F32)
        for j in range(SB):
            brow = ssp_ref[j:j + 1, 4 * D_SSD + g * SSD_N:4 * D_SSD + (g + 1) * SSD_N]
            s_new = decT[rows, j:j + 1] * sss_ref[j, rows, :] + xdtT[rows, j:j + 1] * brow
            sss_out[j, rows, :] = s_new
            y = jnp.where(rid_ss == j, _dot_nt(c_b, s_new.astype(BF16)), y)
        y_ref[:, rows] = y


def _sample_state(hgp, ssp, shg_all, sss_all, layer):
    n = hgp.shape[0]
    rowblk = lambda w: pl.BlockSpec((SB, w), lambda i: (i, 0))
    hg_in = pl.BlockSpec((None, SB, HG_HEADS, HG_DK, HG_DK), lambda i: (layer, i, 0, 0, 0))
    ss_in = pl.BlockSpec((None, SB, SSD_HEADS * SSD_P, SSD_N), lambda i: (layer, i, 0, 0))
    hg_blk = pl.BlockSpec((SB, HG_HEADS, HG_DK, HG_DK), lambda i: (i, 0, 0, 0))
    ss_blk = pl.BlockSpec((SB, SSD_HEADS * SSD_P, SSD_N), lambda i: (i, 0, 0))
    return pl.pallas_call(
        _sample_state_kernel,
        grid=(n // SB,),
        in_specs=[rowblk(D_HGP), rowblk(D_SSP), hg_in, ss_in],
        out_specs=[rowblk(D_HG), rowblk(D_SSD), hg_blk, ss_blk],
        out_shape=[jax.ShapeDtypeStruct((n, D_HG), F32), jax.ShapeDtypeStruct((n, D_SSD), F32),
                   jax.ShapeDtypeStruct(shg_all.shape[1:], F32), jax.ShapeDtypeStruct(sss_all.shape[1:], F32)],
        compiler_params=pltpu.CompilerParams(dimension_semantics=("arbitrary",), vmem_limit_bytes=VMEM_LIMIT),
        name="sample_state",
    )(hgp, ssp, shg_all, sss_all)


def _sample_mid_kernel(x_ref, ya_ref, o_ref, y_ref, hgp_ref, ssp_ref, wout_ref, wq_ref, gn_ref, dx_ref, snorm_ref,
                       gpost_ref, gprex_ref, x1_ref, q_ref):
    gz = hgp_ref[:, 3 * D_HG:]
    o = o_ref[...]
    yb = jnp.concatenate([_rms(o[:, hd * HG_DK:(hd + 1) * HG_DK], gn_ref[:, hd * HG_DK:(hd + 1) * HG_DK])
                          for hd in range(HG_HEADS)], axis=1) * _silu(gz)
    y = y_ref[...] + dx_ref[...] * ssp_ref[:, 0:D_SSD]
    yc = _rms(y * _silu(ssp_ref[:, 3 * D_SSD:4 * D_SSD]), snorm_ref[...])
    mix = jnp.concatenate([ya_ref[...], yb, yc], axis=1).astype(BF16)
    x1 = x_ref[...] + _rms(_dot(mix, wout_ref[...]), gpost_ref[...])
    x1_ref[...] = x1
    q = _dot(_rms(x1, gprex_ref[...]).astype(BF16), wq_ref[...])
    for hd in range(XA_HEADS):
        for k in range(XA_HD // 128):
            q_ref[:, k * XA_HEADS + hd, :] = q[:, hd * XA_HD + k * 128:hd * XA_HD + (k + 1) * 128]


def _sample_mid(x, ya, o, y, hgp, ssp, wts):
    n = x.shape[0]
    args = [x, ya, o, y, hgp, ssp, wts["wout"], wts["wq"], wts["gn"], wts["dx"], wts["snorm"], wts["gpost"],
            wts["gprex"]]
    out_shape = [jax.ShapeDtypeStruct((n, D_MODEL), F32),
                 jax.ShapeDtypeStruct((n, XA_HEADS * (XA_HD // 128), 128), F32)]
    return pl.pallas_call(
        _sample_mid_kernel, grid=(1,),
        in_specs=[_full_spec(a) for a in args], out_specs=[_full_spec(s) for s in out_shape], out_shape=out_shape,
        compiler_params=pltpu.CompilerParams(dimension_semantics=("arbitrary",), vmem_limit_bytes=VMEM_LIMIT),
        name="sample_mid",
    )(*args)


KV_SPLIT = XA_HD // 128
KV_SUB = XA_HEADS * KV_SPLIT
KV_ROWS = N_MEM * KV_SUB


def _cache_rows_view(c_all):
    depth, n = c_all.shape[:2]
    c = c_all.reshape(depth, n, N_MEM, XA_HEADS, KV_SPLIT, 128)
    return jnp.transpose(c, (0, 1, 2, 4, 3, 5)).reshape(depth, n, KV_ROWS, 128)


def _lane_class_reduce(x, op):
    sh = KV_SUB
    while sh < 128:
        x = op(x, pltpu.roll(x, sh, axis=1))
        sh *= 2
    return x


def _sample_attn_kernel(x1_ref, q_ref, k_ref, v_ref, wo_ref, gpostx_ref, x2_ref):
    lane = lax.broadcasted_iota(jnp.int32, (KV_SUB, KV_ROWS), 1)
    sub = lax.broadcasted_iota(jnp.int32, (KV_SUB, KV_ROWS), 0)
    own = ((lane & (KV_SUB - 1)) == sub).astype(F32)
    rid = lax.broadcasted_iota(jnp.int32, (SB, KV_ROWS), 0)
    t_all = jnp.zeros((SB, KV_ROWS), F32)
    for j in range(SB):
        r = _dot_nt(q_ref[j].astype(BF16), k_ref[j].astype(BF16))
        t = jnp.sum(r * own, axis=0, keepdims=True)
        t_all = jnp.where(rid == j, t, t_all)
    n_tiles = KV_ROWS // 128
    lane1 = lax.broadcasted_iota(jnp.int32, (SB, 128), 1)
    piece = (lane1 // XA_HEADS) % KV_SPLIT
    chunks = []
    for c in range(n_tiles):
        x = t_all[:, c * 128:(c + 1) * 128]
        tot = x
        for k in range(1, KV_SPLIT):
            fwd = pltpu.roll(x, 128 - k * XA_HEADS, axis=1)
            bwd = pltpu.roll(x, (KV_SPLIT - k) * XA_HEADS, axis=1)
            tot = tot + jnp.where(piece + k < KV_SPLIT, fwd, bwd)
        chunks.append(tot * (XA_HD ** -0.5))
    mx = _lane_class_reduce(functools.reduce(jnp.maximum, chunks), jnp.maximum)
    es = [jnp.exp(ch - mx) for ch in chunks]
    den = _lane_class_reduce(functools.reduce(lambda a, b: a + b, es), lambda a, b: a + b)
    p_all = jnp.concatenate([e * (1.0 / den) for e in es], axis=1)
    rid_o = lax.broadcasted_iota(jnp.int32, (SB, D_MODEL), 0)
    att = jnp.zeros((SB, D_MODEL), F32)
    for j in range(SB):
        p8 = (own * p_all[j:j + 1, :]).astype(BF16)
        o = _dot(p8, v_ref[j].astype(BF16))
        row = jnp.concatenate([o[k * XA_HEADS + hd:k * XA_HEADS + hd + 1, :]
                               for hd in range(XA_HEADS) for k in range(KV_SPLIT)], axis=1)
        att = jnp.where(rid_o == j, row, att)
    x2_ref[...] = x1_ref[...] + _rms(_dot(att.astype(BF16), wo_ref[...]), gpostx_ref[...])


def _sample_attn(x1, q8, ck_rows, cv_rows, wts, layer):
    n = x1.shape[0]
    rowblk = pl.BlockSpec((SB, D_MODEL), lambda i: (i, 0))
    qblk = pl.BlockSpec((SB, KV_SUB, 128), lambda i: (i, 0, 0))
    kvblk = pl.BlockSpec((None, SB, KV_ROWS, 128), lambda i: (layer, i, 0, 0))
    return pl.pallas_call(
        _sample_attn_kernel,
        grid=(n // SB,),
        in_specs=[rowblk, qblk, kvblk, kvblk, _full_spec(wts["wo"]), _full_spec(wts["gpostx"])],
        out_specs=rowblk,
        out_shape=jax.ShapeDtypeStruct((n, D_MODEL), F32),
        compiler_params=pltpu.CompilerParams(dimension_semantics=("arbitrary",), vmem_limit_bytes=VMEM_LIMIT),
        name="sample_attn",
    )(x1, q8, ck_rows, cv_rows, wts["wo"], wts["gpostx"])


def _sample_layer(x, ca, shg_all, sc, sss_all, ck_rows, cv_rows, wts, layer):
    n = x.shape[0]
    depth = shg_all.shape[0]
    ya, ca_new, hgp, ssp, sc_new = _sample_pre(x, ca.reshape(n, -1), sc.reshape(n, -1), wts, layer)
    o, y, shg_new, sss_new = _sample_state(hgp, ssp, shg_all,
                                           sss_all.reshape(depth, n, SSD_HEADS * SSD_P, SSD_N), layer)
    x1, q8 = _sample_mid(x, ya, o, y, hgp, ssp, wts)
    x2 = _sample_attn(x1, q8, ck_rows, cv_rows, wts, layer)
    return (x2, ca_new.reshape(ca.shape), shg_new, sc_new.reshape(sc.shape), sss_new.reshape(sss_all.shape[1:]))


PROMPT_TILE = 256


def kernel(x_prompt, x_sample, mem_prompt, state_conv_a, state_hgrn, state_ssd_conv, state_ssd, cache_mem_k,
           cache_mem_v, w_in, conv_a_w, hgrn_lb, hgrn_gnorm, ssd_conv_w, ssd_conv_b, ssd_dt_bias, ssd_A_log, ssd_D,
           ssd_norm, w_out, g_pre_mix, g_post_mix, g_pre_x, g_post_x, g_mem, w_q, w_k, w_v, w_o):
    depth = w_in.shape[0]
    b = x_prompt.shape[0]
    n = x_sample.shape[0]
    yp = x_prompt
    ys = x_sample.reshape(n, D_MODEL)
    outs = [[] for _ in range(10)]
    ck_rows, cv_rows = _cache_rows_view(cache_mem_k), _cache_rows_view(cache_mem_v)
    for l in range(depth):
        wts = _layer_weights(l, w_in, conv_a_w, hgrn_lb, hgrn_gnorm, ssd_conv_w, ssd_conv_b, ssd_dt_bias, ssd_A_log,
                             ssd_D, ssd_norm, w_out, g_pre_mix, g_post_mix, g_pre_x, g_post_x, w_q, w_o)
        mk, mv = _memory_kv(mem_prompt, g_mem[l], w_k[l].astype(BF16), w_v[l].astype(BF16))
        yp, ca, hg, sc, ss = _prompt_layer(yp, mk, mv, wts, l, PROMPT_TILE)
        ys, s_ca, s_hg, s_sc, s_ss = _sample_layer(ys, state_conv_a[l], state_hgrn, state_ssd_conv[l],
                                                   state_ssd, ck_rows, cv_rows, wts, l)
        kv_shape = (b, N_MEM, XA_HEADS, XA_HD)
        for lst, val in zip(outs, (ca, hg, sc, ss, mk.reshape(kv_shape), mv.reshape(kv_shape), s_ca, s_hg, s_sc, s_ss)):
            lst.append(val)
    return (yp, ys.reshape(x_sample.shape)) + tuple(jnp.stack(o) for o in outs)
```

```python
import functools

import numpy as np
import jax
import jax.numpy as jnp
from jax import lax
from jax.experimental import pallas as pl
from jax.experimental.pallas import tpu as pltpu

F32 = jnp.float32
BF16 = jnp.bfloat16

D_MODEL = 1024
D_A = 512
CONV_A_W = 3
D_HG = 512
HG_HEADS = 4
HG_DK = 128
D_SSD = 1024
SSD_P = 64
SSD_HEADS = 16
SSD_GROUPS = 2
SSD_N = 128
SSD_CONV_W = 4
SSD_CONV_DIM = D_SSD + 2 * SSD_GROUPS * SSD_N
N_MEM = 256
XA_HEADS = 4
XA_HD = 256
EPS = 1e-6
KV_SPLIT = XA_HD // 128
KV_SUB = XA_HEADS * KV_SPLIT
KV_ROWS = N_MEM * KV_SUB

OFF_A = 0
OFF_G = 2048
OFF_SZ = 4096
OFF_XBC = 5120
OFF_DT = 6656
D_IN = 6672
D_IN_PAD = 6784

CH = 128
HG_LEVELS = (1, 2, 4, 8, 16, 32, 64)
VMEM_LIMIT = 56 * 1024 * 1024


def _rms(x, g):
    ms = jnp.mean(x * x, axis=-1, keepdims=True)
    return x * lax.rsqrt(ms + EPS) * g


def _silu(x):
    return x * (1.0 / (1.0 + jnp.exp(-x)))


def _sigmoid(x):
    return 1.0 / (1.0 + jnp.exp(-x))


def _softplus(x):
    return jnp.maximum(x, 0.0) + jnp.log(1.0 + jnp.exp(-jnp.abs(x)))


def _dot(a, b):
    return jnp.dot(a, b, preferred_element_type=F32)


def _dot_nt(a, b):
    return lax.dot_general(a, b, (((1,), (1,)), ((), ())), preferred_element_type=F32)


def _dot_tn(a, b):
    return lax.dot_general(a, b, (((0,), (0,)), ((), ())), preferred_element_type=F32)


def _split3(x):
    hi = x.astype(BF16)
    r = x - hi.astype(F32)
    mid = r.astype(BF16)
    lo = (r - mid.astype(F32)).astype(BF16)
    return hi, mid, lo


def _split3_rows(x):
    return jnp.concatenate(_split3(x), axis=0)


def _split3_cols(x):
    return jnp.concatenate(_split3(x), axis=1)


@functools.lru_cache(maxsize=None)
def _consts():
    r = np.arange(CH)
    i, t = r[:, None], r[None, :]
    masks = [np.eye(CH, dtype=bool)]
    for s in HG_LEVELS:
        up = ((r // s) % 2 == 1)
        same = (i // (2 * s)) == (t // (2 * s))
        masks.append(same & up[:, None] & (~up)[None, :])
    masks = np.stack(masks).astype(np.float32)
    tril = (t <= i).astype(np.float32)
    tril3 = np.tile(tril, (1, 3))
    triu3 = np.tile(tril.T, (3, 1))
    e = (np.arange(D_SSD)[None, :] // SSD_P == np.arange(SSD_HEADS)[:, None]).astype(np.float32)
    expand3 = np.tile(e, (3, 1))
    lane = np.arange(D_SSD) % (2 * SSD_P)
    pairsel = np.stack([(lane < SSD_P), (lane >= SSD_P)]).astype(np.float32)
    return dict(
        masks=jnp.asarray(masks, F32),
        tril=jnp.asarray(tril, F32), tril3=jnp.asarray(tril3, BF16), triu3=jnp.asarray(triu3, BF16),
        expand3=jnp.asarray(expand3, BF16), pairsel=jnp.asarray(pairsel, F32))


def _kv_rows_view(c_all):
    depth, n = c_all.shape[:2]
    c = c_all.reshape(depth, n, N_MEM, XA_HEADS, KV_SPLIT, 128)
    return jnp.transpose(c, (0, 1, 2, 4, 3, 5)).reshape(depth, n, KV_ROWS, 128)


def _kv_from_rows(r_all):
    depth, n = r_all.shape[:2]
    c = r_all.reshape(depth, n, N_MEM, KV_SPLIT, XA_HEADS, 128)
    return jnp.transpose(c, (0, 1, 2, 4, 3, 5)).reshape(depth, n, N_MEM, XA_HEADS, XA_HD)


def _store_kv_rows(r_ref, x):
    for hd in range(XA_HEADS):
        for k in range(KV_SPLIT):
            r_ref[:, k * XA_HEADS + hd, :] = x[:, hd * XA_HD + k * 128:hd * XA_HD + (k + 1) * 128]


_ANY = pl.BlockSpec(memory_space=pl.ANY)


def _memkv_kernel(mem_ref, g_ref, wk_ref, wv_ref, *refs):
    kr_ref, vr_ref, kb_ref, vb_ref = refs[-4:]
    m = _rms(mem_ref[0], g_ref[...]).astype(BF16)
    for w_ref, r_ref, b_ref in ((wk_ref, kr_ref, kb_ref), (wv_ref, vr_ref, vb_ref)):
        kv = _dot(m, w_ref[...])
        b_ref[0] = kv.astype(BF16)
        _store_kv_rows(r_ref, kv)


def _memory_kv(mem, g_mem, w_k, w_v, layer, depth, prev):
    b = mem.shape[0]
    full = lambda shape: pl.BlockSpec(shape, lambda i: (0,) * len(shape))
    blk = pl.BlockSpec((1, N_MEM, D_MODEL), lambda i: (i, 0, 0))
    rows_blk = pl.BlockSpec((None, None, N_MEM, KV_SUB, 128), lambda i: (layer, i, 0, 0, 0))
    rows_sds = jax.ShapeDtypeStruct((depth, b, N_MEM, KV_SUB, 128), F32)
    extra, extra_specs, aliases = [], [], {}
    if prev is not None:
        extra, extra_specs, aliases = list(prev), [_ANY, _ANY], {4: 0, 5: 1}
    return pl.pallas_call(
        _memkv_kernel,
        grid=(b,),
        in_specs=[blk, full((1, D_MODEL)), full((D_MODEL, D_MODEL)), full((D_MODEL, D_MODEL))] + extra_specs,
        out_specs=[rows_blk, rows_blk, blk, blk],
        out_shape=[rows_sds, rows_sds] + [jax.ShapeDtypeStruct((b, N_MEM, D_MODEL), BF16)] * 2,
        input_output_aliases=aliases,
        compiler_params=pltpu.CompilerParams(dimension_semantics=("arbitrary",), vmem_limit_bytes=VMEM_LIMIT),
        name="memory_kv",
    )(mem, g_mem.reshape(1, D_MODEL), w_k, w_v, *extra)


def _hgrn_lower_bound(lb_all, layer):
    depth = lb_all.shape[0]
    rows = [lb_all[j:j + 1, :] for j in range(depth)]
    mx = functools.reduce(jnp.maximum, rows)
    ex = [jnp.exp(rw - mx) for rw in rows]
    tot = functools.reduce(lambda a, b: a + b, ex)
    acc = jnp.zeros_like(tot)
    for j in range(1, layer + 1):
        acc = acc + ex[j]
    return acc / tot


def _hgrn_level(c, f, q, k, s):
    n, w = c.shape
    if s >= 8:
        nb = n // (2 * s)
        c4, q4, k4 = (a.reshape(nb, 2, s, w) for a in (c, q, k))
        lower, upper = c4[:, 0], c4[:, 1]
        tot = lower[:, s - 1:s, :]
        w_lower = k4[:, 0] * jnp.exp(tot - lower)
        w_upper = q4[:, 1] * jnp.exp(upper)
        wv = jnp.stack([w_lower, w_upper], axis=1).reshape(n, w)
        c_next = jnp.stack([lower, upper + tot], axis=1).reshape(n, w)
        return wv, c_next
    sub = lax.broadcasted_iota(jnp.int32, (1, 8, w), 1)
    c3, f3, q3, k3 = (a.reshape(n // 8, 8, w) for a in (c, f, q, k))
    up = (sub // s) % 2 == 1
    tot = None
    for gi in reversed(range(8 // (2 * s))):
        r = gi * 2 * s + s - 1
        tg = jnp.broadcast_to(c3[:, r:r + 1, :], c3.shape)
        tot = tg if tot is None else jnp.where(sub < (gi + 1) * 2 * s, tg, tot)
    if s == 1:
        e = jnp.where(up, f3, 1.0)
    else:
        e = jnp.exp(jnp.where(up, c3, tot - c3))
    wv = jnp.where(up, q3, k3) * e
    c_next = c3 + jnp.where(up, tot, 0.0)
    return wv.reshape(n, w), c_next.reshape(n, w)


def _cross_attention(q, mk, mv):
    outs = []
    for hd in range(XA_HEADS):
        sl = slice(hd * XA_HD, (hd + 1) * XA_HD)
        s = _dot_nt(q[:, sl].astype(BF16), mk[:, sl]) * (XA_HD ** -0.5)
        s = s - jnp.max(s, axis=-1, keepdims=True)
        e = jnp.exp(s)
        p = e * (1.0 / jnp.sum(e, axis=-1, keepdims=True))
        outs.append(_dot(p.astype(BF16), mv[:, sl]))
    return jnp.concatenate(outs, axis=1)


def _prompt_kernel(x_ref, mk_ref, mv_ref, win_ref, wdtT_ref, wout_ref, wq_ref, wo_ref,
                   caw_ref, lb_ref, gn_ref, scw_ref, scb_ref, dtb_ref, dtbc_ref, al_ref, alc_ref, dx_ref,
                   snorm_ref, gpre_ref, gpost_ref, gprex_ref, gpostx_ref,
                   masks_ref, tril_ref, tril3_ref, triu3_ref, expand_ref, pairsel_ref,
                   *rest, T, layer, n_prev):
    (y_ref, ca_ref, hg_ref, sc_ref, ss_ref,
     bufa, bufc, ug_s, z_s, xbc_s, dt_s, dtT_s, mix_s, sthg, stssd) = rest[n_prev:]
    ti = pl.program_id(1)
    n_chunks = T // CH

    @pl.when(ti == 0)
    def _():
        bufa[0:8, :] = jnp.zeros((8, D_A), F32)
        bufc[0:8, :] = jnp.zeros((8, SSD_CONV_DIM), F32)
        sthg[...] = jnp.zeros(sthg.shape, F32)
        stssd[...] = jnp.zeros(stssd.shape, F32)

    x = x_ref[0]
    h = _rms(x, gpre_ref[...]).astype(BF16)

    ua = _dot(h, win_ref[:, OFF_A:OFF_A + 4 * D_A])
    a_h, a_b, a_c, a_z = (ua[:, k * D_A:(k + 1) * D_A] for k in range(4))
    va = a_c * a_h
    bufa[8:8 + T, :] = va
    caw = caw_ref[...]
    conv = va * caw[2:3, :] + bufa[7:7 + T, :] * caw[1:2, :] + bufa[6:6 + T, :] * caw[0:1, :]
    mix_s[:, 0:D_A] = (a_b * conv * _silu(a_z)).astype(BF16)
    ca_ref[0] = va[T - 2:T, :]
    bufa[0:8, :] = va[T - 8:T, :]

    ug_s[...] = _dot(h, win_ref[:, OFF_G:OFF_G + 4 * D_HG])
    z_s[...] = _dot(h, win_ref[:, OFF_SZ:OFF_SZ + D_SSD])
    sxbc = _dot(h, win_ref[:, OFF_XBC:OFF_XBC + SSD_CONV_DIM])
    bufc[8:8 + T, :] = sxbc
    scw = scw_ref[...]
    xbc = (sxbc * scw[3:4, :] + bufc[7:7 + T, :] * scw[2:3, :] + bufc[6:6 + T, :] * scw[1:2, :]
           + bufc[5:5 + T, :] * scw[0:1, :] + scb_ref[...])
    xbc_s[...] = _silu(xbc)
    sc_ref[0] = sxbc[T - 3:T, :]
    bufc[0:8, :] = sxbc[T - 8:T, :]
    sdt = _dot(h, win_ref[:, OFF_DT:OFF_DT + 128])[:, 0:SSD_HEADS]
    dt_s[...] = _softplus(sdt + dtb_ref[...])
    dtT = _softplus(_dot_nt(wdtT_ref[...], h) + dtbc_ref[...])
    for c in range(n_chunks):
        dtT_s[c] = dtT[:, c * CH:(c + 1) * CH]

    lb = _hgrn_lower_bound(lb_ref[...], layer)
    a_row = -jnp.exp(al_ref[...])
    a_col = -jnp.exp(alc_ref[...])
    tril = tril_ref[...]
    pairsel = pairsel_ref[...]

    def chunk(c, carry):
        r0 = pl.multiple_of(c * CH, CH)
        rows = pl.ds(r0, CH)

        ug = ug_s[rows, :]
        gq, gf, gi, gz = (ug[:, k * D_HG:(k + 1) * D_HG] for k in range(4))
        f = lb + (1.0 - lb) * _sigmoid(gf)
        logf = jnp.log(f)
        kk = 1.0 - f
        q_b, k_b, v_b = gq.astype(BF16), kk.astype(BF16), gi.astype(BF16)
        hs = [slice(hd * HG_DK, (hd + 1) * HG_DK) for hd in range(HG_HEADS)]
        A = [masks_ref[0] * _dot_nt(q_b[:, s_], k_b[:, s_]) for s_ in hs]
        G = logf
        for li, s in enumerate(HG_LEVELS):
            w, G = _hgrn_level(G, f, gq, kk, s)
            w = w.astype(BF16)
            m = masks_ref[li + 1]
            A = [A[hd] + m * _dot_nt(w[:, hs[hd]], w[:, hs[hd]]) for hd in range(HG_HEADS)]
        g_last = G[CH - 1:CH, :]
        qg = (gq * jnp.exp(G)).astype(BF16)
        kd = (kk * jnp.exp(g_last - G)).astype(BF16)
        dec = jnp.exp(g_last)
        o_heads = []
        for hd in range(HG_HEADS):
            s_ = hs[hd]
            st = sthg[hd]
            o = _dot_nt(qg[:, s_], st.astype(BF16)) + _dot(A[hd].astype(BF16), v_b[:, s_])
            sthg[hd] = st * dec[:, s_] + _dot_tn(v_b[:, s_], kd[:, s_])
            o_heads.append(_rms(o, gn_ref[:, s_]))
        yb = jnp.concatenate(o_heads, axis=1) * _silu(gz)
        mix_s[rows, D_A:D_A + D_HG] = yb.astype(BF16)

        xbc_c = xbc_s[rows, :]
        xs = xbc_c[:, 0:D_SSD]
        Bm = xbc_c[:, D_SSD:D_SSD + SSD_GROUPS * SSD_N].astype(BF16)
        Cm = xbc_c[:, D_SSD + SSD_GROUPS * SSD_N:].astype(BF16)
        dt = dt_s[rows, :]
        dtT_c = dtT_s[c]
        cs = _dot(tril3_ref[...], _split3_rows(dt * a_row))
        csT = _dot(_split3_cols(dtT_c * a_col), triu3_ref[...])
        cs_last = cs[CH - 1:CH, :]
        w_all = jnp.concatenate([dt * jnp.exp(cs_last - cs), jnp.exp(cs), dt,
                                 jnp.broadcast_to(jnp.exp(cs_last), (8, SSD_HEADS))], axis=0)
        e_all = _dot(_split3_cols(w_all), expand_ref[...])
        e_dec, e_cs, e_dt, e_last = e_all[0:CH], e_all[CH:2 * CH], e_all[2 * CH:3 * CH], e_all[3 * CH:3 * CH + 1]
        xdt = xs * e_dt
        xdt_lo = (xdt * pairsel[0:1, :]).astype(BF16)
        xdt_hi = (xdt * pairsel[1:2, :]).astype(BF16)
        xdec = (xs * e_dec).astype(BF16)
        y_groups = []
        hpg = SSD_HEADS // SSD_GROUPS
        gw = hpg * SSD_P
        for g in range(SSD_GROUPS):
            Cg = Cm[:, g * SSD_N:(g + 1) * SSD_N]
            Bg = Bm[:, g * SSD_N:(g + 1) * SSD_N]
            cb = _dot_nt(Cg, Bg)
            st = stssd[g]
            gcols = slice(g * gw, (g + 1) * gw)
            y_off = _dot(Cg, st.astype(BF16)) * e_cs[:, gcols]
            stssd[g] = st * e_last[:, gcols] + _dot_tn(Bg, xdec[:, gcols])
            pair_out = []
            for pr in range(hpg // 2):
                h0 = g * hpg + 2 * pr
                ms = []
                for hh in (h0, h0 + 1):
                    diff = cs[:, hh:hh + 1] - csT[hh:hh + 1, :]
                    ms.append((cb * (jnp.exp(jnp.minimum(diff, 0.0)) * tril)).astype(BF16))
                lhs = jnp.concatenate(ms, axis=1)
                pc = slice(h0 * SSD_P, (h0 + 2) * SSD_P)
                rhs = jnp.concatenate([xdt_lo[:, pc], xdt_hi[:, pc]], axis=0)
                pair_out.append(_dot(lhs, rhs))
            y_groups.append(y_off + jnp.concatenate(pair_out, axis=1))
        y = jnp.concatenate(y_groups, axis=1) + dx_ref[...] * xs
        yc = _rms(y * _silu(z_s[rows, :]), snorm_ref[...])
        mix_s[rows, D_A + D_HG:] = yc.astype(BF16)
        return carry

    lax.fori_loop(0, n_chunks, chunk, 0, unroll=True)

    x1 = x + _rms(_dot(mix_s[...], wout_ref[...]), gpost_ref[...])
    hx = _rms(x1, gprex_ref[...]).astype(BF16)
    q = _dot(hx, wq_ref[...])
    att = _cross_attention(q, mk_ref[0], mv_ref[0])
    y_ref[0] = x1 + _rms(_dot(att.astype(BF16), wo_ref[...]), gpostx_ref[...])

    @pl.when(ti == pl.num_programs(1) - 1)
    def _():
        for hd in range(HG_HEADS):
            hg_ref[0, hd] = sthg[hd].T
        hpg = SSD_HEADS // SSD_GROUPS
        for g in range(SSD_GROUPS):
            sg = stssd[g].T
            for hh in range(hpg):
                ss_ref[0, g * hpg + hh] = sg[hh * SSD_P:(hh + 1) * SSD_P, :]


def _prompt_layer(x, mk, mv, wts, layer, depth, prev, T):
    b, L, _ = x.shape
    prev = [] if prev is None else list(prev)
    c = _consts()
    n_chunks = T // CH
    const_names = ("masks", "tril", "tril3", "triu3", "expand3", "pairsel")
    consts = [c[k] for k in const_names]
    small = [wts[k] for k in ("caw", "lb", "gn", "scw", "scb", "dtb", "dtbc", "al", "alc", "dx", "snorm",
                              "gpre", "gpost", "gprex", "gpostx")]
    big = [wts[k] for k in ("win", "wdtT", "wout", "wq", "wo")]

    def full(a):
        nd = a.ndim
        return pl.BlockSpec(a.shape, lambda bi, ti, _n=nd: (0,) * _n, pipeline_mode=pl.Buffered(1))

    in_specs = ([pl.BlockSpec((1, T, D_MODEL), lambda bi, ti: (bi, ti, 0)),
                 pl.BlockSpec((1, N_MEM, D_MODEL), lambda bi, ti: (bi, 0, 0)),
                 pl.BlockSpec((1, N_MEM, D_MODEL), lambda bi, ti: (bi, 0, 0))]
                + [full(a) for a in big] + [full(a) for a in small] + [full(a) for a in consts]
                + [_ANY] * len(prev))
    n_in = len(in_specs)
    state_shapes = [(CONV_A_W - 1, D_A), (HG_HEADS, HG_DK, HG_DK), (SSD_CONV_W - 1, SSD_CONV_DIM),
                    (SSD_HEADS, SSD_P, SSD_N)]
    out_shape = ([jax.ShapeDtypeStruct((b, L, D_MODEL), F32)]
                 + [jax.ShapeDtypeStruct((depth, b) + s, F32) for s in state_shapes])
    out_specs = ([pl.BlockSpec((1, T, D_MODEL), lambda bi, ti: (bi, ti, 0))]
                 + [pl.BlockSpec((None, 1) + s, lambda bi, ti, _n=len(s): (layer, bi) + (0,) * _n)
                    for s in state_shapes])
    aliases = {n_in - len(prev) + k: 1 + k for k in range(len(prev))}
    scratch = [pltpu.VMEM((8 + T, D_A), F32), pltpu.VMEM((8 + T, SSD_CONV_DIM), F32),
               pltpu.VMEM((T, 4 * D_HG), F32), pltpu.VMEM((T, D_SSD), F32), pltpu.VMEM((T, SSD_CONV_DIM), F32),
               pltpu.VMEM((T, SSD_HEADS), F32), pltpu.VMEM((n_chunks, SSD_HEADS, CH), F32),
               pltpu.VMEM((T, 2 * D_MODEL), BF16),
               pltpu.VMEM((HG_HEADS, HG_DK, HG_DK), F32),
               pltpu.VMEM((SSD_GROUPS, SSD_N, (SSD_HEADS // SSD_GROUPS) * SSD_P), F32)]
    return pl.pallas_call(
        functools.partial(_prompt_kernel, T=T, layer=layer, n_prev=len(prev)),
        grid=(b, L // T),
        in_specs=in_specs, out_specs=out_specs, out_shape=out_shape, scratch_shapes=scratch,
        input_output_aliases=aliases,
        compiler_params=pltpu.CompilerParams(dimension_semantics=("arbitrary", "arbitrary"),
                                             vmem_limit_bytes=VMEM_LIMIT),
        name=f"prompt_layer{layer}",
    )(x, mk, mv, *big, *small, *consts, *prev)


def _layer_weights(l, w_in, conv_a_w, hgrn_lb, hgrn_gnorm, ssd_conv_w, ssd_conv_b, ssd_dt_bias, ssd_A_log, ssd_D,
                   ssd_norm, w_out, g_pre_mix, g_post_mix, g_pre_x, g_post_x, w_q, w_o):
    win = jnp.pad(w_in[l].astype(BF16), ((0, 0), (0, D_IN_PAD - D_IN)))
    row = lambda a: a.reshape(1, -1)
    return dict(
        win=win, wdtT=w_in[l][:, OFF_DT:OFF_DT + SSD_HEADS].T.astype(BF16),
        wout=w_out[l].astype(BF16), wq=w_q[l].astype(BF16), wo=w_o[l].astype(BF16),
        caw=conv_a_w[l], lb=hgrn_lb, gn=row(hgrn_gnorm[l]), scw=ssd_conv_w[l], scb=row(ssd_conv_b[l]),
        dtb=row(ssd_dt_bias[l]), dtbc=ssd_dt_bias[l].reshape(-1, 1), al=row(ssd_A_log[l]),
        alc=ssd_A_log[l].reshape(-1, 1), dx=row(jnp.repeat(ssd_D[l], SSD_P)), snorm=row(ssd_norm[l]),
        gpre=row(g_pre_mix[l]), gpost=row(g_post_mix[l]), gprex=row(g_pre_x[l]), gpostx=row(g_post_x[l]))


SB = 8
D_HGP = 4 * D_HG
D_SSP = 4 * D_SSD + 2 * SSD_GROUPS * SSD_N


def _full_spec(a):
    nd = a.ndim
    return pl.BlockSpec(a.shape, lambda *_, _n=nd: (0,) * _n)


def _sample_pre_kernel(x_ref, ca_ref, sc_ref, win_ref, caw_ref, lb_ref, scw_ref, scb_ref, dtb_ref, al_ref,
                       gpre_ref, expand_ref,
                       ya_ref, canew_ref, hgp_ref, ssp_ref, scnew_ref, *, layer):
    h = _rms(x_ref[...], gpre_ref[...]).astype(BF16)
    u = _dot(h, win_ref[...])
    a_h, a_b, a_c, a_z = (u[:, OFF_A + k * D_A:OFF_A + (k + 1) * D_A] for k in range(4))
    va = a_c * a_h
    p0, p1 = ca_ref[:, 0:D_A], ca_ref[:, D_A:2 * D_A]
    caw = caw_ref[...]
    conv = va * caw[2:3, :] + p1 * caw[1:2, :] + p0 * caw[0:1, :]
    ya_ref[...] = a_b * conv * _silu(a_z)
    canew_ref[:, 0:D_A] = p1
    canew_ref[:, D_A:2 * D_A] = va
    lb = _hgrn_lower_bound(lb_ref[...], layer)
    gq, gf, gi, gz = (u[:, OFF_G + k * D_HG:OFF_G + (k + 1) * D_HG] for k in range(4))
    hgp_ref[:, 0:D_HG] = gq
    hgp_ref[:, D_HG:2 * D_HG] = lb + (1.0 - lb) * _sigmoid(gf)
    hgp_ref[:, 2 * D_HG:3 * D_HG] = gi
    hgp_ref[:, 3 * D_HG:] = gz
    sxbc = u[:, OFF_XBC:OFF_XBC + SSD_CONV_DIM]
    W = SSD_CONV_DIM
    q0, q1, q2 = sc_ref[:, 0:W], sc_ref[:, W:2 * W], sc_ref[:, 2 * W:3 * W]
    scw = scw_ref[...]
    xbc = _silu(sxbc * scw[3:4, :] + q2 * scw[2:3, :] + q1 * scw[1:2, :] + q0 * scw[0:1, :] + scb_ref[...])
    scnew_ref[:, 0:W] = q1
    scnew_ref[:, W:2 * W] = q2
    scnew_ref[:, 2 * W:3 * W] = sxbc
    xs = xbc[:, 0:D_SSD]
    dt = _softplus(u[:, OFF_DT:OFF_DT + SSD_HEADS] + dtb_ref[...])
    dec = jnp.exp(dt * -jnp.exp(al_ref[...]))
    n = dt.shape[0]
    e_all = _dot(_split3_cols(jnp.concatenate([dt, dec], axis=0)), expand_ref[...])
    ssp_ref[:, 0:D_SSD] = xs
    ssp_ref[:, D_SSD:2 * D_SSD] = xs * e_all[0:n]
    ssp_ref[:, 2 * D_SSD:3 * D_SSD] = e_all[n:2 * n]
    ssp_ref[:, 3 * D_SSD:4 * D_SSD] = u[:, OFF_SZ:OFF_SZ + D_SSD]
    ssp_ref[:, 4 * D_SSD:] = xbc[:, D_SSD:]


def _sample_pre(x, ca, sc, wts, layer):
    n = x.shape[0]
    args = [x, ca, sc, wts["win"], wts["caw"], wts["lb"], wts["scw"], wts["scb"], wts["dtb"], wts["al"],
            wts["gpre"], _consts()["expand3"]]
    out_shape = [jax.ShapeDtypeStruct((n, D_A), F32), jax.ShapeDtypeStruct((n, 2 * D_A), F32),
                 jax.ShapeDtypeStruct((n, D_HGP), F32), jax.ShapeDtypeStruct((n, D_SSP), F32),
                 jax.ShapeDtypeStruct((n, 3 * SSD_CONV_DIM), F32)]
    return pl.pallas_call(
        functools.partial(_sample_pre_kernel, layer=layer),
        in_specs=[_full_spec(a) for a in args],
        out_specs=[_full_spec(s) for s in out_shape],
        out_shape=out_shape, grid=(1,),
        compiler_params=pltpu.CompilerParams(dimension_semantics=("arbitrary",), vmem_limit_bytes=VMEM_LIMIT),
        name=f"sample_pre{layer}",
    )(*args)


def _pad_rows_T(blk):
    w = blk.shape[1]
    return jnp.concatenate([blk, jnp.zeros((128 - blk.shape[0], w), blk.dtype)], axis=0).T


def _sample_state_kernel(hgp_ref, ssp_ref, shg_ref, sss_ref, *rest):
    o_ref, y_ref, shg_out, sss_out = rest[-4:]
    rid_hg = lax.broadcasted_iota(jnp.int32, (SB, HG_DK), 0)
    for hd in range(HG_HEADS):
        cols = slice(hd * HG_DK, (hd + 1) * HG_DK)
        q_b = hgp_ref[:, cols].astype(BF16)
        fT = _pad_rows_T(hgp_ref[:, D_HG + hd * HG_DK:D_HG + (hd + 1) * HG_DK])
        o = jnp.zeros((SB, HG_DK), F32)
        for j in range(SB):
            fcol = fT[:, j:j + 1]
            vrow = hgp_ref[j:j + 1, 2 * D_HG + hd * HG_DK:2 * D_HG + (hd + 1) * HG_DK]
            s_new = fcol * shg_ref[j, hd] + (1.0 - fcol) * vrow
            shg_out[j, hd] = s_new
            o = jnp.where(rid_hg == j, _dot(q_b, s_new.astype(BF16)), o)
        o_ref[:, cols] = o
    gw = (SSD_HEADS // SSD_GROUPS) * SSD_P
    rid_ss = lax.broadcasted_iota(jnp.int32, (SB, gw), 0)
    xdtT = _pad_rows_T(ssp_ref[:, D_SSD:2 * D_SSD])
    decT = _pad_rows_T(ssp_ref[:, 2 * D_SSD:3 * D_SSD])
    for g in range(SSD_GROUPS):
        rows = slice(g * gw, (g + 1) * gw)
        c_b = ssp_ref[:, 4 * D_SSD + (SSD_GROUPS + g) * SSD_N:4 * D_SSD + (SSD_GROUPS + g + 1) * SSD_N].astype(BF16)
        y = jnp.zeros((SB, gw), F32)
        for j in range(SB):
            brow = ssp_ref[j:j + 1, 4 * D_SSD + g * SSD_N:4 * D_SSD + (g + 1) * SSD_N]
            s_new = decT[rows, j:j + 1] * sss_ref[j, rows, :] + xdtT[rows, j:j + 1] * brow
            sss_out[j, rows, :] = s_new
            y = jnp.where(rid_ss == j, _dot_nt(c_b, s_new.astype(BF16)), y)
        y_ref[:, rows] = y


def _sample_state(hgp, ssp, shg_all, sss_all, layer, prev):
    n = hgp.shape[0]
    prev = [] if prev is None else list(prev)
    rowblk = lambda w: pl.BlockSpec((SB, w), lambda i: (i, 0))
    hg_blk = pl.BlockSpec((None, SB, HG_HEADS, HG_DK, HG_DK), lambda i: (layer, i, 0, 0, 0))
    ss_blk = pl.BlockSpec((None, SB, SSD_HEADS * SSD_P, SSD_N), lambda i: (layer, i, 0, 0))
    return pl.pallas_call(
        _sample_state_kernel,
        grid=(n // SB,),
        in_specs=[rowblk(D_HGP), rowblk(D_SSP), hg_blk, ss_blk] + [_ANY] * len(prev),
        out_specs=[rowblk(D_HG), rowblk(D_SSD), hg_blk, ss_blk],
        out_shape=[jax.ShapeDtypeStruct((n, D_HG), F32), jax.ShapeDtypeStruct((n, D_SSD), F32),
                   jax.ShapeDtypeStruct(shg_all.shape, F32), jax.ShapeDtypeStruct(sss_all.shape, F32)],
        input_output_aliases={4 + k: 2 + k for k in range(len(prev))},
        compiler_params=pltpu.CompilerParams(dimension_semantics=("arbitrary",), vmem_limit_bytes=VMEM_LIMIT),
        name="sample_state",
    )(hgp, ssp, shg_all, sss_all, *prev)


def _sample_mid_kernel(x_ref, ya_ref, o_ref, y_ref, hgp_ref, ssp_ref, wout_ref, wq_ref, gn_ref, dx_ref, snorm_ref,
                       gpost_ref, gprex_ref, x1_ref, q_ref):
    gz = hgp_ref[:, 3 * D_HG:]
    o = o_ref[...]
    yb = jnp.concatenate([_rms(o[:, hd * HG_DK:(hd + 1) * HG_DK], gn_ref[:, hd * HG_DK:(hd + 1) * HG_DK])
                          for hd in range(HG_HEADS)], axis=1) * _silu(gz)
    y = y_ref[...] + dx_ref[...] * ssp_ref[:, 0:D_SSD]
    yc = _rms(y * _silu(ssp_ref[:, 3 * D_SSD:4 * D_SSD]), snorm_ref[...])
    mix = jnp.concatenate([ya_ref[...], yb, yc], axis=1).astype(BF16)
    x1 = x_ref[...] + _rms(_dot(mix, wout_ref[...]), gpost_ref[...])
    x1_ref[...] = x1
    q = _dot(_rms(x1, gprex_ref[...]).astype(BF16), wq_ref[...])
    for hd in range(XA_HEADS):
        for k in range(XA_HD // 128):
            q_ref[:, k * XA_HEADS + hd, :] = q[:, hd * XA_HD + k * 128:hd * XA_HD + (k + 1) * 128]


def _sample_mid(x, ya, o, y, hgp, ssp, wts):
    n = x.shape[0]
    args = [x, ya, o, y, hgp, ssp, wts["wout"], wts["wq"], wts["gn"], wts["dx"], wts["snorm"], wts["gpost"],
            wts["gprex"]]
    out_shape = [jax.ShapeDtypeStruct((n, D_MODEL), F32),
                 jax.ShapeDtypeStruct((n, XA_HEADS * (XA_HD // 128), 128), F32)]
    return pl.pallas_call(
        _sample_mid_kernel, grid=(1,),
        in_specs=[_full_spec(a) for a in args], out_specs=[_full_spec(s) for s in out_shape], out_shape=out_shape,
        compiler_params=pltpu.CompilerParams(dimension_semantics=("arbitrary",), vmem_limit_bytes=VMEM_LIMIT),
        name="sample_mid",
    )(*args)


def _lane_class_reduce(x, op):
    sh = KV_SUB
    while sh < 128:
        x = op(x, pltpu.roll(x, sh, axis=1))
        sh *= 2
    return x


def _sample_attn_kernel(x1_ref, q_ref, k_ref, v_ref, wo_ref, gpostx_ref, x2_ref):
    lane = lax.broadcasted_iota(jnp.int32, (KV_SUB, KV_ROWS), 1)
    sub = lax.broadcasted_iota(jnp.int32, (KV_SUB, KV_ROWS), 0)
    own = ((lane & (KV_SUB - 1)) == sub).astype(F32)
    rid = lax.broadcasted_iota(jnp.int32, (SB, KV_ROWS), 0)
    t_all = jnp.zeros((SB, KV_ROWS), F32)
    for j in range(SB):
        r = _dot_nt(q_ref[j].astype(BF16), k_ref[j].astype(BF16))
        t = jnp.sum(r * own, axis=0, keepdims=True)
        t_all = jnp.where(rid == j, t, t_all)
    n_tiles = KV_ROWS // 128
    lane1 = lax.broadcasted_iota(jnp.int32, (SB, 128), 1)
    piece = (lane1 // XA_HEADS) % KV_SPLIT
    chunks = []
    for c in range(n_tiles):
        x = t_all[:, c * 128:(c + 1) * 128]
        tot = x
        for k in range(1, KV_SPLIT):
            fwd = pltpu.roll(x, 128 - k * XA_HEADS, axis=1)
            bwd = pltpu.roll(x, (KV_SPLIT - k) * XA_HEADS, axis=1)
            tot = tot + jnp.where(piece + k < KV_SPLIT, fwd, bwd)
        chunks.append(tot * (XA_HD ** -0.5))
    mx = _lane_class_reduce(functools.reduce(jnp.maximum, chunks), jnp.maximum)
    es = [jnp.exp(ch - mx) for ch in chunks]
    den = _lane_class_reduce(functools.reduce(lambda a, b: a + b, es), lambda a, b: a + b)
    p_all = jnp.concatenate([e * (1.0 / den) for e in es], axis=1)
    rid_o = lax.broadcasted_iota(jnp.int32, (SB, D_MODEL), 0)
    att = jnp.zeros((SB, D_MODEL), F32)
    for j in range(SB):
        p8 = (own * p_all[j:j + 1, :]).astype(BF16)
        o = _dot(p8, v_ref[j].astype(BF16))
        row = jnp.concatenate([o[k * XA_HEADS + hd:k * XA_HEADS + hd + 1, :]
                               for hd in range(XA_HEADS) for k in range(KV_SPLIT)], axis=1)
        att = jnp.where(rid_o == j, row, att)
    x2_ref[...] = x1_ref[...] + _rms(_dot(att.astype(BF16), wo_ref[...]), gpostx_ref[...])


def _sample_attn(x1, q8, ck_rows, cv_rows, wts, layer):
    n = x1.shape[0]
    rowblk = pl.BlockSpec((SB, D_MODEL), lambda i: (i, 0))
    qblk = pl.BlockSpec((SB, KV_SUB, 128), lambda i: (i, 0, 0))
    kvblk = pl.BlockSpec((None, SB, KV_ROWS, 128), lambda i: (layer, i, 0, 0))
    return pl.pallas_call(
        _sample_attn_kernel,
        grid=(n // SB,),
        in_specs=[rowblk, qblk, kvblk, kvblk, _full_spec(wts["wo"]), _full_spec(wts["gpostx"])],
        out_specs=rowblk,
        out_shape=jax.ShapeDtypeStruct((n, D_MODEL), F32),
        compiler_params=pltpu.CompilerParams(dimension_semantics=("arbitrary",), vmem_limit_bytes=VMEM_LIMIT),
        name="sample_attn",
    )(x1, q8, ck_rows, cv_rows, wts["wo"], wts["gpostx"])


def _sample_layer(x, ca, shg_all, sc, sss_all, ck_rows, cv_rows, wts, layer, prev_states):
    n = x.shape[0]
    ya, ca_new, hgp, ssp, sc_new = _sample_pre(x, ca.reshape(n, -1), sc.reshape(n, -1), wts, layer)
    o, y, shg_new, sss_new = _sample_state(hgp, ssp, shg_all, sss_all, layer, prev_states)
    x1, q8 = _sample_mid(x, ya, o, y, hgp, ssp, wts)
    x2 = _sample_attn(x1, q8, ck_rows, cv_rows, wts, layer)
    return x2, ca_new.reshape(ca.shape), sc_new.reshape(sc.shape), (shg_new, sss_new)


PROMPT_TILE = 256


def kernel(x_prompt, x_sample, mem_prompt, state_conv_a, state_hgrn, state_ssd_conv, state_ssd, cache_mem_k,
           cache_mem_v, w_in, conv_a_w, hgrn_lb, hgrn_gnorm, ssd_conv_w, ssd_conv_b, ssd_dt_bias, ssd_A_log, ssd_D,
           ssd_norm, w_out, g_pre_mix, g_post_mix, g_pre_x, g_post_x, g_mem, w_q, w_k, w_v, w_o):
    depth = w_in.shape[0]
    b = x_prompt.shape[0]
    n = x_sample.shape[0]
    yp = x_prompt
    ys = x_sample.reshape(n, D_MODEL)
    ck_rows, cv_rows = _kv_rows_view(cache_mem_k), _kv_rows_view(cache_mem_v)
    sss_all = state_ssd.reshape(depth, n, SSD_HEADS * SSD_P, SSD_N)
    kv_rows = p_states = s_states = None
    s_ca, s_sc = [], []
    for l in range(depth):
        wts = _layer_weights(l, w_in, conv_a_w, hgrn_lb, hgrn_gnorm, ssd_conv_w, ssd_conv_b, ssd_dt_bias, ssd_A_log,
                             ssd_D, ssd_norm, w_out, g_pre_mix, g_post_mix, g_pre_x, g_post_x, w_q, w_o)
        *kv_rows, mk, mv = _memory_kv(mem_prompt, g_mem[l], w_k[l].astype(BF16), w_v[l].astype(BF16), l, depth,
                                      kv_rows)
        yp, *p_states = _prompt_layer(yp, mk, mv, wts, l, depth, p_states, PROMPT_TILE)
        ys, ca, sc, s_states = _sample_layer(ys, state_conv_a[l], state_hgrn, state_ssd_conv[l], sss_all,
                                             ck_rows, cv_rows, wts, l, s_states)
        s_ca.append(ca)
        s_sc.append(sc)
    p_ca, p_hg, p_sc, p_ss = p_states
    s_hg, s_ss = s_states
    return (yp, ys.reshape(x_sample.shape), p_ca, p_hg, p_sc, p_ss, _kv_from_rows(kv_rows[0]),
            _kv_from_rows(kv_rows[1]), jnp.stack(s_ca), s_hg, jnp.stack(s_sc), s_ss.reshape(state_ssd.shape))
```

```python
import functools

import numpy as np
import jax
import jax.numpy as jnp
from jax import lax
from jax.experimental import pallas as pl
from jax.experimental.pallas import tpu as pltpu

F32 = jnp.float32
BF16 = jnp.bfloat16

D_MODEL = 1024
D_A = 512
CONV_A_W = 3
D_HG = 512
HG_HEADS = 4
HG_DK = 128
D_SSD = 1024
SSD_P = 64
SSD_HEADS = 16
SSD_GROUPS = 2
SSD_N = 128
SSD_CONV_W = 4
SSD_CONV_DIM = D_SSD + 2 * SSD_GROUPS * SSD_N
N_MEM = 256
XA_HEADS = 4
XA_HD = 256
EPS = 1e-6
KV_SPLIT = XA_HD // 128
KV_SUB = XA_HEADS * KV_SPLIT
KV_ROWS = N_MEM * KV_SUB

OFF_A = 0
OFF_G = 2048
OFF_SZ = 4096
OFF_XBC = 5120
OFF_DT = 6656
D_IN = 6672

CH = 128
HG_LEVELS = (1, 2, 4, 8, 16, 32, 64)
VMEM_LIMIT = 56 * 1024 * 1024


def _rms(x, g):
    ms = jnp.mean(x * x, axis=-1, keepdims=True)
    return x * lax.rsqrt(ms + EPS) * g


def _silu(x):
    return x * (1.0 / (1.0 + jnp.exp(-x)))


def _sigmoid(x):
    return 1.0 / (1.0 + jnp.exp(-x))


def _softplus(x):
    return jnp.maximum(x, 0.0) + jnp.log(1.0 + jnp.exp(-jnp.abs(x)))


def _dot(a, b):
    return jnp.dot(a, b, preferred_element_type=F32)


def _dot_nt(a, b):
    return lax.dot_general(a, b, (((1,), (1,)), ((), ())), preferred_element_type=F32)


def _dot_tn(a, b):
    return lax.dot_general(a, b, (((0,), (0,)), ((), ())), preferred_element_type=F32)


def _split3(x):
    hi = x.astype(BF16)
    r = x - hi.astype(F32)
    mid = r.astype(BF16)
    lo = (r - mid.astype(F32)).astype(BF16)
    return hi, mid, lo


def _split3_rows(x):
    return jnp.concatenate(_split3(x), axis=0)


def _split3_cols(x):
    return jnp.concatenate(_split3(x), axis=1)


@functools.lru_cache(maxsize=None)
def _consts():
    r = np.arange(CH)
    i, t = r[:, None], r[None, :]
    masks = [np.eye(CH, dtype=bool)]
    for s in HG_LEVELS:
        up = ((r // s) % 2 == 1)
        same = (i // (2 * s)) == (t // (2 * s))
        masks.append(same & up[:, None] & (~up)[None, :])
    masks = np.stack(masks).astype(np.float32)
    tril = (t <= i).astype(np.float32)
    tril3 = np.tile(tril, (1, 3))
    triu3 = np.tile(tril.T, (3, 1))
    e = (np.arange(D_SSD)[None, :] // SSD_P == np.arange(SSD_HEADS)[:, None]).astype(np.float32)
    expand3 = np.tile(e, (3, 1))
    return dict(
        masks=jnp.asarray(masks, F32),
        tril=jnp.asarray(tril, F32), tril3=jnp.asarray(tril3, BF16), triu3=jnp.asarray(triu3, BF16),
        expand3=jnp.asarray(expand3, BF16))


def _kv_rows_view(c_all):
    depth, n = c_all.shape[:2]
    c = c_all.reshape(depth, n, N_MEM, XA_HEADS, KV_SPLIT, 128)
    return jnp.transpose(c, (0, 1, 2, 4, 3, 5)).reshape(depth, n, KV_ROWS, 128)


def _kv_from_rows(r_all):
    depth, n = r_all.shape[:2]
    c = r_all.reshape(depth, n, N_MEM, KV_SPLIT, XA_HEADS, 128)
    return jnp.transpose(c, (0, 1, 2, 4, 3, 5)).reshape(depth, n, N_MEM, XA_HEADS, XA_HD)


def _store_kv_rows(r_ref, x):
    for hd in range(XA_HEADS):
        for k in range(KV_SPLIT):
            r_ref[:, k * XA_HEADS + hd, :] = x[:, hd * XA_HD + k * 128:hd * XA_HD + (k + 1) * 128]


_ANY = pl.BlockSpec(memory_space=pl.ANY)


def _memkv_kernel(mem_ref, g_ref, wk_ref, wv_ref, *refs):
    kr_ref, vr_ref, kb_ref, vb_ref = refs[-4:]
    m = _rms(mem_ref[0], g_ref[...]).astype(BF16)
    for w_ref, r_ref, b_ref in ((wk_ref, kr_ref, kb_ref), (wv_ref, vr_ref, vb_ref)):
        kv = _dot(m, w_ref[...])
        b_ref[0] = kv.astype(BF16)
        _store_kv_rows(r_ref, kv)


def _memory_kv(mem, g_mem, w_k, w_v, layer, depth, prev):
    b = mem.shape[0]
    full = lambda shape: pl.BlockSpec(shape, lambda i: (0,) * len(shape))
    blk = pl.BlockSpec((1, N_MEM, D_MODEL), lambda i: (i, 0, 0))
    rows_blk = pl.BlockSpec((None, None, N_MEM, KV_SUB, 128), lambda i: (layer, i, 0, 0, 0))
    rows_sds = jax.ShapeDtypeStruct((depth, b, N_MEM, KV_SUB, 128), F32)
    extra, extra_specs, aliases = [], [], {}
    if prev is not None:
        extra, extra_specs, aliases = list(prev), [_ANY, _ANY], {4: 0, 5: 1}
    return pl.pallas_call(
        _memkv_kernel,
        grid=(b,),
        in_specs=[blk, full((1, D_MODEL)), full((D_MODEL, D_MODEL)), full((D_MODEL, D_MODEL))] + extra_specs,
        out_specs=[rows_blk, rows_blk, blk, blk],
        out_shape=[rows_sds, rows_sds] + [jax.ShapeDtypeStruct((b, N_MEM, D_MODEL), BF16)] * 2,
        input_output_aliases=aliases,
        compiler_params=pltpu.CompilerParams(dimension_semantics=("arbitrary",), vmem_limit_bytes=VMEM_LIMIT),
        name="memory_kv",
    )(mem, g_mem.reshape(1, D_MODEL), w_k, w_v, *extra)


def _hgrn_lower_bound(lb_all, layer):
    depth = lb_all.shape[0]
    rows = [lb_all[j:j + 1, :] for j in range(depth)]
    mx = functools.reduce(jnp.maximum, rows)
    ex = [jnp.exp(rw - mx) for rw in rows]
    tot = functools.reduce(lambda a, b: a + b, ex)
    acc = jnp.zeros_like(tot)
    for j in range(1, layer + 1):
        acc = acc + ex[j]
    return acc / tot


def _hgrn_level(c, f, q, k, s):
    n, w = c.shape
    if s >= 8:
        nb = n // (2 * s)
        c4, q4, k4 = (a.reshape(nb, 2, s, w) for a in (c, q, k))
        lower, upper = c4[:, 0], c4[:, 1]
        tot = lower[:, s - 1:s, :]
        w_lower = k4[:, 0] * jnp.exp(tot - lower)
        w_upper = q4[:, 1] * jnp.exp(upper)
        wv = jnp.stack([w_lower, w_upper], axis=1).reshape(n, w)
        c_next = jnp.stack([lower, upper + tot], axis=1).reshape(n, w)
        return wv, c_next
    sub = lax.broadcasted_iota(jnp.int32, (1, 8, w), 1)
    c3, f3, q3, k3 = (a.reshape(n // 8, 8, w) for a in (c, f, q, k))
    up = (sub // s) % 2 == 1
    tot = None
    for gi in reversed(range(8 // (2 * s))):
        r = gi * 2 * s + s - 1
        tg = jnp.broadcast_to(c3[:, r:r + 1, :], c3.shape)
        tot = tg if tot is None else jnp.where(sub < (gi + 1) * 2 * s, tg, tot)
    if s == 1:
        e = jnp.where(up, f3, 1.0)
    else:
        e = jnp.exp(jnp.where(up, c3, tot - c3))
    wv = jnp.where(up, q3, k3) * e
    c_next = c3 + jnp.where(up, tot, 0.0)
    return wv.reshape(n, w), c_next.reshape(n, w)


def _cross_attention(q, mk, mv):
    outs = []
    for hd in range(XA_HEADS):
        sl = slice(hd * XA_HD, (hd + 1) * XA_HD)
        s = _dot_nt(q[:, sl].astype(BF16), mk[:, sl]) * (XA_HD ** -0.5)
        s = s - jnp.max(s, axis=-1, keepdims=True)
        e = jnp.exp(s)
        p = e * (1.0 / jnp.sum(e, axis=-1, keepdims=True))
        outs.append(_dot(p.astype(BF16), mv[:, sl]))
    return jnp.concatenate(outs, axis=1)


def _prompt_kernel(x_ref, mk_ref, mv_ref, win_ref, wdt_ref, wdtT_ref, wout_ref, wq_ref, wo_ref,
                   caw_ref, lb_ref, gn_ref, scw_ref, scb_ref, dtb_ref, dtbc_ref, al_ref, alc_ref, dx_ref,
                   snorm_ref, gpre_ref, gpost_ref, gprex_ref, gpostx_ref,
                   masks_ref, tril_ref, tril3_ref, triu3_ref, expand_ref,
                   *rest, T, layer, n_prev):
    (y_ref, ca_ref, hg_ref, sc_ref, ss_ref,
     bufa, bufc, ug_s, z_s, xbc_s, dt_s, dtT_s, mix_s, sthg, stssd) = rest[n_prev:]
    ti = pl.program_id(1)
    n_chunks = T // CH

    @pl.when(ti == 0)
    def _():
        bufa[0:8, :] = jnp.zeros((8, D_A), F32)
        bufc[0:8, :] = jnp.zeros((8, SSD_CONV_DIM), F32)
        sthg[...] = jnp.zeros(sthg.shape, F32)
        stssd[...] = jnp.zeros(stssd.shape, F32)

    x = x_ref[0]
    h = _rms(x, gpre_ref[...]).astype(BF16)

    ua = _dot(h, win_ref[:, OFF_A:OFF_A + 4 * D_A])
    a_h, a_b, a_c, a_z = (ua[:, k * D_A:(k + 1) * D_A] for k in range(4))
    va = a_c * a_h
    bufa[8:8 + T, :] = va
    caw = caw_ref[...]
    conv = va * caw[2:3, :] + bufa[7:7 + T, :] * caw[1:2, :] + bufa[6:6 + T, :] * caw[0:1, :]
    mix_s[:, 0:D_A] = (a_b * conv * _silu(a_z)).astype(BF16)
    ca_ref[0] = va[T - 2:T, :]
    bufa[0:8, :] = va[T - 8:T, :]

    ug_s[...] = _dot(h, win_ref[:, OFF_G:OFF_G + 4 * D_HG])
    z_s[...] = _dot(h, win_ref[:, OFF_SZ:OFF_SZ + D_SSD])
    sxbc = _dot(h, win_ref[:, OFF_XBC:OFF_XBC + SSD_CONV_DIM])
    bufc[8:8 + T, :] = sxbc
    scw = scw_ref[...]
    xbc = (sxbc * scw[3:4, :] + bufc[7:7 + T, :] * scw[2:3, :] + bufc[6:6 + T, :] * scw[1:2, :]
           + bufc[5:5 + T, :] * scw[0:1, :] + scb_ref[...])
    xbc_s[...] = _silu(xbc)
    sc_ref[0] = sxbc[T - 3:T, :]
    bufc[0:8, :] = sxbc[T - 8:T, :]
    sdt = _dot(h, wdt_ref[...])[:, 0:SSD_HEADS]
    dt_s[...] = _softplus(sdt + dtb_ref[...])
    dtT = _softplus(_dot_nt(wdtT_ref[...], h) + dtbc_ref[...])
    for c in range(n_chunks):
        dtT_s[c] = dtT[:, c * CH:(c + 1) * CH]

    lb = _hgrn_lower_bound(lb_ref[...], layer)
    a_row = -jnp.exp(al_ref[...])
    a_col = -jnp.exp(alc_ref[...])
    tril = tril_ref[...]
    first_of_pair = lax.broadcasted_iota(jnp.int32, (1, 2 * SSD_P), 1) < SSD_P

    def chunk(c, carry):
        r0 = pl.multiple_of(c * CH, CH)
        rows = pl.ds(r0, CH)

        ug = ug_s[rows, :]
        gq, gf, gi, gz = (ug[:, k * D_HG:(k + 1) * D_HG] for k in range(4))
        f = lb + (1.0 - lb) * _sigmoid(gf)
        logf = jnp.log(f)
        kk = 1.0 - f
        q_b, k_b, v_b = gq.astype(BF16), kk.astype(BF16), gi.astype(BF16)
        hs = [slice(hd * HG_DK, (hd + 1) * HG_DK) for hd in range(HG_HEADS)]
        A = [masks_ref[0] * _dot_nt(q_b[:, s_], k_b[:, s_]) for s_ in hs]
        G = logf
        for li, s in enumerate(HG_LEVELS):
            w, G = _hgrn_level(G, f, gq, kk, s)
            w = w.astype(BF16)
            m = masks_ref[li + 1]
            A = [A[hd] + m * _dot_nt(w[:, hs[hd]], w[:, hs[hd]]) for hd in range(HG_HEADS)]
        g_last = G[CH - 1:CH, :]
        qg = (gq * jnp.exp(G)).astype(BF16)
        kd = (kk * jnp.exp(g_last - G)).astype(BF16)
        dec = jnp.exp(g_last)
        o_heads = []
        for hd in range(HG_HEADS):
            s_ = hs[hd]
            st = sthg[hd]
            o = _dot_nt(qg[:, s_], st.astype(BF16)) + _dot(A[hd].astype(BF16), v_b[:, s_])
            sthg[hd] = st * dec[:, s_] + _dot_tn(v_b[:, s_], kd[:, s_])
            o_heads.append(_rms(o, gn_ref[:, s_]))
        yb = jnp.concatenate(o_heads, axis=1) * _silu(gz)
        mix_s[rows, D_A:D_A + D_HG] = yb.astype(BF16)

        xbc_c = xbc_s[rows, :]
        xs = xbc_c[:, 0:D_SSD]
        Bm = xbc_c[:, D_SSD:D_SSD + SSD_GROUPS * SSD_N].astype(BF16)
        Cm = xbc_c[:, D_SSD + SSD_GROUPS * SSD_N:].astype(BF16)
        dt = dt_s[rows, :]
        dtT_c = dtT_s[c]
        cs = _dot(tril3_ref[...], _split3_rows(dt * a_row))
        csT = _dot(_split3_cols(dtT_c * a_col), triu3_ref[...])
        cs_last = cs[CH - 1:CH, :]
        w_all = jnp.concatenate([dt * jnp.exp(cs_last - cs), jnp.exp(cs), dt,
                                 jnp.broadcast_to(jnp.exp(cs_last), (8, SSD_HEADS))], axis=0)
        e_all = _dot(_split3_cols(w_all), expand_ref[...])
        e_dec, e_cs, e_dt, e_last = e_all[0:CH], e_all[CH:2 * CH], e_all[2 * CH:3 * CH], e_all[3 * CH:3 * CH + 1]
        xdt = (xs * e_dt).astype(BF16)
        xdec = (xs * e_dec).astype(BF16)
        y_groups = []
        hpg = SSD_HEADS // SSD_GROUPS
        gw = hpg * SSD_P
        for g in range(SSD_GROUPS):
            Cg = Cm[:, g * SSD_N:(g + 1) * SSD_N]
            Bg = Bm[:, g * SSD_N:(g + 1) * SSD_N]
            cb = _dot_nt(Cg, Bg)
            st = stssd[g]
            gcols = slice(g * gw, (g + 1) * gw)
            y_off = _dot(Cg, st.astype(BF16)) * e_cs[:, gcols]
            stssd[g] = st * e_last[:, gcols] + _dot_tn(Bg, xdec[:, gcols])
            pair_out = []
            for pr in range(hpg // 2):
                h0 = g * hpg + 2 * pr
                ms = []
                for hh in (h0, h0 + 1):
                    diff = cs[:, hh:hh + 1] - csT[hh:hh + 1, :]
                    ms.append((cb * (jnp.exp(jnp.minimum(diff, 0.0)) * tril)).astype(BF16))
                both = _dot(jnp.concatenate(ms, axis=0), xdt[:, h0 * SSD_P:(h0 + 2) * SSD_P])
                pair_out.append(jnp.where(first_of_pair, both[0:CH], both[CH:2 * CH]))
            y_groups.append(y_off + jnp.concatenate(pair_out, axis=1))
        y = jnp.concatenate(y_groups, axis=1) + dx_ref[...] * xs
        yc = _rms(y * _silu(z_s[rows, :]), snorm_ref[...])
        mix_s[rows, D_A + D_HG:] = yc.astype(BF16)
        return carry

    lax.fori_loop(0, n_chunks, chunk, 0, unroll=True)

    x1 = x + _rms(_dot(mix_s[...], wout_ref[...]), gpost_ref[...])
    hx = _rms(x1, gprex_ref[...]).astype(BF16)
    q = _dot(hx, wq_ref[...])
    att = _cross_attention(q, mk_ref[0], mv_ref[0])
    y_ref[0] = x1 + _rms(_dot(att.astype(BF16), wo_ref[...]), gpostx_ref[...])

    @pl.when(ti == pl.num_programs(1) - 1)
    def _():
        for hd in range(HG_HEADS):
            hg_ref[0, hd] = sthg[hd].T
        hpg = SSD_HEADS // SSD_GROUPS
        for g in range(SSD_GROUPS):
            sg = stssd[g].T
            for hh in range(hpg):
                ss_ref[0, g * hpg + hh] = sg[hh * SSD_P:(hh + 1) * SSD_P, :]


def _prompt_layer(x, mk, mv, wts, layer, depth, prev, T):
    b, L, _ = x.shape
    prev = [] if prev is None else list(prev)
    c = _consts()
    n_chunks = T // CH
    const_names = ("masks", "tril", "tril3", "triu3", "expand3")
    consts = [c[k] for k in const_names]
    small = [wts[k] for k in ("caw", "lb", "gn", "scw", "scb", "dtb", "dtbc", "al", "alc", "dx", "snorm",
                              "gpre", "gpost", "gprex", "gpostx")]
    big = [wts[k] for k in ("win", "wdt", "wdtT", "wout", "wq", "wo")]

    def full(a):
        nd = a.ndim
        return pl.BlockSpec(a.shape, lambda bi, ti, _n=nd: (0,) * _n, pipeline_mode=pl.Buffered(1))

    in_specs = ([pl.BlockSpec((1, T, D_MODEL), lambda bi, ti: (bi, ti, 0)),
                 pl.BlockSpec((1, N_MEM, D_MODEL), lambda bi, ti: (bi, 0, 0)),
                 pl.BlockSpec((1, N_MEM, D_MODEL), lambda bi, ti: (bi, 0, 0))]
                + [full(a) for a in big] + [full(a) for a in small] + [full(a) for a in consts]
                + [_ANY] * len(prev))
    n_in = len(in_specs)
    state_shapes = [(CONV_A_W - 1, D_A), (HG_HEADS, HG_DK, HG_DK), (SSD_CONV_W - 1, SSD_CONV_DIM),
                    (SSD_HEADS, SSD_P, SSD_N)]
    out_shape = ([jax.ShapeDtypeStruct((b, L, D_MODEL), F32)]
                 + [jax.ShapeDtypeStruct((depth, b) + s, F32) for s in state_shapes])
    out_specs = ([pl.BlockSpec((1, T, D_MODEL), lambda bi, ti: (bi, ti, 0))]
                 + [pl.BlockSpec((None, 1) + s, lambda bi, ti, _n=len(s): (layer, bi) + (0,) * _n)
                    for s in state_shapes])
    aliases = {n_in - len(prev) + k: 1 + k for k in range(len(prev))}
    scratch = [pltpu.VMEM((8 + T, D_A), F32), pltpu.VMEM((8 + T, SSD_CONV_DIM), F32),
               pltpu.VMEM((T, 4 * D_HG), F32), pltpu.VMEM((T, D_SSD), F32), pltpu.VMEM((T, SSD_CONV_DIM), F32),
               pltpu.VMEM((T, SSD_HEADS), F32), pltpu.VMEM((n_chunks, SSD_HEADS, CH), F32),
               pltpu.VMEM((T, 2 * D_MODEL), BF16),
               pltpu.VMEM((HG_HEADS, HG_DK, HG_DK), F32),
               pltpu.VMEM((SSD_GROUPS, SSD_N, (SSD_HEADS // SSD_GROUPS) * SSD_P), F32)]
    return pl.pallas_call(
        functools.partial(_prompt_kernel, T=T, layer=layer, n_prev=len(prev)),
        grid=(b, L // T),
        in_specs=in_specs, out_specs=out_specs, out_shape=out_shape, scratch_shapes=scratch,
        input_output_aliases=aliases,
        compiler_params=pltpu.CompilerParams(dimension_semantics=("arbitrary", "arbitrary"),
                                             vmem_limit_bytes=VMEM_LIMIT),
        name=f"prompt_layer{layer}",
    )(x, mk, mv, *big, *small, *consts, *prev)


def _layer_weights(l, w_in, conv_a_w, hgrn_lb, hgrn_gnorm, ssd_conv_w, ssd_conv_b, ssd_dt_bias, ssd_A_log, ssd_D,
                   ssd_norm, w_out, g_pre_mix, g_post_mix, g_pre_x, g_post_x, w_q, w_o):
    wdt = jnp.pad(w_in[l][:, OFF_DT:].astype(BF16), ((0, 0), (0, 128 - SSD_HEADS)))
    row = lambda a: a.reshape(1, -1)
    return dict(
        win=w_in[l].astype(BF16), wdt=wdt, wdtT=w_in[l][:, OFF_DT:OFF_DT + SSD_HEADS].T.astype(BF16),
        wout=w_out[l].astype(BF16), wq=w_q[l].astype(BF16), wo=w_o[l].astype(BF16),
        caw=conv_a_w[l], lb=hgrn_lb, gn=row(hgrn_gnorm[l]), scw=ssd_conv_w[l], scb=row(ssd_conv_b[l]),
        dtb=row(ssd_dt_bias[l]), dtbc=ssd_dt_bias[l].reshape(-1, 1), al=row(ssd_A_log[l]),
        alc=ssd_A_log[l].reshape(-1, 1), dx=row(jnp.repeat(ssd_D[l], SSD_P)), snorm=row(ssd_norm[l]),
        gpre=row(g_pre_mix[l]), gpost=row(g_post_mix[l]), gprex=row(g_pre_x[l]), gpostx=row(g_post_x[l]))


SB = 8
D_HGP = 4 * D_HG
D_SSP = 4 * D_SSD + 2 * SSD_GROUPS * SSD_N


def _full_spec(a):
    nd = a.ndim
    return pl.BlockSpec(a.shape, lambda *_, _n=nd: (0,) * _n)


def _sample_pre_kernel(x_ref, ca_ref, sc_ref, win_ref, caw_ref, lb_ref, scw_ref, scb_ref, dtb_ref, al_ref,
                       gpre_ref, expand_ref,
                       ya_ref, canew_ref, hgp_ref, ssp_ref, scnew_ref, *, layer):
    h = _rms(x_ref[...], gpre_ref[...]).astype(BF16)
    u = _dot(h, win_ref[...])
    a_h, a_b, a_c, a_z = (u[:, OFF_A + k * D_A:OFF_A + (k + 1) * D_A] for k in range(4))
    va = a_c * a_h
    p0, p1 = ca_ref[:, 0:D_A], ca_ref[:, D_A:2 * D_A]
    caw = caw_ref[...]
    conv = va * caw[2:3, :] + p1 * caw[1:2, :] + p0 * caw[0:1, :]
    ya_ref[...] = a_b * conv * _silu(a_z)
    canew_ref[:, 0:D_A] = p1
    canew_ref[:, D_A:2 * D_A] = va
    lb = _hgrn_lower_bound(lb_ref[...], layer)
    gq, gf, gi, gz = (u[:, OFF_G + k * D_HG:OFF_G + (k + 1) * D_HG] for k in range(4))
    hgp_ref[:, 0:D_HG] = gq
    hgp_ref[:, D_HG:2 * D_HG] = lb + (1.0 - lb) * _sigmoid(gf)
    hgp_ref[:, 2 * D_HG:3 * D_HG] = gi
    hgp_ref[:, 3 * D_HG:] = gz
    sxbc = u[:, OFF_XBC:OFF_XBC + SSD_CONV_DIM]
    W = SSD_CONV_DIM
    q0, q1, q2 = sc_ref[:, 0:W], sc_ref[:, W:2 * W], sc_ref[:, 2 * W:3 * W]
    scw = scw_ref[...]
    xbc = _silu(sxbc * scw[3:4, :] + q2 * scw[2:3, :] + q1 * scw[1:2, :] + q0 * scw[0:1, :] + scb_ref[...])
    scnew_ref[:, 0:W] = q1
    scnew_ref[:, W:2 * W] = q2
    scnew_ref[:, 2 * W:3 * W] = sxbc
    xs = xbc[:, 0:D_SSD]
    dt = _softplus(u[:, OFF_DT:OFF_DT + SSD_HEADS] + dtb_ref[...])
    dec = jnp.exp(dt * -jnp.exp(al_ref[...]))
    n = dt.shape[0]
    e_all = _dot(_split3_cols(jnp.concatenate([dt, dec], axis=0)), expand_ref[...])
    ssp_ref[:, 0:D_SSD] = xs
    ssp_ref[:, D_SSD:2 * D_SSD] = xs * e_all[0:n]
    ssp_ref[:, 2 * D_SSD:3 * D_SSD] = e_all[n:2 * n]
    ssp_ref[:, 3 * D_SSD:4 * D_SSD] = u[:, OFF_SZ:OFF_SZ + D_SSD]
    ssp_ref[:, 4 * D_SSD:] = xbc[:, D_SSD:]


def _sample_pre(x, ca, sc, wts, layer):
    n = x.shape[0]
    args = [x, ca, sc, wts["win"], wts["caw"], wts["lb"], wts["scw"], wts["scb"], wts["dtb"], wts["al"],
            wts["gpre"], _consts()["expand3"]]
    out_shape = [jax.ShapeDtypeStruct((n, D_A), F32), jax.ShapeDtypeStruct((n, 2 * D_A), F32),
                 jax.ShapeDtypeStruct((n, D_HGP), F32), jax.ShapeDtypeStruct((n, D_SSP), F32),
                 jax.ShapeDtypeStruct((n, 3 * SSD_CONV_DIM), F32)]
    return pl.pallas_call(
        functools.partial(_sample_pre_kernel, layer=layer),
        in_specs=[_full_spec(a) for a in args],
        out_specs=[_full_spec(s) for s in out_shape],
        out_shape=out_shape, grid=(1,),
        compiler_params=pltpu.CompilerParams(dimension_semantics=("arbitrary",), vmem_limit_bytes=VMEM_LIMIT),
        name=f"sample_pre{layer}",
    )(*args)


def _pad_rows_T(blk):
    w = blk.shape[1]
    return jnp.concatenate([blk, jnp.zeros((128 - blk.shape[0], w), blk.dtype)], axis=0).T


def _sample_state_kernel(hgp_ref, ssp_ref, shg_ref, sss_ref, *rest):
    o_ref, y_ref, shg_out, sss_out = rest[-4:]
    rid_hg = lax.broadcasted_iota(jnp.int32, (SB, HG_DK), 0)
    for hd in range(HG_HEADS):
        cols = slice(hd * HG_DK, (hd + 1) * HG_DK)
        q_b = hgp_ref[:, cols].astype(BF16)
        fT = _pad_rows_T(hgp_ref[:, D_HG + hd * HG_DK:D_HG + (hd + 1) * HG_DK])
        o = jnp.zeros((SB, HG_DK), F32)
        for j in range(SB):
            fcol = fT[:, j:j + 1]
            vrow = hgp_ref[j:j + 1, 2 * D_HG + hd * HG_DK:2 * D_HG + (hd + 1) * HG_DK]
            s_new = fcol * shg_ref[j, hd] + (1.0 - fcol) * vrow
            shg_out[j, hd] = s_new
            o = jnp.where(rid_hg == j, _dot(q_b, s_new.astype(BF16)), o)
        o_ref[:, cols] = o
    gw = (SSD_HEADS // SSD_GROUPS) * SSD_P
    rid_ss = lax.broadcasted_iota(jnp.int32, (SB, gw), 0)
    xdtT = _pad_rows_T(ssp_ref[:, D_SSD:2 * D_SSD])
    decT = _pad_rows_T(ssp_ref[:, 2 * D_SSD:3 * D_SSD])
    for g in range(SSD_GROUPS):
        rows = slice(g * gw, (g + 1) * gw)
        c_b = ssp_ref[:, 4 * D_SSD + (SSD_GROUPS + g) * SSD_N:4 * D_SSD + (SSD_GROUPS + g + 1) * SSD_N].astype(BF16)
        y = jnp.zeros((SB, gw), F32)
        for j in range(SB):
            brow = ssp_ref[j:j + 1, 4 * D_SSD + g * SSD_N:4 * D_SSD + (g + 1) * SSD_N]
            s_new = decT[rows, j:j + 1] * sss_ref[j, rows, :] + xdtT[rows, j:j + 1] * brow
            sss_out[j, rows, :] = s_new
            y = jnp.where(rid_ss == j, _dot_nt(c_b, s_new.astype(BF16)), y)
        y_ref[:, rows] = y


def _sample_state(hgp, ssp, shg_all, sss_all, layer, prev):
    n = hgp.shape[0]
    prev = [] if prev is None else list(prev)
    rowblk = lambda w: pl.BlockSpec((SB, w), lambda i: (i, 0))
    hg_blk = pl.BlockSpec((None, SB, HG_HEADS, HG_DK, HG_DK), lambda i: (layer, i, 0, 0, 0))
    ss_blk = pl.BlockSpec((None, SB, SSD_HEADS * SSD_P, SSD_N), lambda i: (layer, i, 0, 0))
    return pl.pallas_call(
        _sample_state_kernel,
        grid=(n // SB,),
        in_specs=[rowblk(D_HGP), rowblk(D_SSP), hg_blk, ss_blk] + [_ANY] * len(prev),
        out_specs=[rowblk(D_HG), rowblk(D_SSD), hg_blk, ss_blk],
        out_shape=[jax.ShapeDtypeStruct((n, D_HG), F32), jax.ShapeDtypeStruct((n, D_SSD), F32),
                   jax.ShapeDtypeStruct(shg_all.shape, F32), jax.ShapeDtypeStruct(sss_all.shape, F32)],
        input_output_aliases={4 + k: 2 + k for k in range(len(prev))},
        compiler_params=pltpu.CompilerParams(dimension_semantics=("arbitrary",), vmem_limit_bytes=VMEM_LIMIT),
        name="sample_state",
    )(hgp, ssp, shg_all, sss_all, *prev)


def _sample_mid_kernel(x_ref, ya_ref, o_ref, y_ref, hgp_ref, ssp_ref, wout_ref, wq_ref, gn_ref, dx_ref, snorm_ref,
                       gpost_ref, gprex_ref, x1_ref, q_ref):
    gz = hgp_ref[:, 3 * D_HG:]
    o = o_ref[...]
    yb = jnp.concatenate([_rms(o[:, hd * HG_DK:(hd + 1) * HG_DK], gn_ref[:, hd * HG_DK:(hd + 1) * HG_DK])
                          for hd in range(HG_HEADS)], axis=1) * _silu(gz)
    y = y_ref[...] + dx_ref[...] * ssp_ref[:, 0:D_SSD]
    yc = _rms(y * _silu(ssp_ref[:, 3 * D_SSD:4 * D_SSD]), snorm_ref[...])
    mix = jnp.concatenate([ya_ref[...], yb, yc], axis=1).astype(BF16)
    x1 = x_ref[...] + _rms(_dot(mix, wout_ref[...]), gpost_ref[...])
    x1_ref[...] = x1
    q = _dot(_rms(x1, gprex_ref[...]).astype(BF16), wq_ref[...])
    for hd in range(XA_HEADS):
        for k in range(XA_HD // 128):
            q_ref[:, k * XA_HEADS + hd, :] = q[:, hd * XA_HD + k * 128:hd * XA_HD + (k + 1) * 128]


def _sample_mid(x, ya, o, y, hgp, ssp, wts):
    n = x.shape[0]
    args = [x, ya, o, y, hgp, ssp, wts["wout"], wts["wq"], wts["gn"], wts["dx"], wts["snorm"], wts["gpost"],
            wts["gprex"]]
    out_shape = [jax.ShapeDtypeStruct((n, D_MODEL), F32),
                 jax.ShapeDtypeStruct((n, XA_HEADS * (XA_HD // 128), 128), F32)]
    return pl.pallas_call(
        _sample_mid_kernel, grid=(1,),
        in_specs=[_full_spec(a) for a in args], out_specs=[_full_spec(s) for s in out_shape], out_shape=out_shape,
        compiler_params=pltpu.CompilerParams(dimension_semantics=("arbitrary",), vmem_limit_bytes=VMEM_LIMIT),
        name="sample_mid",
    )(*args)


def _lane_class_reduce(x, op):
    sh = KV_SUB
    while sh < 128:
        x = op(x, pltpu.roll(x, sh, axis=1))
        sh *= 2
    return x


def _sample_attn_kernel(x1_ref, q_ref, k_ref, v_ref, wo_ref, gpostx_ref, x2_ref):
    lane = lax.broadcasted_iota(jnp.int32, (KV_SUB, KV_ROWS), 1)
    sub = lax.broadcasted_iota(jnp.int32, (KV_SUB, KV_ROWS), 0)
    own = ((lane & (KV_SUB - 1)) == sub).astype(F32)
    rid = lax.broadcasted_iota(jnp.int32, (SB, KV_ROWS), 0)
    t_all = jnp.zeros((SB, KV_ROWS), F32)
    for j in range(SB):
        r = _dot_nt(q_ref[j].astype(BF16), k_ref[j].astype(BF16))
        t = jnp.sum(r * own, axis=0, keepdims=True)
        t_all = jnp.where(rid == j, t, t_all)
    n_tiles = KV_ROWS // 128
    lane1 = lax.broadcasted_iota(jnp.int32, (SB, 128), 1)
    piece = (lane1 // XA_HEADS) % KV_SPLIT
    chunks = []
    for c in range(n_tiles):
        x = t_all[:, c * 128:(c + 1) * 128]
        tot = x
        for k in range(1, KV_SPLIT):
            fwd = pltpu.roll(x, 128 - k * XA_HEADS, axis=1)
            bwd = pltpu.roll(x, (KV_SPLIT - k) * XA_HEADS, axis=1)
            tot = tot + jnp.where(piece + k < KV_SPLIT, fwd, bwd)
        chunks.append(tot * (XA_HD ** -0.5))
    mx = _lane_class_reduce(functools.reduce(jnp.maximum, chunks), jnp.maximum)
    es = [jnp.exp(ch - mx) for ch in chunks]
    den = _lane_class_reduce(functools.reduce(lambda a, b: a + b, es), lambda a, b: a + b)
    p_all = jnp.concatenate([e * (1.0 / den) for e in es], axis=1)
    rid_o = lax.broadcasted_iota(jnp.int32, (SB, D_MODEL), 0)
    att = jnp.zeros((SB, D_MODEL), F32)
    for j in range(SB):
        p8 = (own * p_all[j:j + 1, :]).astype(BF16)
        o = _dot(p8, v_ref[j].astype(BF16))
        row = jnp.concatenate([o[k * XA_HEADS + hd:k * XA_HEADS + hd + 1, :]
                               for hd in range(XA_HEADS) for k in range(KV_SPLIT)], axis=1)
        att = jnp.where(rid_o == j, row, att)
    x2_ref[...] = x1_ref[...] + _rms(_dot(att.astype(BF16), wo_ref[...]), gpostx_ref[...])


def _sample_attn(x1, q8, ck_rows, cv_rows, wts, layer):
    n = x1.shape[0]
    rowblk = pl.BlockSpec((SB, D_MODEL), lambda i: (i, 0))
    qblk = pl.BlockSpec((SB, KV_SUB, 128), lambda i: (i, 0, 0))
    kvblk = pl.BlockSpec((None, SB, KV_ROWS, 128), lambda i: (layer, i, 0, 0))
    return pl.pallas_call(
        _sample_attn_kernel,
        grid=(n // SB,),
        in_specs=[rowblk, qblk, kvblk, kvblk, _full_spec(wts["wo"]), _full_spec(wts["gpostx"])],
        out_specs=rowblk,
        out_shape=jax.ShapeDtypeStruct((n, D_MODEL), F32),
        compiler_params=pltpu.CompilerParams(dimension_semantics=("arbitrary",), vmem_limit_bytes=VMEM_LIMIT),
        name="sample_attn",
    )(x1, q8, ck_rows, cv_rows, wts["wo"], wts["gpostx"])


def _sample_layer(x, ca, shg_all, sc, sss_all, ck_rows, cv_rows, wts, layer, prev_states):
    n = x.shape[0]
    ya, ca_new, hgp, ssp, sc_new = _sample_pre(x, ca.reshape(n, -1), sc.reshape(n, -1), wts, layer)
    o, y, shg_new, sss_new = _sample_state(hgp, ssp, shg_all, sss_all, layer, prev_states)
    x1, q8 = _sample_mid(x, ya, o, y, hgp, ssp, wts)
    x2 = _sample_attn(x1, q8, ck_rows, cv_rows, wts, layer)
    return x2, ca_new.reshape(ca.shape), sc_new.reshape(sc.shape), (shg_new, sss_new)


PROMPT_TILE = 256


def kernel(x_prompt, x_sample, mem_prompt, state_conv_a, state_hgrn, state_ssd_conv, state_ssd, cache_mem_k,
           cache_mem_v, w_in, conv_a_w, hgrn_lb, hgrn_gnorm, ssd_conv_w, ssd_conv_b, ssd_dt_bias, ssd_A_log, ssd_D,
           ssd_norm, w_out, g_pre_mix, g_post_mix, g_pre_x, g_post_x, g_mem, w_q, w_k, w_v, w_o):
    depth = w_in.shape[0]
    n = x_sample.shape[0]
    yp = x_prompt
    ys = x_sample.reshape(n, D_MODEL)
    ck_rows, cv_rows = _kv_rows_view(cache_mem_k), _kv_rows_view(cache_mem_v)
    sss_all = state_ssd.reshape(depth, n, SSD_HEADS * SSD_P, SSD_N)
    kv_rows = p_states = s_states = None
    s_ca, s_sc = [], []
    for l in range(depth):
        wts = _layer_weights(l, w_in, conv_a_w, hgrn_lb, hgrn_gnorm, ssd_conv_w, ssd_conv_b, ssd_dt_bias, ssd_A_log,
                             ssd_D, ssd_norm, w_out, g_pre_mix, g_post_mix, g_pre_x, g_post_x, w_q, w_o)
        *kv_rows, mk, mv = _memory_kv(mem_prompt, g_mem[l], w_k[l].astype(BF16), w_v[l].astype(BF16), l, depth,
                                      kv_rows)
        yp, *p_states = _prompt_layer(yp, mk, mv, wts, l, depth, p_states, PROMPT_TILE)
        ys, ca, sc, s_states = _sample_layer(ys, state_conv_a[l], state_hgrn, state_ssd_conv[l], sss_all,
                                             ck_rows, cv_rows, wts, l, s_states)
        s_ca.append(ca)
        s_sc.append(sc)
    p_ca, p_hg, p_sc, p_ss = p_states
    s_hg, s_ss = s_states
    return (yp, ys.reshape(x_sample.shape), p_ca, p_hg, p_sc, p_ss, _kv_from_rows(kv_rows[0]),
            _kv_from_rows(kv_rows[1]), jnp.stack(s_ca), s_hg, jnp.stack(s_sc), s_ss.reshape(state_ssd.shape))
```

```python
import functools

import numpy as np
import jax
import jax.numpy as jnp
from jax import lax
from jax.experimental import pallas as pl
from jax.experimental.pallas import tpu as pltpu

F32 = jnp.float32
BF16 = jnp.bfloat16

D_MODEL = 1024
D_A = 512
CONV_A_W = 3
D_HG = 512
HG_HEADS = 4
HG_DK = 128
D_SSD = 1024
SSD_P = 64
SSD_HEADS = 16
SSD_GROUPS = 2
SSD_N = 128
SSD_CONV_W = 4
SSD_CONV_DIM = D_SSD + 2 * SSD_GROUPS * SSD_N
N_MEM = 256
XA_HEADS = 4
XA_HD = 256
EPS = 1e-6
KV_SPLIT = XA_HD // 128
KV_SUB = XA_HEADS * KV_SPLIT
KV_ROWS = N_MEM * KV_SUB

OFF_A = 0
OFF_G = 2048
OFF_SZ = 4096
OFF_XBC = 5120
OFF_DT = 6656
D_IN = 6672

CH = 128
HG_LEVELS = (1, 2, 4, 8, 16, 32, 64)
VMEM_LIMIT = 56 * 1024 * 1024


def _rms(x, g):
    ms = jnp.mean(x * x, axis=-1, keepdims=True)
    return x * lax.rsqrt(ms + EPS) * g


def _silu(x):
    return x * (1.0 / (1.0 + jnp.exp(-x)))


def _sigmoid(x):
    return 1.0 / (1.0 + jnp.exp(-x))


def _softplus(x):
    return jnp.maximum(x, 0.0) + jnp.log(1.0 + jnp.exp(-jnp.abs(x)))


def _dot(a, b):
    return jnp.dot(a, b, preferred_element_type=F32)


def _dot_nt(a, b):
    return lax.dot_general(a, b, (((1,), (1,)), ((), ())), preferred_element_type=F32)


def _dot_tn(a, b):
    return lax.dot_general(a, b, (((0,), (0,)), ((), ())), preferred_element_type=F32)


def _split3(x):
    hi = x.astype(BF16)
    r = x - hi.astype(F32)
    mid = r.astype(BF16)
    lo = (r - mid.astype(F32)).astype(BF16)
    return hi, mid, lo


def _split3_rows(x):
    return jnp.concatenate(_split3(x), axis=0)


def _split3_cols(x):
    return jnp.concatenate(_split3(x), axis=1)


@functools.lru_cache(maxsize=None)
def _consts():
    r = np.arange(CH)
    i, t = r[:, None], r[None, :]
    masks = [np.eye(CH, dtype=bool)]
    for s in HG_LEVELS:
        up = ((r // s) % 2 == 1)
        same = (i // (2 * s)) == (t // (2 * s))
        masks.append(same & up[:, None] & (~up)[None, :])
    masks = np.stack(masks).astype(np.float32)
    tril = (t <= i).astype(np.float32)
    tril3 = np.tile(tril, (1, 3))
    triu3 = np.tile(tril.T, (3, 1))
    e = (np.arange(D_SSD)[None, :] // SSD_P == np.arange(SSD_HEADS)[:, None]).astype(np.float32)
    expand3 = np.tile(e, (3, 1))
    return dict(
        masks=jnp.asarray(masks, F32),
        tril=jnp.asarray(tril, F32), tril3=jnp.asarray(tril3, BF16), triu3=jnp.asarray(triu3, BF16),
        expand3=jnp.asarray(expand3, BF16))


def _kv_rows_view(c_all):
    depth, n = c_all.shape[:2]
    c = c_all.reshape(depth, n, N_MEM, XA_HEADS, KV_SPLIT, 128)
    return jnp.transpose(c, (0, 1, 2, 4, 3, 5)).reshape(depth, n, KV_ROWS, 128)


def _kv_from_rows(r_all):
    depth, n = r_all.shape[:2]
    c = r_all.reshape(depth, n, N_MEM, KV_SPLIT, XA_HEADS, 128)
    return jnp.transpose(c, (0, 1, 2, 4, 3, 5)).reshape(depth, n, N_MEM, XA_HEADS, XA_HD)


def _store_kv_rows(r_ref, x):
    for hd in range(XA_HEADS):
        for k in range(KV_SPLIT):
            r_ref[:, k * XA_HEADS + hd, :] = x[:, hd * XA_HD + k * 128:hd * XA_HD + (k + 1) * 128]


_ANY = pl.BlockSpec(memory_space=pl.ANY)


def _memkv_kernel(mem_ref, g_ref, wk_ref, wv_ref, *refs):
    kr_ref, vr_ref, kb_ref, vb_ref = refs[-4:]
    m = _rms(mem_ref[0], g_ref[...]).astype(BF16)
    for w_ref, r_ref, b_ref in ((wk_ref, kr_ref, kb_ref), (wv_ref, vr_ref, vb_ref)):
        kv = _dot(m, w_ref[...])
        b_ref[0] = kv.astype(BF16)
        _store_kv_rows(r_ref, kv)


def _memory_kv(mem, g_mem, w_k, w_v, layer, depth, prev):
    b = mem.shape[0]
    full = lambda shape: pl.BlockSpec(shape, lambda i: (0,) * len(shape))
    blk = pl.BlockSpec((1, N_MEM, D_MODEL), lambda i: (i, 0, 0))
    rows_blk = pl.BlockSpec((None, None, N_MEM, KV_SUB, 128), lambda i: (layer, i, 0, 0, 0))
    rows_sds = jax.ShapeDtypeStruct((depth, b, N_MEM, KV_SUB, 128), F32)
    extra, extra_specs, aliases = [], [], {}
    if prev is not None:
        extra, extra_specs, aliases = list(prev), [_ANY, _ANY], {4: 0, 5: 1}
    return pl.pallas_call(
        _memkv_kernel,
        grid=(b,),
        in_specs=[blk, full((1, D_MODEL)), full((D_MODEL, D_MODEL)), full((D_MODEL, D_MODEL))] + extra_specs,
        out_specs=[rows_blk, rows_blk, blk, blk],
        out_shape=[rows_sds, rows_sds] + [jax.ShapeDtypeStruct((b, N_MEM, D_MODEL), BF16)] * 2,
        input_output_aliases=aliases,
        compiler_params=pltpu.CompilerParams(dimension_semantics=("arbitrary",), vmem_limit_bytes=VMEM_LIMIT),
        name="memory_kv",
    )(mem, g_mem.reshape(1, D_MODEL), w_k, w_v, *extra)


def _hgrn_lower_bound(lb_all, layer):
    depth = lb_all.shape[0]
    rows = [lb_all[j:j + 1, :] for j in range(depth)]
    mx = functools.reduce(jnp.maximum, rows)
    ex = [jnp.exp(rw - mx) for rw in rows]
    tot = functools.reduce(lambda a, b: a + b, ex)
    acc = jnp.zeros_like(tot)
    for j in range(1, layer + 1):
        acc = acc + ex[j]
    return acc / tot


def _hgrn_level(c, f, q, k, s):
    n, w = c.shape
    if s >= 8:
        nb = n // (2 * s)
        c4, q4, k4 = (a.reshape(nb, 2, s, w) for a in (c, q, k))
        lower, upper = c4[:, 0], c4[:, 1]
        tot = lower[:, s - 1:s, :]
        w_lower = k4[:, 0] * jnp.exp(tot - lower)
        w_upper = q4[:, 1] * jnp.exp(upper)
        wv = jnp.stack([w_lower, w_upper], axis=1).reshape(n, w)
        c_next = jnp.stack([lower, upper + tot], axis=1).reshape(n, w)
        return wv, c_next
    sub = lax.broadcasted_iota(jnp.int32, (1, 8, w), 1)
    c3, f3, q3, k3 = (a.reshape(n // 8, 8, w) for a in (c, f, q, k))
    up = (sub // s) % 2 == 1
    tot = None
    for gi in reversed(range(8 // (2 * s))):
        r = gi * 2 * s + s - 1
        tg = jnp.broadcast_to(c3[:, r:r + 1, :], c3.shape)
        tot = tg if tot is None else jnp.where(sub < (gi + 1) * 2 * s, tg, tot)
    if s == 1:
        e = jnp.where(up, f3, 1.0)
    else:
        e = jnp.exp(jnp.where(up, c3, tot - c3))
    wv = jnp.where(up, q3, k3) * e
    c_next = c3 + jnp.where(up, tot, 0.0)
    return wv.reshape(n, w), c_next.reshape(n, w)


def _cross_attention(q, mk, mv):
    outs = []
    for hd in range(XA_HEADS):
        sl = slice(hd * XA_HD, (hd + 1) * XA_HD)
        s = _dot_nt(q[:, sl].astype(BF16), mk[:, sl]) * (XA_HD ** -0.5)
        s = s - jnp.max(s, axis=-1, keepdims=True)
        e = jnp.exp(s)
        p = e * (1.0 / jnp.sum(e, axis=-1, keepdims=True))
        outs.append(_dot(p.astype(BF16), mv[:, sl]))
    return jnp.concatenate(outs, axis=1)


def _prompt_kernel(x_ref, mk_ref, mv_ref, win_ref, wdt_ref, wdtT_ref, wout_ref, wq_ref, wo_ref,
                   caw_ref, lb_ref, gn_ref, scw_ref, scb_ref, dtb_ref, dtbc_ref, al_ref, alc_ref, dx_ref,
                   snorm_ref, gpre_ref, gpost_ref, gprex_ref, gpostx_ref,
                   masks_ref, tril_ref, tril3_ref, triu3_ref, expand_ref,
                   *rest, T, layer, n_prev):
    (y_ref, ca_ref, hg_ref, sc_ref, ss_ref,
     bufa, bufc, ug_s, z_s, xbc_s, dt_s, dtT_s, mix_s, sthg, stssd) = rest[n_prev:]
    ti = pl.program_id(1)
    n_chunks = T // CH

    @pl.when(ti == 0)
    def _():
        bufa[0:8, :] = jnp.zeros((8, D_A), F32)
        bufc[0:8, :] = jnp.zeros((8, SSD_CONV_DIM), F32)
        sthg[...] = jnp.zeros(sthg.shape, F32)
        stssd[...] = jnp.zeros(stssd.shape, F32)

    x = x_ref[0]
    h = _rms(x, gpre_ref[...]).astype(BF16)

    sxbc = _dot(h, win_ref[:, OFF_XBC:OFF_XBC + SSD_CONV_DIM])
    bufc[8:8 + T, :] = sxbc
    scw = scw_ref[...]
    xbc = (sxbc * scw[3:4, :] + bufc[7:7 + T, :] * scw[2:3, :] + bufc[6:6 + T, :] * scw[1:2, :]
           + bufc[5:5 + T, :] * scw[0:1, :] + scb_ref[...])
    xbc_s[...] = _silu(xbc)
    sc_ref[0] = sxbc[T - 3:T, :]
    bufc[0:8, :] = sxbc[T - 8:T, :]
    sdt = _dot(h, wdt_ref[...])[:, 0:SSD_HEADS]
    dt_s[...] = _softplus(sdt + dtb_ref[...])
    dtT = _softplus(_dot_nt(wdtT_ref[...], h) + dtbc_ref[...])
    for c in range(n_chunks):
        dtT_s[c] = dtT[:, c * CH:(c + 1) * CH]
    z_s[...] = _dot(h, win_ref[:, OFF_SZ:OFF_SZ + D_SSD])

    ua = _dot(h, win_ref[:, OFF_A:OFF_A + 4 * D_A])
    a_h, a_b, a_c, a_z = (ua[:, k * D_A:(k + 1) * D_A] for k in range(4))
    va = a_c * a_h
    bufa[8:8 + T, :] = va
    caw = caw_ref[...]
    conv = va * caw[2:3, :] + bufa[7:7 + T, :] * caw[1:2, :] + bufa[6:6 + T, :] * caw[0:1, :]
    mix_s[:, 0:D_A] = (a_b * conv * _silu(a_z)).astype(BF16)
    ca_ref[0] = va[T - 2:T, :]
    bufa[0:8, :] = va[T - 8:T, :]

    ug_s[...] = _dot(h, win_ref[:, OFF_G:OFF_G + 4 * D_HG])


    lb = _hgrn_lower_bound(lb_ref[...], layer)
    a_row = -jnp.exp(al_ref[...])
    a_col = -jnp.exp(alc_ref[...])
    tril = tril_ref[...]
    first_of_pair = lax.broadcasted_iota(jnp.int32, (1, 2 * SSD_P), 1) < SSD_P

    def chunk(c, carry):
        r0 = pl.multiple_of(c * CH, CH)
        rows = pl.ds(r0, CH)

        ug = ug_s[rows, :]
        gq, gf, gi, gz = (ug[:, k * D_HG:(k + 1) * D_HG] for k in range(4))
        f = lb + (1.0 - lb) * _sigmoid(gf)
        logf = jnp.log(f)
        kk = 1.0 - f
        q_b, k_b, v_b = gq.astype(BF16), kk.astype(BF16), gi.astype(BF16)
        hs = [slice(hd * HG_DK, (hd + 1) * HG_DK) for hd in range(HG_HEADS)]
        A = [masks_ref[0] * _dot_nt(q_b[:, s_], k_b[:, s_]) for s_ in hs]
        G = logf
        for li, s in enumerate(HG_LEVELS):
            w, G = _hgrn_level(G, f, gq, kk, s)
            w = w.astype(BF16)
            m = masks_ref[li + 1]
            A = [A[hd] + m * _dot_nt(w[:, hs[hd]], w[:, hs[hd]]) for hd in range(HG_HEADS)]
        g_last = G[CH - 1:CH, :]
        qg = (gq * jnp.exp(G)).astype(BF16)
        kd = (kk * jnp.exp(g_last - G)).astype(BF16)
        dec = jnp.exp(g_last)
        o_heads = []
        for hd in range(HG_HEADS):
            s_ = hs[hd]
            st = sthg[hd]
            o = _dot_nt(qg[:, s_], st.astype(BF16)) + _dot(A[hd].astype(BF16), v_b[:, s_])
            sthg[hd] = st * dec[:, s_] + _dot_tn(v_b[:, s_], kd[:, s_])
            o_heads.append(_rms(o, gn_ref[:, s_]))
        yb = jnp.concatenate(o_heads, axis=1) * _silu(gz)
        mix_s[rows, D_A:D_A + D_HG] = yb.astype(BF16)

        xbc_c = xbc_s[rows, :]
        xs = xbc_c[:, 0:D_SSD]
        Bm = xbc_c[:, D_SSD:D_SSD + SSD_GROUPS * SSD_N].astype(BF16)
        Cm = xbc_c[:, D_SSD + SSD_GROUPS * SSD_N:].astype(BF16)
        dt = dt_s[rows, :]
        dtT_c = dtT_s[c]
        cs = _dot(tril3_ref[...], _split3_rows(dt * a_row))
        csT = _dot(_split3_cols(dtT_c * a_col), triu3_ref[...])
        cs_last = cs[CH - 1:CH, :]
        w_all = jnp.concatenate([dt * jnp.exp(cs_last - cs), jnp.exp(cs), dt,
                                 jnp.broadcast_to(jnp.exp(cs_last), (8, SSD_HEADS))], axis=0)
        e_all = _dot(_split3_cols(w_all), expand_ref[...])
        e_dec, e_cs, e_dt, e_last = e_all[0:CH], e_all[CH:2 * CH], e_all[2 * CH:3 * CH], e_all[3 * CH:3 * CH + 1]
        xdt = (xs * e_dt).astype(BF16)
        xdec = (xs * e_dec).astype(BF16)
        y_groups = []
        hpg = SSD_HEADS // SSD_GROUPS
        gw = hpg * SSD_P
        for g in range(SSD_GROUPS):
            Cg = Cm[:, g * SSD_N:(g + 1) * SSD_N]
            Bg = Bm[:, g * SSD_N:(g + 1) * SSD_N]
            cb = _dot_nt(Cg, Bg)
            st = stssd[g]
            gcols = slice(g * gw, (g + 1) * gw)
            y_off = _dot(Cg, st.astype(BF16)) * e_cs[:, gcols]
            stssd[g] = st * e_last[:, gcols] + _dot_tn(Bg, xdec[:, gcols])
            pair_out = []
            for pr in range(hpg // 2):
                h0 = g * hpg + 2 * pr
                ms = []
                for hh in (h0, h0 + 1):
                    diff = cs[:, hh:hh + 1] - csT[hh:hh + 1, :]
                    ms.append((cb * (jnp.exp(jnp.minimum(diff, 0.0)) * tril)).astype(BF16))
                both = _dot(jnp.concatenate(ms, axis=0), xdt[:, h0 * SSD_P:(h0 + 2) * SSD_P])
                pair_out.append(jnp.where(first_of_pair, both[0:CH], both[CH:2 * CH]))
            y_groups.append(y_off + jnp.concatenate(pair_out, axis=1))
        y = jnp.concatenate(y_groups, axis=1) + dx_ref[...] * xs
        yc = _rms(y * _silu(z_s[rows, :]), snorm_ref[...])
        mix_s[rows, D_A + D_HG:] = yc.astype(BF16)
        return carry

    lax.fori_loop(0, n_chunks, chunk, 0, unroll=True)

    x1 = x + _rms(_dot(mix_s[...], wout_ref[...]), gpost_ref[...])
    hx = _rms(x1, gprex_ref[...]).astype(BF16)
    q = _dot(hx, wq_ref[...])
    att = _cross_attention(q, mk_ref[0], mv_ref[0])
    y_ref[0] = x1 + _rms(_dot(att.astype(BF16), wo_ref[...]), gpostx_ref[...])

    @pl.when(ti == pl.num_programs(1) - 1)
    def _():
        for hd in range(HG_HEADS):
            hg_ref[0, hd] = sthg[hd].T
        hpg = SSD_HEADS // SSD_GROUPS
        for g in range(SSD_GROUPS):
            sg = stssd[g].T
            for hh in range(hpg):
                ss_ref[0, g * hpg + hh] = sg[hh * SSD_P:(hh + 1) * SSD_P, :]


def _prompt_layer(x, mk, mv, wts, layer, depth, prev, T):
    b, L, _ = x.shape
    prev = [] if prev is None else list(prev)
    c = _consts()
    n_chunks = T // CH
    const_names = ("masks", "tril", "tril3", "triu3", "expand3")
    consts = [c[k] for k in const_names]
    small = [wts[k] for k in ("caw", "lb", "gn", "scw", "scb", "dtb", "dtbc", "al", "alc", "dx", "snorm",
                              "gpre", "gpost", "gprex", "gpostx")]
    big = [wts[k] for k in ("win", "wdt", "wdtT", "wout", "wq", "wo")]

    def full(a):
        nd = a.ndim
        return pl.BlockSpec(a.shape, lambda bi, ti, _n=nd: (0,) * _n, pipeline_mode=pl.Buffered(1))

    in_specs = ([pl.BlockSpec((1, T, D_MODEL), lambda bi, ti: (bi, ti, 0)),
                 pl.BlockSpec((1, N_MEM, D_MODEL), lambda bi, ti: (bi, 0, 0)),
                 pl.BlockSpec((1, N_MEM, D_MODEL), lambda bi, ti: (bi, 0, 0))]
                + [full(a) for a in big] + [full(a) for a in small] + [full(a) for a in consts]
                + [_ANY] * len(prev))
    n_in = len(in_specs)
    state_shapes = [(CONV_A_W - 1, D_A), (HG_HEADS, HG_DK, HG_DK), (SSD_CONV_W - 1, SSD_CONV_DIM),
                    (SSD_HEADS, SSD_P, SSD_N)]
    out_shape = ([jax.ShapeDtypeStruct((b, L, D_MODEL), F32)]
                 + [jax.ShapeDtypeStruct((depth, b) + s, F32) for s in state_shapes])
    out_specs = ([pl.BlockSpec((1, T, D_MODEL), lambda bi, ti: (bi, ti, 0))]
                 + [pl.BlockSpec((None, 1) + s, lambda bi, ti, _n=len(s): (layer, bi) + (0,) * _n)
                    for s in state_shapes])
    aliases = {n_in - len(prev) + k: 1 + k for k in range(len(prev))}
    scratch = [pltpu.VMEM((8 + T, D_A), F32), pltpu.VMEM((8 + T, SSD_CONV_DIM), F32),
               pltpu.VMEM((T, 4 * D_HG), F32), pltpu.VMEM((T, D_SSD), F32), pltpu.VMEM((T, SSD_CONV_DIM), F32),
               pltpu.VMEM((T, SSD_HEADS), F32), pltpu.VMEM((n_chunks, SSD_HEADS, CH), F32),
               pltpu.VMEM((T, 2 * D_MODEL), BF16),
               pltpu.VMEM((HG_HEADS, HG_DK, HG_DK), F32),
               pltpu.VMEM((SSD_GROUPS, SSD_N, (SSD_HEADS // SSD_GROUPS) * SSD_P), F32)]
    return pl.pallas_call(
        functools.partial(_prompt_kernel, T=T, layer=layer, n_prev=len(prev)),
        grid=(b, L // T),
        in_specs=in_specs, out_specs=out_specs, out_shape=out_shape, scratch_shapes=scratch,
        input_output_aliases=aliases,
        compiler_params=pltpu.CompilerParams(dimension_semantics=("arbitrary", "arbitrary"),
                                             vmem_limit_bytes=VMEM_LIMIT),
        name=f"prompt_layer{layer}",
    )(x, mk, mv, *big, *small, *consts, *prev)


def _layer_weights(l, w_in, conv_a_w, hgrn_lb, hgrn_gnorm, ssd_conv_w, ssd_conv_b, ssd_dt_bias, ssd_A_log, ssd_D,
                   ssd_norm, w_out, g_pre_mix, g_post_mix, g_pre_x, g_post_x, w_q, w_o):
    wdt = jnp.pad(w_in[l][:, OFF_DT:].astype(BF16), ((0, 0), (0, 128 - SSD_HEADS)))
    row = lambda a: a.reshape(1, -1)
    return dict(
        win=w_in[l].astype(BF16), wdt=wdt, wdtT=w_in[l][:, OFF_DT:OFF_DT + SSD_HEADS].T.astype(BF16),
        wout=w_out[l].astype(BF16), wq=w_q[l].astype(BF16), wo=w_o[l].astype(BF16),
        caw=conv_a_w[l], lb=hgrn_lb, gn=row(hgrn_gnorm[l]), scw=ssd_conv_w[l], scb=row(ssd_conv_b[l]),
        dtb=row(ssd_dt_bias[l]), dtbc=ssd_dt_bias[l].reshape(-1, 1), al=row(ssd_A_log[l]),
        alc=ssd_A_log[l].reshape(-1, 1), dx=row(jnp.repeat(ssd_D[l], SSD_P)), snorm=row(ssd_norm[l]),
        gpre=row(g_pre_mix[l]), gpost=row(g_post_mix[l]), gprex=row(g_pre_x[l]), gpostx=row(g_post_x[l]))


SB = 8
D_HGP = 4 * D_HG
D_SSP = 4 * D_SSD + 2 * SSD_GROUPS * SSD_N


def _full_spec(a):
    nd = a.ndim
    return pl.BlockSpec(a.shape, lambda *_, _n=nd: (0,) * _n)


def _sample_pre_kernel(x_ref, ca_ref, sc_ref, win_ref, caw_ref, lb_ref, scw_ref, scb_ref, dtb_ref, al_ref,
                       gpre_ref, expand_ref,
                       ya_ref, canew_ref, hgp_ref, ssp_ref, scnew_ref, *, layer):
    h = _rms(x_ref[...], gpre_ref[...]).astype(BF16)
    u = _dot(h, win_ref[...])
    a_h, a_b, a_c, a_z = (u[:, OFF_A + k * D_A:OFF_A + (k + 1) * D_A] for k in range(4))
    va = a_c * a_h
    p0, p1 = ca_ref[:, 0:D_A], ca_ref[:, D_A:2 * D_A]
    caw = caw_ref[...]
    conv = va * caw[2:3, :] + p1 * caw[1:2, :] + p0 * caw[0:1, :]
    ya_ref[...] = a_b * conv * _silu(a_z)
    canew_ref[:, 0:D_A] = p1
    canew_ref[:, D_A:2 * D_A] = va
    lb = _hgrn_lower_bound(lb_ref[...], layer)
    gq, gf, gi, gz = (u[:, OFF_G + k * D_HG:OFF_G + (k + 1) * D_HG] for k in range(4))
    hgp_ref[:, 0:D_HG] = gq
    hgp_ref[:, D_HG:2 * D_HG] = lb + (1.0 - lb) * _sigmoid(gf)
    hgp_ref[:, 2 * D_HG:3 * D_HG] = gi
    hgp_ref[:, 3 * D_HG:] = gz
    sxbc = u[:, OFF_XBC:OFF_XBC + SSD_CONV_DIM]
    W = SSD_CONV_DIM
    q0, q1, q2 = sc_ref[:, 0:W], sc_ref[:, W:2 * W], sc_ref[:, 2 * W:3 * W]
    scw = scw_ref[...]
    xbc = _silu(sxbc * scw[3:4, :] + q2 * scw[2:3, :] + q1 * scw[1:2, :] + q0 * scw[0:1, :] + scb_ref[...])
    scnew_ref[:, 0:W] = q1
    scnew_ref[:, W:2 * W] = q2
    scnew_ref[:, 2 * W:3 * W] = sxbc
    xs = xbc[:, 0:D_SSD]
    dt = _softplus(u[:, OFF_DT:OFF_DT + SSD_HEADS] + dtb_ref[...])
    dec = jnp.exp(dt * -jnp.exp(al_ref[...]))
    n = dt.shape[0]
    e_all = _dot(_split3_cols(jnp.concatenate([dt, dec], axis=0)), expand_ref[...])
    ssp_ref[:, 0:D_SSD] = xs
    ssp_ref[:, D_SSD:2 * D_SSD] = xs * e_all[0:n]
    ssp_ref[:, 2 * D_SSD:3 * D_SSD] = e_all[n:2 * n]
    ssp_ref[:, 3 * D_SSD:4 * D_SSD] = u[:, OFF_SZ:OFF_SZ + D_SSD]
    ssp_ref[:, 4 * D_SSD:] = xbc[:, D_SSD:]


def _sample_pre(x, ca, sc, wts, layer):
    n = x.shape[0]
    args = [x, ca, sc, wts["win"], wts["caw"], wts["lb"], wts["scw"], wts["scb"], wts["dtb"], wts["al"],
            wts["gpre"], _consts()["expand3"]]
    out_shape = [jax.ShapeDtypeStruct((n, D_A), F32), jax.ShapeDtypeStruct((n, 2 * D_A), F32),
                 jax.ShapeDtypeStruct((n, D_HGP), F32), jax.ShapeDtypeStruct((n, D_SSP), F32),
                 jax.ShapeDtypeStruct((n, 3 * SSD_CONV_DIM), F32)]
    return pl.pallas_call(
        functools.partial(_sample_pre_kernel, layer=layer),
        in_specs=[_full_spec(a) for a in args],
        out_specs=[_full_spec(s) for s in out_shape],
        out_shape=out_shape, grid=(1,),
        compiler_params=pltpu.CompilerParams(dimension_semantics=("arbitrary",), vmem_limit_bytes=VMEM_LIMIT),
        name=f"sample_pre{layer}",
    )(*args)


def _pad_rows_T(blk):
    w = blk.shape[1]
    return jnp.concatenate([blk, jnp.zeros((128 - blk.shape[0], w), blk.dtype)], axis=0).T


def _sample_state_kernel(hgp_ref, ssp_ref, shg_ref, sss_ref, *rest):
    o_ref, y_ref, shg_out, sss_out = rest[-4:]
    rid_hg = lax.broadcasted_iota(jnp.int32, (SB, HG_DK), 0)
    for hd in range(HG_HEADS):
        cols = slice(hd * HG_DK, (hd + 1) * HG_DK)
        q_b = hgp_ref[:, cols].astype(BF16)
        fT = _pad_rows_T(hgp_ref[:, D_HG + hd * HG_DK:D_HG + (hd + 1) * HG_DK])
        o = jnp.zeros((SB, HG_DK), F32)
        for j in range(SB):
            fcol = fT[:, j:j + 1]
            vrow = hgp_ref[j:j + 1, 2 * D_HG + hd * HG_DK:2 * D_HG + (hd + 1) * HG_DK]
            s_new = fcol * shg_ref[j, hd] + (1.0 - fcol) * vrow
            shg_out[j, hd] = s_new
            o = jnp.where(rid_hg == j, _dot(q_b, s_new.astype(BF16)), o)
        o_ref[:, cols] = o
    gw = (SSD_HEADS // SSD_GROUPS) * SSD_P
    rid_ss = lax.broadcasted_iota(jnp.int32, (SB, gw), 0)
    xdtT = _pad_rows_T(ssp_ref[:, D_SSD:2 * D_SSD])
    decT = _pad_rows_T(ssp_ref[:, 2 * D_SSD:3 * D_SSD])
    for g in range(SSD_GROUPS):
        rows = slice(g * gw, (g + 1) * gw)
        c_b = ssp_ref[:, 4 * D_SSD + (SSD_GROUPS + g) * SSD_N:4 * D_SSD + (SSD_GROUPS + g + 1) * SSD_N].astype(BF16)
        y = jnp.zeros((SB, gw), F32)
        for j in range(SB):
            brow = ssp_ref[j:j + 1, 4 * D_SSD + g * SSD_N:4 * D_SSD + (g + 1) * SSD_N]
            s_new = decT[rows, j:j + 1] * sss_ref[j, rows, :] + xdtT[rows, j:j + 1] * brow
            sss_out[j, rows, :] = s_new
            y = jnp.where(rid_ss == j, _dot_nt(c_b, s_new.astype(BF16)), y)
        y_ref[:, rows] = y


def _sample_state(hgp, ssp, shg_all, sss_all, layer, prev):
    n = hgp.shape[0]
    prev = [] if prev is None else list(prev)
    rowblk = lambda w: pl.BlockSpec((SB, w), lambda i: (i, 0))
    hg_blk = pl.BlockSpec((None, SB, HG_HEADS, HG_DK, HG_DK), lambda i: (layer, i, 0, 0, 0))
    ss_blk = pl.BlockSpec((None, SB, SSD_HEADS * SSD_P, SSD_N), lambda i: (layer, i, 0, 0))
    return pl.pallas_call(
        _sample_state_kernel,
        grid=(n // SB,),
        in_specs=[rowblk(D_HGP), rowblk(D_SSP), hg_blk, ss_blk] + [_ANY] * len(prev),
        out_specs=[rowblk(D_HG), rowblk(D_SSD), hg_blk, ss_blk],
        out_shape=[jax.ShapeDtypeStruct((n, D_HG), F32), jax.ShapeDtypeStruct((n, D_SSD), F32),
                   jax.ShapeDtypeStruct(shg_all.shape, F32), jax.ShapeDtypeStruct(sss_all.shape, F32)],
        input_output_aliases={4 + k: 2 + k for k in range(len(prev))},
        compiler_params=pltpu.CompilerParams(dimension_semantics=("arbitrary",), vmem_limit_bytes=VMEM_LIMIT),
        name="sample_state",
    )(hgp, ssp, shg_all, sss_all, *prev)


def _sample_mid_kernel(x_ref, ya_ref, o_ref, y_ref, hgp_ref, ssp_ref, wout_ref, wq_ref, gn_ref, dx_ref, snorm_ref,
                       gpost_ref, gprex_ref, x1_ref, q_ref):
    gz = hgp_ref[:, 3 * D_HG:]
    o = o_ref[...]
    yb = jnp.concatenate([_rms(o[:, hd * HG_DK:(hd + 1) * HG_DK], gn_ref[:, hd * HG_DK:(hd + 1) * HG_DK])
                          for hd in range(HG_HEADS)], axis=1) * _silu(gz)
    y = y_ref[...] + dx_ref[...] * ssp_ref[:, 0:D_SSD]
    yc = _rms(y * _silu(ssp_ref[:, 3 * D_SSD:4 * D_SSD]), snorm_ref[...])
    mix = jnp.concatenate([ya_ref[...], yb, yc], axis=1).astype(BF16)
    x1 = x_ref[...] + _rms(_dot(mix, wout_ref[...]), gpost_ref[...])
    x1_ref[...] = x1
    q = _dot(_rms(x1, gprex_ref[...]).astype(BF16), wq_ref[...])
    for hd in range(XA_HEADS):
        for k in range(XA_HD // 128):
            q_ref[:, k * XA_HEADS + hd, :] = q[:, hd * XA_HD + k * 128:hd * XA_HD + (k + 1) * 128]


def _sample_mid(x, ya, o, y, hgp, ssp, wts):
    n = x.shape[0]
    args = [x, ya, o, y, hgp, ssp, wts["wout"], wts["wq"], wts["gn"], wts["dx"], wts["snorm"], wts["gpost"],
            wts["gprex"]]
    out_shape = [jax.ShapeDtypeStruct((n, D_MODEL), F32),
                 jax.ShapeDtypeStruct((n, XA_HEADS * (XA_HD // 128), 128), F32)]
    return pl.pallas_call(
        _sample_mid_kernel, grid=(1,),
        in_specs=[_full_spec(a) for a in args], out_specs=[_full_spec(s) for s in out_shape], out_shape=out_shape,
        compiler_params=pltpu.CompilerParams(dimension_semantics=("arbitrary",), vmem_limit_bytes=VMEM_LIMIT),
        name="sample_mid",
    )(*args)


def _lane_class_reduce(x, op):
    sh = KV_SUB
    while sh < 128:
        x = op(x, pltpu.roll(x, sh, axis=1))
        sh *= 2
    return x


def _sample_attn_kernel(x1_ref, q_ref, k_ref, v_ref, wo_ref, gpostx_ref, x2_ref):
    lane = lax.broadcasted_iota(jnp.int32, (KV_SUB, KV_ROWS), 1)
    sub = lax.broadcasted_iota(jnp.int32, (KV_SUB, KV_ROWS), 0)
    own = ((lane & (KV_SUB - 1)) == sub).astype(F32)
    rid = lax.broadcasted_iota(jnp.int32, (SB, KV_ROWS), 0)
    t_all = jnp.zeros((SB, KV_ROWS), F32)
    for j in range(SB):
        r = _dot_nt(q_ref[j].astype(BF16), k_ref[j].astype(BF16))
        t = jnp.sum(r * own, axis=0, keepdims=True)
        t_all = jnp.where(rid == j, t, t_all)
    n_tiles = KV_ROWS // 128
    lane1 = lax.broadcasted_iota(jnp.int32, (SB, 128), 1)
    piece = (lane1 // XA_HEADS) % KV_SPLIT
    chunks = []
    for c in range(n_tiles):
        x = t_all[:, c * 128:(c + 1) * 128]
        tot = x
        for k in range(1, KV_SPLIT):
            fwd = pltpu.roll(x, 128 - k * XA_HEADS, axis=1)
            bwd = pltpu.roll(x, (KV_SPLIT - k) * XA_HEADS, axis=1)
            tot = tot + jnp.where(piece + k < KV_SPLIT, fwd, bwd)
        chunks.append(tot * (XA_HD ** -0.5))
    mx = _lane_class_reduce(functools.reduce(jnp.maximum, chunks), jnp.maximum)
    es = [jnp.exp(ch - mx) for ch in chunks]
    den = _lane_class_reduce(functools.reduce(lambda a, b: a + b, es), lambda a, b: a + b)
    p_all = jnp.concatenate([e * (1.0 / den) for e in es], axis=1)
    rid_o = lax.broadcasted_iota(jnp.int32, (SB, D_MODEL), 0)
    att = jnp.zeros((SB, D_MODEL), F32)
    for j in range(SB):
        p8 = (own * p_all[j:j + 1, :]).astype(BF16)
        o = _dot(p8, v_ref[j].astype(BF16))
        row = jnp.concatenate([o[k * XA_HEADS + hd:k * XA_HEADS + hd + 1, :]
                               for hd in range(XA_HEADS) for k in range(KV_SPLIT)], axis=1)
        att = jnp.where(rid_o == j, row, att)
    x2_ref[...] = x1_ref[...] + _rms(_dot(att.astype(BF16), wo_ref[...]), gpostx_ref[...])


def _sample_attn(x1, q8, ck_rows, cv_rows, wts, layer):
    n = x1.shape[0]
    rowblk = pl.BlockSpec((SB, D_MODEL), lambda i: (i, 0))
    qblk = pl.BlockSpec((SB, KV_SUB, 128), lambda i: (i, 0, 0))
    kvblk = pl.BlockSpec((None, SB, KV_ROWS, 128), lambda i: (layer, i, 0, 0))
    return pl.pallas_call(
        _sample_attn_kernel,
        grid=(n // SB,),
        in_specs=[rowblk, qblk, kvblk, kvblk, _full_spec(wts["wo"]), _full_spec(wts["gpostx"])],
        out_specs=rowblk,
        out_shape=jax.ShapeDtypeStruct((n, D_MODEL), F32),
        compiler_params=pltpu.CompilerParams(dimension_semantics=("arbitrary",), vmem_limit_bytes=VMEM_LIMIT),
        name="sample_attn",
    )(x1, q8, ck_rows, cv_rows, wts["wo"], wts["gpostx"])


def _sample_layer(x, ca, shg_all, sc, sss_all, ck_rows, cv_rows, wts, layer, prev_states):
    n = x.shape[0]
    ya, ca_new, hgp, ssp, sc_new = _sample_pre(x, ca.reshape(n, -1), sc.reshape(n, -1), wts, layer)
    o, y, shg_new, sss_new = _sample_state(hgp, ssp, shg_all, sss_all, layer, prev_states)
    x1, q8 = _sample_mid(x, ya, o, y, hgp, ssp, wts)
    x2 = _sample_attn(x1, q8, ck_rows, cv_rows, wts, layer)
    return x2, ca_new.reshape(ca.shape), sc_new.reshape(sc.shape), (shg_new, sss_new)


PROMPT_TILE = 512


def kernel(x_prompt, x_sample, mem_prompt, state_conv_a, state_hgrn, state_ssd_conv, state_ssd, cache_mem_k,
           cache_mem_v, w_in, conv_a_w, hgrn_lb, hgrn_gnorm, ssd_conv_w, ssd_conv_b, ssd_dt_bias, ssd_A_log, ssd_D,
           ssd_norm, w_out, g_pre_mix, g_post_mix, g_pre_x, g_post_x, g_mem, w_q, w_k, w_v, w_o):
    depth = w_in.shape[0]
    n = x_sample.shape[0]
    yp = x_prompt
    ys = x_sample.reshape(n, D_MODEL)
    ck_rows, cv_rows = _kv_rows_view(cache_mem_k), _kv_rows_view(cache_mem_v)
    sss_all = state_ssd.reshape(depth, n, SSD_HEADS * SSD_P, SSD_N)
    kv_rows = p_states = s_states = None
    s_ca, s_sc = [], []
    for l in range(depth):
        wts = _layer_weights(l, w_in, conv_a_w, hgrn_lb, hgrn_gnorm, ssd_conv_w, ssd_conv_b, ssd_dt_bias, ssd_A_log,
                             ssd_D, ssd_norm, w_out, g_pre_mix, g_post_mix, g_pre_x, g_post_x, w_q, w_o)
        *kv_rows, mk, mv = _memory_kv(mem_prompt, g_mem[l], w_k[l].astype(BF16), w_v[l].astype(BF16), l, depth,
                                      kv_rows)
        yp, *p_states = _prompt_layer(yp, mk, mv, wts, l, depth, p_states, PROMPT_TILE)
        ys, ca, sc, s_states = _sample_layer(ys, state_conv_a[l], state_hgrn, state_ssd_conv[l], sss_all,
                                             ck_rows, cv_rows, wts, l, s_states)
        s_ca.append(ca)
        s_sc.append(sc)
    p_ca, p_hg, p_sc, p_ss = p_states
    s_hg, s_ss = s_states
    return (yp, ys.reshape(x_sample.shape), p_ca, p_hg, p_sc, p_ss, _kv_from_rows(kv_rows[0]),
            _kv_from_rows(kv_rows[1]), jnp.stack(s_ca), s_hg, jnp.stack(s_sc), s_ss.reshape(state_ssd.shape))
```

```python
import functools

import numpy as np
import jax
import jax.numpy as jnp
from jax import lax
from jax.experimental import pallas as pl
from jax.experimental.pallas import tpu as pltpu

F32 = jnp.float32
BF16 = jnp.bfloat16

D_MODEL = 1024
D_A = 512
CONV_A_W = 3
D_HG = 512
HG_HEADS = 4
HG_DK = 128
D_SSD = 1024
SSD_P = 64
SSD_HEADS = 16
SSD_GROUPS = 2
SSD_N = 128
SSD_CONV_W = 4
SSD_CONV_DIM = D_SSD + 2 * SSD_GROUPS * SSD_N
N_MEM = 256
XA_HEADS = 4
XA_HD = 256
EPS = 1e-6
KV_SPLIT = XA_HD // 128
KV_SUB = XA_HEADS * KV_SPLIT
KV_ROWS = N_MEM * KV_SUB

OFF_A = 0
OFF_G = 2048
OFF_SZ = 4096
OFF_XBC = 5120
OFF_DT = 6656
D_IN = 6672

CH = 128
HG_LEVELS = (1, 2, 4, 8, 16, 32, 64)
VMEM_LIMIT = 56 * 1024 * 1024


def _rms(x, g):
    ms = jnp.mean(x * x, axis=-1, keepdims=True)
    return x * lax.rsqrt(ms + EPS) * g


def _silu(x):
    return x * (1.0 / (1.0 + jnp.exp(-x)))


def _sigmoid(x):
    return 1.0 / (1.0 + jnp.exp(-x))


def _softplus(x):
    return jnp.maximum(x, 0.0) + jnp.log(1.0 + jnp.exp(-jnp.abs(x)))


def _dot(a, b):
    return jnp.dot(a, b, preferred_element_type=F32)


def _dot_nt(a, b):
    return lax.dot_general(a, b, (((1,), (1,)), ((), ())), preferred_element_type=F32)


def _dot_tn(a, b):
    return lax.dot_general(a, b, (((0,), (0,)), ((), ())), preferred_element_type=F32)


def _split3(x):
    hi = x.astype(BF16)
    r = x - hi.astype(F32)
    mid = r.astype(BF16)
    lo = (r - mid.astype(F32)).astype(BF16)
    return hi, mid, lo


def _split3_rows(x):
    return jnp.concatenate(_split3(x), axis=0)


def _split3_cols(x):
    return jnp.concatenate(_split3(x), axis=1)


@functools.lru_cache(maxsize=None)
def _consts():
    r = np.arange(CH)
    i, t = r[:, None], r[None, :]
    masks = [np.eye(CH, dtype=bool)]
    for s in HG_LEVELS:
        up = ((r // s) % 2 == 1)
        same = (i // (2 * s)) == (t // (2 * s))
        masks.append(same & up[:, None] & (~up)[None, :])
    masks = np.stack(masks).astype(np.float32)
    tril = (t <= i).astype(np.float32)
    tril3 = np.tile(tril, (1, 3))
    triu3 = np.tile(tril.T, (3, 1))
    e = (np.arange(D_SSD)[None, :] // SSD_P == np.arange(SSD_HEADS)[:, None]).astype(np.float32)
    expand3 = np.tile(e, (3, 1))
    return dict(
        masks=jnp.asarray(masks, F32),
        tril=jnp.asarray(tril, F32), tril3=jnp.asarray(tril3, BF16), triu3=jnp.asarray(triu3, BF16),
        expand3=jnp.asarray(expand3, BF16))


def _kv_rows_view(c_all):
    depth, n = c_all.shape[:2]
    c = c_all.reshape(depth, n, N_MEM, XA_HEADS, KV_SPLIT, 128)
    return jnp.transpose(c, (0, 1, 2, 4, 3, 5)).reshape(depth, n, KV_ROWS, 128)


def _kv_from_rows(r_all):
    depth, n = r_all.shape[:2]
    c = r_all.reshape(depth, n, N_MEM, KV_SPLIT, XA_HEADS, 128)
    return jnp.transpose(c, (0, 1, 2, 4, 3, 5)).reshape(depth, n, N_MEM, XA_HEADS, XA_HD)


def _store_kv_rows(r_ref, x):
    for hd in range(XA_HEADS):
        for k in range(KV_SPLIT):
            r_ref[:, k * XA_HEADS + hd, :] = x[:, hd * XA_HD + k * 128:hd * XA_HD + (k + 1) * 128]


_ANY = pl.BlockSpec(memory_space=pl.ANY)


def _full_spec(a):
    nd = a.ndim
    return pl.BlockSpec(a.shape, lambda *_, _n=nd: (0,) * _n)


def _layer_spec(a, layer, **kw):
    nd = a.ndim
    return pl.BlockSpec((None,) + tuple(a.shape[1:]), lambda *_, _n=nd: (layer,) + (0,) * (_n - 1), **kw)


def _memkv_kernel(mem_ref, g_ref, wk_ref, wv_ref, *refs, layer):
    kr_ref, vr_ref, kb_ref, vb_ref = refs[-4:]
    m = _rms(mem_ref[0], g_ref[layer:layer + 1, :]).astype(BF16)
    for w_ref, r_ref, b_ref in ((wk_ref, kr_ref, kb_ref), (wv_ref, vr_ref, vb_ref)):
        kv = _dot(m, w_ref[...])
        b_ref[0] = kv.astype(BF16)
        _store_kv_rows(r_ref, kv)


def _memory_kv(mem, wts, layer, depth, prev):
    b = mem.shape[0]
    blk = pl.BlockSpec((1, N_MEM, D_MODEL), lambda i: (i, 0, 0))
    rows_blk = pl.BlockSpec((None, None, N_MEM, KV_SUB, 128), lambda i: (layer, i, 0, 0, 0))
    rows_sds = jax.ShapeDtypeStruct((depth, b, N_MEM, KV_SUB, 128), F32)
    extra, extra_specs, aliases = [], [], {}
    if prev is not None:
        extra, extra_specs, aliases = list(prev), [_ANY, _ANY], {4: 0, 5: 1}
    return pl.pallas_call(
        functools.partial(_memkv_kernel, layer=layer),
        grid=(b,),
        in_specs=[blk, _full_spec(wts["gmem"]), _layer_spec(wts["wk"], layer), _layer_spec(wts["wv"], layer)]
        + extra_specs,
        out_specs=[rows_blk, rows_blk, blk, blk],
        out_shape=[rows_sds, rows_sds] + [jax.ShapeDtypeStruct((b, N_MEM, D_MODEL), BF16)] * 2,
        input_output_aliases=aliases,
        compiler_params=pltpu.CompilerParams(dimension_semantics=("arbitrary",), vmem_limit_bytes=VMEM_LIMIT),
        name="memory_kv",
    )(mem, wts["gmem"], wts["wk"], wts["wv"], *extra)


def _hgrn_lower_bound(lb_all, layer):
    depth = lb_all.shape[0]
    rows = [lb_all[j:j + 1, :] for j in range(depth)]
    mx = functools.reduce(jnp.maximum, rows)
    ex = [jnp.exp(rw - mx) for rw in rows]
    tot = functools.reduce(lambda a, b: a + b, ex)
    acc = jnp.zeros_like(tot)
    for j in range(1, layer + 1):
        acc = acc + ex[j]
    return acc / tot


def _hgrn_level(c, f, q, k, s):
    n, w = c.shape
    if s >= 8:
        nb = n // (2 * s)
        c4, q4, k4 = (a.reshape(nb, 2, s, w) for a in (c, q, k))
        lower, upper = c4[:, 0], c4[:, 1]
        tot = lower[:, s - 1:s, :]
        w_lower = k4[:, 0] * jnp.exp(tot - lower)
        w_upper = q4[:, 1] * jnp.exp(upper)
        wv = jnp.stack([w_lower, w_upper], axis=1).reshape(n, w)
        c_next = jnp.stack([lower, upper + tot], axis=1).reshape(n, w)
        return wv, c_next
    sub = lax.broadcasted_iota(jnp.int32, (1, 8, w), 1)
    c3, f3, q3, k3 = (a.reshape(n // 8, 8, w) for a in (c, f, q, k))
    up = (sub // s) % 2 == 1
    tot = None
    for gi in reversed(range(8 // (2 * s))):
        r = gi * 2 * s + s - 1
        tg = jnp.broadcast_to(c3[:, r:r + 1, :], c3.shape)
        tot = tg if tot is None else jnp.where(sub < (gi + 1) * 2 * s, tg, tot)
    if s == 1:
        e = jnp.where(up, f3, 1.0)
    else:
        e = jnp.exp(jnp.where(up, c3, tot - c3))
    wv = jnp.where(up, q3, k3) * e
    c_next = c3 + jnp.where(up, tot, 0.0)
    return wv.reshape(n, w), c_next.reshape(n, w)


def _cross_attention(q, mk, mv):
    outs = []
    for hd in range(XA_HEADS):
        sl = slice(hd * XA_HD, (hd + 1) * XA_HD)
        s = _dot_nt(q[:, sl].astype(BF16), mk[:, sl]) * (XA_HD ** -0.5)
        s = s - jnp.max(s, axis=-1, keepdims=True)
        e = jnp.exp(s)
        p = e * (1.0 / jnp.sum(e, axis=-1, keepdims=True))
        outs.append(_dot(p.astype(BF16), mv[:, sl]))
    return jnp.concatenate(outs, axis=1)


def _prompt_kernel(x_ref, mk_ref, mv_ref, win_ref, wdt_ref, wdtT_ref, wout_ref, wq_ref, wo_ref,
                   caw_ref, lb_ref, gn_ref, scw_ref, scb_ref, dtb_ref, dtbc_ref, al_ref, alc_ref, dx_ref,
                   snorm_ref, gpre_ref, gpost_ref, gprex_ref, gpostx_ref,
                   masks_ref, tril_ref, tril3_ref, triu3_ref, expand_ref,
                   *rest, T, layer, n_prev):
    (y_ref, ca_ref, hg_ref, sc_ref, ss_ref,
     bufa, bufc, ug_s, z_s, xbc_s, dt_s, dtT_s, mix_s, sthg, stssd) = rest[n_prev:]
    ti = pl.program_id(1)
    n_chunks = T // CH

    @pl.when(ti == 0)
    def _():
        bufa[0:8, :] = jnp.zeros((8, D_A), F32)
        bufc[0:8, :] = jnp.zeros((8, SSD_CONV_DIM), F32)
        sthg[...] = jnp.zeros(sthg.shape, F32)
        stssd[...] = jnp.zeros(stssd.shape, F32)

    x = x_ref[0]
    row = lambda ref: ref[layer:layer + 1, :]
    h = _rms(x, row(gpre_ref)).astype(BF16)

    sxbc = _dot(h, win_ref[:, OFF_XBC:OFF_XBC + SSD_CONV_DIM])
    bufc[8:8 + T, :] = sxbc
    scw = scw_ref[layer]
    xbc = (sxbc * scw[3:4, :] + bufc[7:7 + T, :] * scw[2:3, :] + bufc[6:6 + T, :] * scw[1:2, :]
           + bufc[5:5 + T, :] * scw[0:1, :] + row(scb_ref))
    xbc_s[...] = _silu(xbc)
    sc_ref[0] = sxbc[T - 3:T, :]
    bufc[0:8, :] = sxbc[T - 8:T, :]
    sdt = _dot(h, wdt_ref[...])[:, 0:SSD_HEADS]
    dt_s[...] = _softplus(sdt + row(dtb_ref))
    dtT = _softplus(_dot_nt(wdtT_ref[...], h) + dtbc_ref[:, layer:layer + 1])
    for c in range(n_chunks):
        dtT_s[c] = dtT[:, c * CH:(c + 1) * CH]
    z_s[...] = _dot(h, win_ref[:, OFF_SZ:OFF_SZ + D_SSD])

    ua = _dot(h, win_ref[:, OFF_A:OFF_A + 4 * D_A])
    a_h, a_b, a_c, a_z = (ua[:, k * D_A:(k + 1) * D_A] for k in range(4))
    va = a_c * a_h
    bufa[8:8 + T, :] = va
    caw = caw_ref[layer]
    conv = va * caw[2:3, :] + bufa[7:7 + T, :] * caw[1:2, :] + bufa[6:6 + T, :] * caw[0:1, :]
    mix_s[:, 0:D_A] = (a_b * conv * _silu(a_z)).astype(BF16)
    ca_ref[0] = va[T - 2:T, :]
    bufa[0:8, :] = va[T - 8:T, :]

    ug_s[...] = _dot(h, win_ref[:, OFF_G:OFF_G + 4 * D_HG])


    lb = _hgrn_lower_bound(lb_ref[...], layer)
    a_row = -jnp.exp(row(al_ref))
    a_col = -jnp.exp(alc_ref[:, layer:layer + 1])
    tril = tril_ref[...]
    first_of_pair = lax.broadcasted_iota(jnp.int32, (1, 2 * SSD_P), 1) < SSD_P

    def chunk(c, carry):
        r0 = pl.multiple_of(c * CH, CH)
        rows = pl.ds(r0, CH)

        ug = ug_s[rows, :]
        gq, gf, gi, gz = (ug[:, k * D_HG:(k + 1) * D_HG] for k in range(4))
        f = lb + (1.0 - lb) * _sigmoid(gf)
        logf = jnp.log(f)
        kk = 1.0 - f
        q_b, k_b, v_b = gq.astype(BF16), kk.astype(BF16), gi.astype(BF16)
        hs = [slice(hd * HG_DK, (hd + 1) * HG_DK) for hd in range(HG_HEADS)]
        A = [masks_ref[0] * _dot_nt(q_b[:, s_], k_b[:, s_]) for s_ in hs]
        G = logf
        for li, s in enumerate(HG_LEVELS):
            w, G = _hgrn_level(G, f, gq, kk, s)
            w = w.astype(BF16)
            m = masks_ref[li + 1]
            A = [A[hd] + m * _dot_nt(w[:, hs[hd]], w[:, hs[hd]]) for hd in range(HG_HEADS)]
        g_last = G[CH - 1:CH, :]
        qg = (gq * jnp.exp(G)).astype(BF16)
        kd = (kk * jnp.exp(g_last - G)).astype(BF16)
        dec = jnp.exp(g_last)
        o_heads = []
        for hd in range(HG_HEADS):
            s_ = hs[hd]
            st = sthg[hd]
            o = _dot_nt(qg[:, s_], st.astype(BF16)) + _dot(A[hd].astype(BF16), v_b[:, s_])
            sthg[hd] = st * dec[:, s_] + _dot_tn(v_b[:, s_], kd[:, s_])
            o_heads.append(_rms(o, gn_ref[layer:layer + 1, s_]))
        yb = jnp.concatenate(o_heads, axis=1) * _silu(gz)
        mix_s[rows, D_A:D_A + D_HG] = yb.astype(BF16)

        xbc_c = xbc_s[rows, :]
        xs = xbc_c[:, 0:D_SSD]
        Bm = xbc_c[:, D_SSD:D_SSD + SSD_GROUPS * SSD_N].astype(BF16)
        Cm = xbc_c[:, D_SSD + SSD_GROUPS * SSD_N:].astype(BF16)
        dt = dt_s[rows, :]
        dtT_c = dtT_s[c]
        cs = _dot(tril3_ref[...], _split3_rows(dt * a_row))
        csT = _dot(_split3_cols(dtT_c * a_col), triu3_ref[...])
        cs_last = cs[CH - 1:CH, :]
        w_all = jnp.concatenate([dt * jnp.exp(cs_last - cs), jnp.exp(cs), dt,
                                 jnp.broadcast_to(jnp.exp(cs_last), (8, SSD_HEADS))], axis=0)
        e_all = _dot(_split3_cols(w_all), expand_ref[...])
        e_dec, e_cs, e_dt, e_last = e_all[0:CH], e_all[CH:2 * CH], e_all[2 * CH:3 * CH], e_all[3 * CH:3 * CH + 1]
        xdt = (xs * e_dt).astype(BF16)
        xdec = (xs * e_dec).astype(BF16)
        y_groups = []
        hpg = SSD_HEADS // SSD_GROUPS
        gw = hpg * SSD_P
        for g in range(SSD_GROUPS):
            Cg = Cm[:, g * SSD_N:(g + 1) * SSD_N]
            Bg = Bm[:, g * SSD_N:(g + 1) * SSD_N]
            cb = _dot_nt(Cg, Bg)
            st = stssd[g]
            gcols = slice(g * gw, (g + 1) * gw)
            y_off = _dot(Cg, st.astype(BF16)) * e_cs[:, gcols]
            stssd[g] = st * e_last[:, gcols] + _dot_tn(Bg, xdec[:, gcols])
            pair_out = []
            for pr in range(hpg // 2):
                h0 = g * hpg + 2 * pr
                ms = []
                for hh in (h0, h0 + 1):
                    diff = cs[:, hh:hh + 1] - csT[hh:hh + 1, :]
                    ms.append((cb * (jnp.exp(jnp.minimum(diff, 0.0)) * tril)).astype(BF16))
                both = _dot(jnp.concatenate(ms, axis=0), xdt[:, h0 * SSD_P:(h0 + 2) * SSD_P])
                pair_out.append(jnp.where(first_of_pair, both[0:CH], both[CH:2 * CH]))
            y_groups.append(y_off + jnp.concatenate(pair_out, axis=1))
        y = jnp.concatenate(y_groups, axis=1) + row(dx_ref) * xs
        yc = _rms(y * _silu(z_s[rows, :]), row(snorm_ref))
        mix_s[rows, D_A + D_HG:] = yc.astype(BF16)
        return carry

    lax.fori_loop(0, n_chunks, chunk, 0, unroll=True)

    x1 = x + _rms(_dot(mix_s[...], wout_ref[...]), row(gpost_ref))
    hx = _rms(x1, row(gprex_ref)).astype(BF16)
    q = _dot(hx, wq_ref[...])
    att = _cross_attention(q, mk_ref[0], mv_ref[0])
    y_ref[0] = x1 + _rms(_dot(att.astype(BF16), wo_ref[...]), row(gpostx_ref))

    @pl.when(ti == pl.num_programs(1) - 1)
    def _():
        for hd in range(HG_HEADS):
            hg_ref[0, hd] = sthg[hd].T
        hpg = SSD_HEADS // SSD_GROUPS
        for g in range(SSD_GROUPS):
            sg = stssd[g].T
            for hh in range(hpg):
                ss_ref[0, g * hpg + hh] = sg[hh * SSD_P:(hh + 1) * SSD_P, :]


def _prompt_layer(x, mk, mv, wts, layer, depth, prev, T):
    b, L, _ = x.shape
    prev = [] if prev is None else list(prev)
    c = _consts()
    n_chunks = T // CH
    const_names = ("masks", "tril", "tril3", "triu3", "expand3")
    consts = [c[k] for k in const_names]
    small = [wts[k] for k in ("caw", "lb", "gn", "scw", "scb", "dtb", "dtbc", "al", "alc", "dx", "snorm",
                              "gpre", "gpost", "gprex", "gpostx")]
    big = [wts[k] for k in ("win", "wdt", "wdtT", "wout", "wq", "wo")]

    full = lambda a: pl.BlockSpec(a.shape, lambda bi, ti, _n=a.ndim: (0,) * _n, pipeline_mode=pl.Buffered(1))
    big_spec = lambda a: _layer_spec(a, layer, pipeline_mode=pl.Buffered(1))

    in_specs = ([pl.BlockSpec((1, T, D_MODEL), lambda bi, ti: (bi, ti, 0)),
                 pl.BlockSpec((1, N_MEM, D_MODEL), lambda bi, ti: (bi, 0, 0)),
                 pl.BlockSpec((1, N_MEM, D_MODEL), lambda bi, ti: (bi, 0, 0))]
                + [big_spec(a) for a in big] + [full(a) for a in small] + [full(a) for a in consts]
                + [_ANY] * len(prev))
    n_in = len(in_specs)
    state_shapes = [(CONV_A_W - 1, D_A), (HG_HEADS, HG_DK, HG_DK), (SSD_CONV_W - 1, SSD_CONV_DIM),
                    (SSD_HEADS, SSD_P, SSD_N)]
    out_shape = ([jax.ShapeDtypeStruct((b, L, D_MODEL), F32)]
                 + [jax.ShapeDtypeStruct((depth, b) + s, F32) for s in state_shapes])
    out_specs = ([pl.BlockSpec((1, T, D_MODEL), lambda bi, ti: (bi, ti, 0))]
                 + [pl.BlockSpec((None, 1) + s, lambda bi, ti, _n=len(s): (layer, bi) + (0,) * _n)
                    for s in state_shapes])
    aliases = {n_in - len(prev) + k: 1 + k for k in range(len(prev))}
    scratch = [pltpu.VMEM((8 + T, D_A), F32), pltpu.VMEM((8 + T, SSD_CONV_DIM), F32),
               pltpu.VMEM((T, 4 * D_HG), F32), pltpu.VMEM((T, D_SSD), F32), pltpu.VMEM((T, SSD_CONV_DIM), F32),
               pltpu.VMEM((T, SSD_HEADS), F32), pltpu.VMEM((n_chunks, SSD_HEADS, CH), F32),
               pltpu.VMEM((T, 2 * D_MODEL), BF16),
               pltpu.VMEM((HG_HEADS, HG_DK, HG_DK), F32),
               pltpu.VMEM((SSD_GROUPS, SSD_N, (SSD_HEADS // SSD_GROUPS) * SSD_P), F32)]
    return pl.pallas_call(
        functools.partial(_prompt_kernel, T=T, layer=layer, n_prev=len(prev)),
        grid=(b, L // T),
        in_specs=in_specs, out_specs=out_specs, out_shape=out_shape, scratch_shapes=scratch,
        input_output_aliases=aliases,
        compiler_params=pltpu.CompilerParams(dimension_semantics=("arbitrary", "arbitrary"),
                                             vmem_limit_bytes=VMEM_LIMIT),
        name=f"prompt_layer{layer}",
    )(x, mk, mv, *big, *small, *consts, *prev)


def _prep_weights(w_in, conv_a_w, hgrn_lb, hgrn_gnorm, ssd_conv_w, ssd_conv_b, ssd_dt_bias, ssd_A_log, ssd_D, ssd_norm,
                  w_out, g_pre_mix, g_post_mix, g_pre_x, g_post_x, g_mem, w_q, w_k, w_v, w_o):
    wdt = jnp.pad(w_in[:, :, OFF_DT:].astype(BF16), ((0, 0), (0, 0), (0, 128 - SSD_HEADS)))
    return dict(
        win=w_in.astype(BF16), wdt=wdt,
        wdtT=jnp.transpose(w_in[:, :, OFF_DT:OFF_DT + SSD_HEADS], (0, 2, 1)).astype(BF16),
        wout=w_out.astype(BF16), wq=w_q.astype(BF16), wo=w_o.astype(BF16), wk=w_k.astype(BF16), wv=w_v.astype(BF16),
        caw=conv_a_w, lb=hgrn_lb, gn=hgrn_gnorm, scw=ssd_conv_w, scb=ssd_conv_b,
        dtb=ssd_dt_bias, dtbc=ssd_dt_bias.T, al=ssd_A_log, alc=ssd_A_log.T,
        dx=jnp.repeat(ssd_D, SSD_P, axis=1), snorm=ssd_norm,
        gpre=g_pre_mix, gpost=g_post_mix, gprex=g_pre_x, gpostx=g_post_x, gmem=g_mem)


SB = 8
D_HGP = 4 * D_HG
D_SSP = 4 * D_SSD + 2 * SSD_GROUPS * SSD_N
D_COLS = D_HG + 2 * D_SSD


def _sample_pre_kernel(x_ref, ca_ref, sc_ref, win_ref, caw_ref, lb_ref, scw_ref, scb_ref, dtb_ref, al_ref,
                       gpre_ref, expand_ref,
                       ya_ref, canew_ref, hgp_ref, ssp_ref, scnew_ref, cols_ref, *, layer):
    row = lambda ref: ref[layer:layer + 1, :]
    h = _rms(x_ref[...], row(gpre_ref)).astype(BF16)
    u = _dot(h, win_ref[...])
    a_h, a_b, a_c, a_z = (u[:, OFF_A + k * D_A:OFF_A + (k + 1) * D_A] for k in range(4))
    va = a_c * a_h
    p0, p1 = ca_ref[:, 0:D_A], ca_ref[:, D_A:2 * D_A]
    caw = caw_ref[layer]
    conv = va * caw[2:3, :] + p1 * caw[1:2, :] + p0 * caw[0:1, :]
    ya_ref[...] = a_b * conv * _silu(a_z)
    canew_ref[:, 0:D_A] = p1
    canew_ref[:, D_A:2 * D_A] = va
    lb = _hgrn_lower_bound(lb_ref[...], layer)
    gq, gf, gi, gz = (u[:, OFF_G + k * D_HG:OFF_G + (k + 1) * D_HG] for k in range(4))
    f = lb + (1.0 - lb) * _sigmoid(gf)
    hgp_ref[:, 0:D_HG] = gq
    hgp_ref[:, D_HG:2 * D_HG] = f
    hgp_ref[:, 2 * D_HG:3 * D_HG] = gi
    hgp_ref[:, 3 * D_HG:] = gz
    sxbc = u[:, OFF_XBC:OFF_XBC + SSD_CONV_DIM]
    W = SSD_CONV_DIM
    q0, q1, q2 = sc_ref[:, 0:W], sc_ref[:, W:2 * W], sc_ref[:, 2 * W:3 * W]
    scw = scw_ref[layer]
    xbc = _silu(sxbc * scw[3:4, :] + q2 * scw[2:3, :] + q1 * scw[1:2, :] + q0 * scw[0:1, :] + row(scb_ref))
    scnew_ref[:, 0:W] = q1
    scnew_ref[:, W:2 * W] = q2
    scnew_ref[:, 2 * W:3 * W] = sxbc
    xs = xbc[:, 0:D_SSD]
    dt = _softplus(u[:, OFF_DT:OFF_DT + SSD_HEADS] + row(dtb_ref))
    dec = jnp.exp(dt * -jnp.exp(row(al_ref)))
    n = dt.shape[0]
    e_all = _dot(_split3_cols(jnp.concatenate([dt, dec], axis=0)), expand_ref[...])
    xdt, decx = xs * e_all[0:n], e_all[n:2 * n]
    ssp_ref[:, 0:D_SSD] = xs
    ssp_ref[:, D_SSD:2 * D_SSD] = xdt
    ssp_ref[:, 2 * D_SSD:3 * D_SSD] = decx
    ssp_ref[:, 3 * D_SSD:4 * D_SSD] = u[:, OFF_SZ:OFF_SZ + D_SSD]
    ssp_ref[:, 4 * D_SSD:] = xbc[:, D_SSD:]
    cols_t = jnp.concatenate([f, xdt, decx], axis=1).T
    for i in range(n // SB):
        cols_ref[i] = cols_t[:, i * SB:(i + 1) * SB]


def _sample_pre(x, ca, sc, wts, layer):
    n = x.shape[0]
    args = [x, ca, sc, wts["win"], wts["caw"], wts["lb"], wts["scw"], wts["scb"], wts["dtb"], wts["al"],
            wts["gpre"], _consts()["expand3"]]
    out_shape = [jax.ShapeDtypeStruct((n, D_A), F32), jax.ShapeDtypeStruct((n, 2 * D_A), F32),
                 jax.ShapeDtypeStruct((n, D_HGP), F32), jax.ShapeDtypeStruct((n, D_SSP), F32),
                 jax.ShapeDtypeStruct((n, 3 * SSD_CONV_DIM), F32),
                 jax.ShapeDtypeStruct((n // SB, D_COLS, SB), F32)]
    return pl.pallas_call(
        functools.partial(_sample_pre_kernel, layer=layer),
        in_specs=[_full_spec(a) for a in args[:3]] + [_layer_spec(args[3], layer)] + [_full_spec(a) for a in args[4:]],
        out_specs=[_full_spec(s) for s in out_shape],
        out_shape=out_shape, grid=(1,),
        compiler_params=pltpu.CompilerParams(dimension_semantics=("arbitrary",), vmem_limit_bytes=VMEM_LIMIT),
        name=f"sample_pre{layer}",
    )(*args)


def _sample_state_kernel(hgp_ref, ssp_ref, cols_ref, shg_ref, sss_ref, *rest):
    o_ref, y_ref, shg_out, sss_out = rest[-4:]
    rid_hg = lax.broadcasted_iota(jnp.int32, (SB, HG_DK), 0)
    for hd in range(HG_HEADS):
        cols = slice(hd * HG_DK, (hd + 1) * HG_DK)
        q_b = hgp_ref[:, cols].astype(BF16)
        o = jnp.zeros((SB, HG_DK), F32)
        for j in range(SB):
            fcol = cols_ref[hd * HG_DK:(hd + 1) * HG_DK, j:j + 1]
            vrow = hgp_ref[j:j + 1, 2 * D_HG + hd * HG_DK:2 * D_HG + (hd + 1) * HG_DK]
            s_new = fcol * shg_ref[j, hd] + (1.0 - fcol) * vrow
            shg_out[j, hd] = s_new
            o = jnp.where(rid_hg == j, _dot(q_b, s_new.astype(BF16)), o)
        o_ref[:, cols] = o
    gw = (SSD_HEADS // SSD_GROUPS) * SSD_P
    rid_ss = lax.broadcasted_iota(jnp.int32, (SB, gw), 0)
    for g in range(SSD_GROUPS):
        rows = slice(g * gw, (g + 1) * gw)
        c_b = ssp_ref[:, 4 * D_SSD + (SSD_GROUPS + g) * SSD_N:4 * D_SSD + (SSD_GROUPS + g + 1) * SSD_N].astype(BF16)
        y = jnp.zeros((SB, gw), F32)
        for j in range(SB):
            brow = ssp_ref[j:j + 1, 4 * D_SSD + g * SSD_N:4 * D_SSD + (g + 1) * SSD_N]
            xcol = cols_ref[D_HG + g * gw:D_HG + (g + 1) * gw, j:j + 1]
            dcol = cols_ref[D_HG + D_SSD + g * gw:D_HG + D_SSD + (g + 1) * gw, j:j + 1]
            s_new = dcol * sss_ref[j, rows, :] + xcol * brow
            sss_out[j, rows, :] = s_new
            y = jnp.where(rid_ss == j, _dot_nt(c_b, s_new.astype(BF16)), y)
        y_ref[:, rows] = y


def _sample_state(hgp, ssp, cols, shg_all, sss_all, layer, prev):
    n = hgp.shape[0]
    prev = [] if prev is None else list(prev)
    rowblk = lambda w: pl.BlockSpec((SB, w), lambda i: (i, 0))
    hg_blk = pl.BlockSpec((None, SB, HG_HEADS, HG_DK, HG_DK), lambda i: (layer, i, 0, 0, 0))
    ss_blk = pl.BlockSpec((None, SB, SSD_HEADS * SSD_P, SSD_N), lambda i: (layer, i, 0, 0))
    return pl.pallas_call(
        _sample_state_kernel,
        grid=(n // SB,),
        in_specs=[rowblk(D_HGP), rowblk(D_SSP), pl.BlockSpec((None, D_COLS, SB), lambda i: (i, 0, 0)), hg_blk, ss_blk]
        + [_ANY] * len(prev),
        out_specs=[rowblk(D_HG), rowblk(D_SSD), hg_blk, ss_blk],
        out_shape=[jax.ShapeDtypeStruct((n, D_HG), F32), jax.ShapeDtypeStruct((n, D_SSD), F32),
                   jax.ShapeDtypeStruct(shg_all.shape, F32), jax.ShapeDtypeStruct(sss_all.shape, F32)],
        input_output_aliases={5 + k: 2 + k for k in range(len(prev))},
        compiler_params=pltpu.CompilerParams(dimension_semantics=("arbitrary",), vmem_limit_bytes=VMEM_LIMIT),
        name="sample_state",
    )(hgp, ssp, cols, shg_all, sss_all, *prev)


def _sample_mid_kernel(x_ref, ya_ref, o_ref, y_ref, hgp_ref, ssp_ref, wout_ref, wq_ref, gn_ref, dx_ref, snorm_ref,
                       gpost_ref, gprex_ref, x1_ref, q_ref, *, layer):
    row = lambda ref: ref[layer:layer + 1, :]
    gz = hgp_ref[:, 3 * D_HG:]
    o = o_ref[...]
    yb = jnp.concatenate([_rms(o[:, hd * HG_DK:(hd + 1) * HG_DK], gn_ref[layer:layer + 1, hd * HG_DK:(hd + 1) * HG_DK])
                          for hd in range(HG_HEADS)], axis=1) * _silu(gz)
    y = y_ref[...] + row(dx_ref) * ssp_ref[:, 0:D_SSD]
    yc = _rms(y * _silu(ssp_ref[:, 3 * D_SSD:4 * D_SSD]), row(snorm_ref))
    mix = jnp.concatenate([ya_ref[...], yb, yc], axis=1).astype(BF16)
    x1 = x_ref[...] + _rms(_dot(mix, wout_ref[...]), row(gpost_ref))
    x1_ref[...] = x1
    q = _dot(_rms(x1, row(gprex_ref)).astype(BF16), wq_ref[...])
    for hd in range(XA_HEADS):
        for k in range(XA_HD // 128):
            q_ref[:, k * XA_HEADS + hd, :] = q[:, hd * XA_HD + k * 128:hd * XA_HD + (k + 1) * 128]


def _sample_mid(x, ya, o, y, hgp, ssp, wts, layer):
    n = x.shape[0]
    args = [x, ya, o, y, hgp, ssp, wts["wout"], wts["wq"], wts["gn"], wts["dx"], wts["snorm"], wts["gpost"],
            wts["gprex"]]
    out_shape = [jax.ShapeDtypeStruct((n, D_MODEL), F32),
                 jax.ShapeDtypeStruct((n, XA_HEADS * (XA_HD // 128), 128), F32)]
    return pl.pallas_call(
        functools.partial(_sample_mid_kernel, layer=layer), grid=(1,),
        in_specs=[_full_spec(a) for a in args[:6]] + [_layer_spec(a, layer) for a in args[6:8]]
        + [_full_spec(a) for a in args[8:]], out_specs=[_full_spec(s) for s in out_shape], out_shape=out_shape,
        compiler_params=pltpu.CompilerParams(dimension_semantics=("arbitrary",), vmem_limit_bytes=VMEM_LIMIT),
        name="sample_mid",
    )(*args)


def _lane_class_reduce(x, op):
    sh = KV_SUB
    while sh < 128:
        x = op(x, pltpu.roll(x, sh, axis=1))
        sh *= 2
    return x


def _sample_attn_kernel(x1_ref, q_ref, k_ref, v_ref, wo_ref, gpostx_ref, x2_ref, *, layer):
    lane = lax.broadcasted_iota(jnp.int32, (KV_SUB, KV_ROWS), 1)
    sub = lax.broadcasted_iota(jnp.int32, (KV_SUB, KV_ROWS), 0)
    own = ((lane & (KV_SUB - 1)) == sub).astype(F32)
    rid = lax.broadcasted_iota(jnp.int32, (SB, KV_ROWS), 0)
    t_all = jnp.zeros((SB, KV_ROWS), F32)
    for j in range(SB):
        r = _dot_nt(q_ref[j].astype(BF16), k_ref[j].astype(BF16))
        t = jnp.sum(r * own, axis=0, keepdims=True)
        t_all = jnp.where(rid == j, t, t_all)
    n_tiles = KV_ROWS // 128
    lane1 = lax.broadcasted_iota(jnp.int32, (SB, 128), 1)
    piece = (lane1 // XA_HEADS) % KV_SPLIT
    chunks = []
    for c in range(n_tiles):
        x = t_all[:, c * 128:(c + 1) * 128]
        tot = x
        for k in range(1, KV_SPLIT):
            fwd = pltpu.roll(x, 128 - k * XA_HEADS, axis=1)
            bwd = pltpu.roll(x, (KV_SPLIT - k) * XA_HEADS, axis=1)
            tot = tot + jnp.where(piece + k < KV_SPLIT, fwd, bwd)
        chunks.append(tot * (XA_HD ** -0.5))
    mx = _lane_class_reduce(functools.reduce(jnp.maximum, chunks), jnp.maximum)
    es = [jnp.exp(ch - mx) for ch in chunks]
    den = _lane_class_reduce(functools.reduce(lambda a, b: a + b, es), lambda a, b: a + b)
    p_all = jnp.concatenate([e * (1.0 / den) for e in es], axis=1)
    rid_o = lax.broadcasted_iota(jnp.int32, (SB, D_MODEL), 0)
    att = jnp.zeros((SB, D_MODEL), F32)
    for j in range(SB):
        p8 = (own * p_all[j:j + 1, :]).astype(BF16)
        o = _dot(p8, v_ref[j].astype(BF16))
        row = jnp.concatenate([o[k * XA_HEADS + hd:k * XA_HEADS + hd + 1, :]
                               for hd in range(XA_HEADS) for k in range(KV_SPLIT)], axis=1)
        att = jnp.where(rid_o == j, row, att)
    x2_ref[...] = x1_ref[...] + _rms(_dot(att.astype(BF16), wo_ref[...]), gpostx_ref[layer:layer + 1, :])


def _sample_attn(x1, q8, ck_rows, cv_rows, wts, layer):
    n = x1.shape[0]
    rowblk = pl.BlockSpec((SB, D_MODEL), lambda i: (i, 0))
    qblk = pl.BlockSpec((SB, KV_SUB, 128), lambda i: (i, 0, 0))
    kvblk = pl.BlockSpec((None, SB, KV_ROWS, 128), lambda i: (layer, i, 0, 0))
    return pl.pallas_call(
        functools.partial(_sample_attn_kernel, layer=layer),
        grid=(n // SB,),
        in_specs=[rowblk, qblk, kvblk, kvblk, _layer_spec(wts["wo"], layer), _full_spec(wts["gpostx"])],
        out_specs=rowblk,
        out_shape=jax.ShapeDtypeStruct((n, D_MODEL), F32),
        compiler_params=pltpu.CompilerParams(dimension_semantics=("arbitrary",), vmem_limit_bytes=VMEM_LIMIT),
        name="sample_attn",
    )(x1, q8, ck_rows, cv_rows, wts["wo"], wts["gpostx"])


def _sample_layer(x, ca, shg_all, sc, sss_all, ck_rows, cv_rows, wts, layer, prev_states):
    n = x.shape[0]
    ya, ca_new, hgp, ssp, sc_new, cols = _sample_pre(x, ca.reshape(n, -1), sc.reshape(n, -1), wts, layer)
    o, y, shg_new, sss_new = _sample_state(hgp, ssp, cols, shg_all, sss_all, layer, prev_states)
    x1, q8 = _sample_mid(x, ya, o, y, hgp, ssp, wts, layer)
    x2 = _sample_attn(x1, q8, ck_rows, cv_rows, wts, layer)
    return x2, ca_new.reshape(ca.shape), sc_new.reshape(sc.shape), (shg_new, sss_new)


PROMPT_TILE = 512


def kernel(x_prompt, x_sample, mem_prompt, state_conv_a, state_hgrn, state_ssd_conv, state_ssd, cache_mem_k,
           cache_mem_v, w_in, conv_a_w, hgrn_lb, hgrn_gnorm, ssd_conv_w, ssd_conv_b, ssd_dt_bias, ssd_A_log, ssd_D,
           ssd_norm, w_out, g_pre_mix, g_post_mix, g_pre_x, g_post_x, g_mem, w_q, w_k, w_v, w_o):
    depth = w_in.shape[0]
    n = x_sample.shape[0]
    yp = x_prompt
    ys = x_sample.reshape(n, D_MODEL)
    ck_rows, cv_rows = _kv_rows_view(cache_mem_k), _kv_rows_view(cache_mem_v)
    sss_all = state_ssd.reshape(depth, n, SSD_HEADS * SSD_P, SSD_N)
    kv_rows = p_states = s_states = None
    s_ca, s_sc = [], []
    wts = _prep_weights(w_in, conv_a_w, hgrn_lb, hgrn_gnorm, ssd_conv_w, ssd_conv_b, ssd_dt_bias, ssd_A_log, ssd_D,
                        ssd_norm, w_out, g_pre_mix, g_post_mix, g_pre_x, g_post_x, g_mem, w_q, w_k, w_v, w_o)
    for l in range(depth):
        *kv_rows, mk, mv = _memory_kv(mem_prompt, wts, l, depth, kv_rows)
        yp, *p_states = _prompt_layer(yp, mk, mv, wts, l, depth, p_states, PROMPT_TILE)
        ys, ca, sc, s_states = _sample_layer(ys, state_conv_a[l], state_hgrn, state_ssd_conv[l], sss_all,
                                             ck_rows, cv_rows, wts, l, s_states)
        s_ca.append(ca)
        s_sc.append(sc)
    p_ca, p_hg, p_sc, p_ss = p_states
    s_hg, s_ss = s_states
    return (yp, ys.reshape(x_sample.shape), p_ca, p_hg, p_sc, p_ss, _kv_from_rows(kv_rows[0]),
            _kv_from_rows(kv_rows[1]), jnp.stack(s_ca), s_hg, jnp.stack(s_sc), s_ss.reshape(state_ssd.shape))
```

```python
import functools

import numpy as np
import jax
import jax.numpy as jnp
from jax import lax
from jax.experimental import pallas as pl
from jax.experimental.pallas import tpu as pltpu

F32 = jnp.float32
BF16 = jnp.bfloat16

D_MODEL = 1024
D_A = 512
CONV_A_W = 3
D_HG = 512
HG_HEADS = 4
HG_DK = 128
D_SSD = 1024
SSD_P = 64
SSD_HEADS = 16
SSD_GROUPS = 2
SSD_N = 128
SSD_CONV_W = 4
SSD_CONV_DIM = D_SSD + 2 * SSD_GROUPS * SSD_N
N_MEM = 256
XA_HEADS = 4
XA_HD = 256
EPS = 1e-6
KV_SPLIT = XA_HD // 128
KV_SUB = XA_HEADS * KV_SPLIT
KV_ROWS = N_MEM * KV_SUB

OFF_A = 0
OFF_G = 2048
OFF_SZ = 4096
OFF_XBC = 5120
OFF_DT = 6656
D_IN = 6672

CH = 128
HG_LEVELS = (1, 2, 4, 8, 16, 32, 64)
VMEM_LIMIT = 56 * 1024 * 1024


def _rms(x, g):
    ms = jnp.mean(x * x, axis=-1, keepdims=True)
    return x * lax.rsqrt(ms + EPS) * g


def _silu(x):
    return x * (1.0 / (1.0 + jnp.exp(-x)))


def _sigmoid(x):
    return 1.0 / (1.0 + jnp.exp(-x))


def _softplus(x):
    return jnp.maximum(x, 0.0) + jnp.log(1.0 + jnp.exp(-jnp.abs(x)))


def _dot(a, b):
    return jnp.dot(a, b, preferred_element_type=F32)


def _dot_nt(a, b):
    return lax.dot_general(a, b, (((1,), (1,)), ((), ())), preferred_element_type=F32)


def _dot_tn(a, b):
    return lax.dot_general(a, b, (((0,), (0,)), ((), ())), preferred_element_type=F32)


def _split3(x):
    hi = x.astype(BF16)
    r = x - hi.astype(F32)
    mid = r.astype(BF16)
    lo = (r - mid.astype(F32)).astype(BF16)
    return hi, mid, lo


def _split3_rows(x):
    return jnp.concatenate(_split3(x), axis=0)


def _split3_cols(x):
    return jnp.concatenate(_split3(x), axis=1)


@functools.lru_cache(maxsize=None)
def _consts():
    r = np.arange(CH)
    i, t = r[:, None], r[None, :]
    masks = [np.eye(CH, dtype=bool)]
    for s in HG_LEVELS:
        up = ((r // s) % 2 == 1)
        same = (i // (2 * s)) == (t // (2 * s))
        masks.append(same & up[:, None] & (~up)[None, :])
    masks = np.stack(masks).astype(np.float32)
    tril = (t <= i).astype(np.float32)
    tril3 = np.tile(tril, (1, 3))
    triu3 = np.tile(tril.T, (3, 1))
    e = (np.arange(D_SSD)[None, :] // SSD_P == np.arange(SSD_HEADS)[:, None]).astype(np.float32)
    expand3 = np.tile(e, (3, 1))
    return dict(
        masks=jnp.asarray(masks, F32),
        tril=jnp.asarray(tril, F32), tril3=jnp.asarray(tril3, BF16), triu3=jnp.asarray(triu3, BF16),
        expand3=jnp.asarray(expand3, BF16))


def _kv_rows_view(c_all):
    depth, n = c_all.shape[:2]
    c = c_all.reshape(depth, n, N_MEM, XA_HEADS, KV_SPLIT, 128)
    return jnp.transpose(c, (0, 1, 2, 4, 3, 5)).reshape(depth, n, KV_ROWS, 128)


def _kv_from_rows(r_all):
    depth, n = r_all.shape[:2]
    c = r_all.reshape(depth, n, N_MEM, KV_SPLIT, XA_HEADS, 128)
    return jnp.transpose(c, (0, 1, 2, 4, 3, 5)).reshape(depth, n, N_MEM, XA_HEADS, XA_HD)


def _store_kv_rows(r_ref, x):
    for hd in range(XA_HEADS):
        for k in range(KV_SPLIT):
            r_ref[:, k * XA_HEADS + hd, :] = x[:, hd * XA_HD + k * 128:hd * XA_HD + (k + 1) * 128]


_ANY = pl.BlockSpec(memory_space=pl.ANY)


def _full_spec(a):
    nd = a.ndim
    return pl.BlockSpec(a.shape, lambda *_, _n=nd: (0,) * _n)


def _layer_spec(a, layer, **kw):
    nd = a.ndim
    return pl.BlockSpec((None,) + tuple(a.shape[1:]), lambda *_, _n=nd: (layer,) + (0,) * (_n - 1), **kw)


def _memkv_kernel(mem_ref, g_ref, wk_ref, wv_ref, *refs, layer):
    kr_ref, vr_ref, kb_ref, vb_ref = refs[-4:]
    m = _rms(mem_ref[0], g_ref[layer:layer + 1, :]).astype(BF16)
    for w_ref, r_ref, b_ref in ((wk_ref, kr_ref, kb_ref), (wv_ref, vr_ref, vb_ref)):
        kv = _dot(m, w_ref[...])
        b_ref[0] = kv.astype(BF16)
        _store_kv_rows(r_ref, kv)


def _memory_kv(mem, wts, layer, depth, prev):
    b = mem.shape[0]
    blk = pl.BlockSpec((1, N_MEM, D_MODEL), lambda i: (i, 0, 0))
    rows_blk = pl.BlockSpec((None, None, N_MEM, KV_SUB, 128), lambda i: (layer, i, 0, 0, 0))
    rows_sds = jax.ShapeDtypeStruct((depth, b, N_MEM, KV_SUB, 128), F32)
    extra, extra_specs, aliases = [], [], {}
    if prev is not None:
        extra, extra_specs, aliases = list(prev), [_ANY, _ANY], {4: 0, 5: 1}
    return pl.pallas_call(
        functools.partial(_memkv_kernel, layer=layer),
        grid=(b,),
        in_specs=[blk, _full_spec(wts["gmem"]), _layer_spec(wts["wk"], layer), _layer_spec(wts["wv"], layer)]
        + extra_specs,
        out_specs=[rows_blk, rows_blk, blk, blk],
        out_shape=[rows_sds, rows_sds] + [jax.ShapeDtypeStruct((b, N_MEM, D_MODEL), BF16)] * 2,
        input_output_aliases=aliases,
        compiler_params=pltpu.CompilerParams(dimension_semantics=("arbitrary",), vmem_limit_bytes=VMEM_LIMIT),
        name="memory_kv",
    )(mem, wts["gmem"], wts["wk"], wts["wv"], *extra)


def _hgrn_lower_bound(lb_all, layer):
    depth = lb_all.shape[0]
    rows = [lb_all[j:j + 1, :] for j in range(depth)]
    mx = functools.reduce(jnp.maximum, rows)
    ex = [jnp.exp(rw - mx) for rw in rows]
    tot = functools.reduce(lambda a, b: a + b, ex)
    acc = jnp.zeros_like(tot)
    for j in range(1, layer + 1):
        acc = acc + ex[j]
    return acc / tot


def _hgrn_level(c, f, q, k, s):
    n, w = c.shape
    if s >= 8:
        nb = n // (2 * s)
        c4, q4, k4 = (a.reshape(nb, 2, s, w) for a in (c, q, k))
        lower, upper = c4[:, 0], c4[:, 1]
        tot = lower[:, s - 1:s, :]
        w_lower = k4[:, 0] * jnp.exp(tot - lower)
        w_upper = q4[:, 1] * jnp.exp(upper)
        wv = jnp.stack([w_lower, w_upper], axis=1).reshape(n, w)
        c_next = jnp.stack([lower, upper + tot], axis=1).reshape(n, w)
        return wv, c_next
    sub = lax.broadcasted_iota(jnp.int32, (1, 8, w), 1)
    c3, f3, q3, k3 = (a.reshape(n // 8, 8, w) for a in (c, f, q, k))
    up = (sub // s) % 2 == 1
    tot = None
    for gi in reversed(range(8 // (2 * s))):
        r = gi * 2 * s + s - 1
        tg = jnp.broadcast_to(c3[:, r:r + 1, :], c3.shape)
        tot = tg if tot is None else jnp.where(sub < (gi + 1) * 2 * s, tg, tot)
    if s == 1:
        e = jnp.where(up, f3, 1.0)
    else:
        e = jnp.exp(jnp.where(up, c3, tot - c3))
    wv = jnp.where(up, q3, k3) * e
    c_next = c3 + jnp.where(up, tot, 0.0)
    return wv.reshape(n, w), c_next.reshape(n, w)


def _cross_attention(q, mk, mv):
    outs = []
    for hd in range(XA_HEADS):
        sl = slice(hd * XA_HD, (hd + 1) * XA_HD)
        s = _dot_nt(q[:, sl].astype(BF16), mk[:, sl]) * (XA_HD ** -0.5)
        s = s - jnp.max(s, axis=-1, keepdims=True)
        e = jnp.exp(s)
        p = e * (1.0 / jnp.sum(e, axis=-1, keepdims=True))
        outs.append(_dot(p.astype(BF16), mv[:, sl]))
    return jnp.concatenate(outs, axis=1)


def _prompt_kernel(x_ref, mk_ref, mv_ref, winT_ref, wout_ref, wq_ref, wo_ref,
                   caw_ref, lb_ref, gn_ref, scw_ref, scb_ref, dtb_ref, dtbc_ref, al_ref, alc_ref, dx_ref,
                   snorm_ref, gpre_ref, gpost_ref, gprex_ref, gpostx_ref,
                   masks_ref, tril_ref, tril3_ref, triu3_ref, expand_ref,
                   *rest, T, layer, n_prev):
    (y_ref, ca_ref, hg_ref, sc_ref, ss_ref,
     bufa, bufc, ug_s, z_s, xbc_s, dt_s, dtT_s, mix_s, sthg, stssd) = rest[n_prev:]
    ti = pl.program_id(1)
    n_chunks = T // CH

    @pl.when(ti == 0)
    def _():
        bufa[0:8, :] = jnp.zeros((8, D_A), F32)
        bufc[0:8, :] = jnp.zeros((8, SSD_CONV_DIM), F32)
        sthg[...] = jnp.zeros(sthg.shape, F32)
        stssd[...] = jnp.zeros(stssd.shape, F32)

    x = x_ref[0]
    row = lambda ref: ref[layer:layer + 1, :]
    h = _rms(x, row(gpre_ref)).astype(BF16)

    sxbc = _dot_nt(h, winT_ref[OFF_XBC:OFF_XBC + SSD_CONV_DIM, :])
    bufc[8:8 + T, :] = sxbc
    scw = scw_ref[layer]
    xbc = (sxbc * scw[3:4, :] + bufc[7:7 + T, :] * scw[2:3, :] + bufc[6:6 + T, :] * scw[1:2, :]
           + bufc[5:5 + T, :] * scw[0:1, :] + row(scb_ref))
    xbc_s[...] = _silu(xbc)
    sc_ref[0] = sxbc[T - 3:T, :]
    bufc[0:8, :] = sxbc[T - 8:T, :]
    wdtT = winT_ref[OFF_DT:OFF_DT + SSD_HEADS, :]
    sdt = _dot_nt(h, wdtT)
    dt_s[...] = _softplus(sdt + row(dtb_ref))
    dtT = _softplus(_dot_nt(wdtT, h) + dtbc_ref[:, layer:layer + 1])
    for c in range(n_chunks):
        dtT_s[c] = dtT[:, c * CH:(c + 1) * CH]
    z_s[...] = _dot_nt(h, winT_ref[OFF_SZ:OFF_SZ + D_SSD, :])

    ua = _dot_nt(h, winT_ref[OFF_A:OFF_A + 4 * D_A, :])
    a_h, a_b, a_c, a_z = (ua[:, k * D_A:(k + 1) * D_A] for k in range(4))
    va = a_c * a_h
    bufa[8:8 + T, :] = va
    caw = caw_ref[layer]
    conv = va * caw[2:3, :] + bufa[7:7 + T, :] * caw[1:2, :] + bufa[6:6 + T, :] * caw[0:1, :]
    mix_s[:, 0:D_A] = (a_b * conv * _silu(a_z)).astype(BF16)
    ca_ref[0] = va[T - 2:T, :]
    bufa[0:8, :] = va[T - 8:T, :]

    ug_s[...] = _dot_nt(h, winT_ref[OFF_G:OFF_G + 4 * D_HG, :])


    lb = _hgrn_lower_bound(lb_ref[...], layer)
    a_row = -jnp.exp(row(al_ref))
    a_col = -jnp.exp(alc_ref[:, layer:layer + 1])
    tril = tril_ref[...]
    first_of_pair = lax.broadcasted_iota(jnp.int32, (1, 2 * SSD_P), 1) < SSD_P

    def chunk(c, carry):
        r0 = pl.multiple_of(c * CH, CH)
        rows = pl.ds(r0, CH)

        ug = ug_s[rows, :]
        gq, gf, gi, gz = (ug[:, k * D_HG:(k + 1) * D_HG] for k in range(4))
        f = lb + (1.0 - lb) * _sigmoid(gf)
        logf = jnp.log(f)
        kk = 1.0 - f
        q_b, k_b, v_b = gq.astype(BF16), kk.astype(BF16), gi.astype(BF16)
        hs = [slice(hd * HG_DK, (hd + 1) * HG_DK) for hd in range(HG_HEADS)]
        A = [masks_ref[0] * _dot_nt(q_b[:, s_], k_b[:, s_]) for s_ in hs]
        G = logf
        for li, s in enumerate(HG_LEVELS):
            w, G = _hgrn_level(G, f, gq, kk, s)
            w = w.astype(BF16)
            m = masks_ref[li + 1]
            A = [A[hd] + m * _dot_nt(w[:, hs[hd]], w[:, hs[hd]]) for hd in range(HG_HEADS)]
        g_last = G[CH - 1:CH, :]
        qg = (gq * jnp.exp(G)).astype(BF16)
        kd = (kk * jnp.exp(g_last - G)).astype(BF16)
        dec = jnp.exp(g_last)
        o_heads = []
        for hd in range(HG_HEADS):
            s_ = hs[hd]
            st = sthg[hd]
            o = _dot_nt(qg[:, s_], st.astype(BF16)) + _dot(A[hd].astype(BF16), v_b[:, s_])
            sthg[hd] = st * dec[:, s_] + _dot_tn(v_b[:, s_], kd[:, s_])
            o_heads.append(_rms(o, gn_ref[layer:layer + 1, s_]))
        yb = jnp.concatenate(o_heads, axis=1) * _silu(gz)
        mix_s[rows, D_A:D_A + D_HG] = yb.astype(BF16)

        xbc_c = xbc_s[rows, :]
        xs = xbc_c[:, 0:D_SSD]
        Bm = xbc_c[:, D_SSD:D_SSD + SSD_GROUPS * SSD_N].astype(BF16)
        Cm = xbc_c[:, D_SSD + SSD_GROUPS * SSD_N:].astype(BF16)
        dt = dt_s[rows, :]
        dtT_c = dtT_s[c]
        cs = _dot(tril3_ref[...], _split3_rows(dt * a_row))
        csT = _dot(_split3_cols(dtT_c * a_col), triu3_ref[...])
        cs_last = cs[CH - 1:CH, :]
        w_all = jnp.concatenate([dt * jnp.exp(cs_last - cs), jnp.exp(cs), dt,
                                 jnp.broadcast_to(jnp.exp(cs_last), (8, SSD_HEADS))], axis=0)
        e_all = _dot(_split3_cols(w_all), expand_ref[...])
        e_dec, e_cs, e_dt, e_last = e_all[0:CH], e_all[CH:2 * CH], e_all[2 * CH:3 * CH], e_all[3 * CH:3 * CH + 1]
        xdt = (xs * e_dt).astype(BF16)
        xdec = (xs * e_dec).astype(BF16)
        y_groups = []
        hpg = SSD_HEADS // SSD_GROUPS
        gw = hpg * SSD_P
        for g in range(SSD_GROUPS):
            Cg = Cm[:, g * SSD_N:(g + 1) * SSD_N]
            Bg = Bm[:, g * SSD_N:(g + 1) * SSD_N]
            cb = _dot_nt(Cg, Bg)
            st = stssd[g]
            gcols = slice(g * gw, (g + 1) * gw)
            y_off = _dot(Cg, st.astype(BF16)) * e_cs[:, gcols]
            stssd[g] = st * e_last[:, gcols] + _dot_tn(Bg, xdec[:, gcols])
            pair_out = []
            for pr in range(hpg // 2):
                h0 = g * hpg + 2 * pr
                ms = []
                for hh in (h0, h0 + 1):
                    diff = cs[:, hh:hh + 1] - csT[hh:hh + 1, :]
                    ms.append((cb * (jnp.exp(jnp.minimum(diff, 0.0)) * tril)).astype(BF16))
                both = _dot(jnp.concatenate(ms, axis=0), xdt[:, h0 * SSD_P:(h0 + 2) * SSD_P])
                pair_out.append(jnp.where(first_of_pair, both[0:CH], both[CH:2 * CH]))
            y_groups.append(y_off + jnp.concatenate(pair_out, axis=1))
        y = jnp.concatenate(y_groups, axis=1) + row(dx_ref) * xs
        yc = _rms(y * _silu(z_s[rows, :]), row(snorm_ref))
        mix_s[rows, D_A + D_HG:] = yc.astype(BF16)
        return carry

    lax.fori_loop(0, n_chunks, chunk, 0, unroll=True)

    x1 = x + _rms(_dot(mix_s[...], wout_ref[...]), row(gpost_ref))
    hx = _rms(x1, row(gprex_ref)).astype(BF16)
    q = _dot(hx, wq_ref[...])
    att = _cross_attention(q, mk_ref[0], mv_ref[0])
    y_ref[0] = x1 + _rms(_dot(att.astype(BF16), wo_ref[...]), row(gpostx_ref))

    @pl.when(ti == pl.num_programs(1) - 1)
    def _():
        for hd in range(HG_HEADS):
            hg_ref[0, hd] = sthg[hd].T
        hpg = SSD_HEADS // SSD_GROUPS
        for g in range(SSD_GROUPS):
            sg = stssd[g].T
            for hh in range(hpg):
                ss_ref[0, g * hpg + hh] = sg[hh * SSD_P:(hh + 1) * SSD_P, :]


def _prompt_layer(x, mk, mv, wts, layer, depth, prev, T):
    b, L, _ = x.shape
    prev = [] if prev is None else list(prev)
    c = _consts()
    n_chunks = T // CH
    const_names = ("masks", "tril", "tril3", "triu3", "expand3")
    consts = [c[k] for k in const_names]
    small = [wts[k] for k in ("caw", "lb", "gn", "scw", "scb", "dtb", "dtbc", "al", "alc", "dx", "snorm",
                              "gpre", "gpost", "gprex", "gpostx")]
    big = [wts[k] for k in ("winT", "wout", "wq", "wo")]

    full = lambda a: pl.BlockSpec(a.shape, lambda bi, ti, _n=a.ndim: (0,) * _n, pipeline_mode=pl.Buffered(1))
    big_spec = lambda a: _layer_spec(a, layer, pipeline_mode=pl.Buffered(1))

    in_specs = ([pl.BlockSpec((1, T, D_MODEL), lambda bi, ti: (bi, ti, 0)),
                 pl.BlockSpec((1, N_MEM, D_MODEL), lambda bi, ti: (bi, 0, 0)),
                 pl.BlockSpec((1, N_MEM, D_MODEL), lambda bi, ti: (bi, 0, 0))]
                + [big_spec(a) for a in big] + [full(a) for a in small] + [full(a) for a in consts]
                + [_ANY] * len(prev))
    n_in = len(in_specs)
    state_shapes = [(CONV_A_W - 1, D_A), (HG_HEADS, HG_DK, HG_DK), (SSD_CONV_W - 1, SSD_CONV_DIM),
                    (SSD_HEADS, SSD_P, SSD_N)]
    out_shape = ([jax.ShapeDtypeStruct((b, L, D_MODEL), F32)]
                 + [jax.ShapeDtypeStruct((depth, b) + s, F32) for s in state_shapes])
    out_specs = ([pl.BlockSpec((1, T, D_MODEL), lambda bi, ti: (bi, ti, 0))]
                 + [pl.BlockSpec((None, 1) + s, lambda bi, ti, _n=len(s): (layer, bi) + (0,) * _n)
                    for s in state_shapes])
    aliases = {n_in - len(prev) + k: 1 + k for k in range(len(prev))}
    scratch = [pltpu.VMEM((8 + T, D_A), F32), pltpu.VMEM((8 + T, SSD_CONV_DIM), F32),
               pltpu.VMEM((T, 4 * D_HG), F32), pltpu.VMEM((T, D_SSD), F32), pltpu.VMEM((T, SSD_CONV_DIM), F32),
               pltpu.VMEM((T, SSD_HEADS), F32), pltpu.VMEM((n_chunks, SSD_HEADS, CH), F32),
               pltpu.VMEM((T, 2 * D_MODEL), BF16),
               pltpu.VMEM((HG_HEADS, HG_DK, HG_DK), F32),
               pltpu.VMEM((SSD_GROUPS, SSD_N, (SSD_HEADS // SSD_GROUPS) * SSD_P), F32)]
    return pl.pallas_call(
        functools.partial(_prompt_kernel, T=T, layer=layer, n_prev=len(prev)),
        grid=(b, L // T),
        in_specs=in_specs, out_specs=out_specs, out_shape=out_shape, scratch_shapes=scratch,
        input_output_aliases=aliases,
        compiler_params=pltpu.CompilerParams(dimension_semantics=("arbitrary", "arbitrary"),
                                             vmem_limit_bytes=VMEM_LIMIT),
        name=f"prompt_layer{layer}",
    )(x, mk, mv, *big, *small, *consts, *prev)


def _prep_weights(w_in, conv_a_w, hgrn_lb, hgrn_gnorm, ssd_conv_w, ssd_conv_b, ssd_dt_bias, ssd_A_log, ssd_D, ssd_norm,
                  w_out, g_pre_mix, g_post_mix, g_pre_x, g_post_x, g_mem, w_q, w_k, w_v, w_o):
    return dict(
        winT=jnp.transpose(w_in, (0, 2, 1)).astype(BF16),
        wout=w_out.astype(BF16), wq=w_q.astype(BF16), wo=w_o.astype(BF16), wk=w_k.astype(BF16), wv=w_v.astype(BF16),
        caw=conv_a_w, lb=hgrn_lb, gn=hgrn_gnorm, scw=ssd_conv_w, scb=ssd_conv_b,
        dtb=ssd_dt_bias, dtbc=ssd_dt_bias.T, al=ssd_A_log, alc=ssd_A_log.T,
        dx=jnp.repeat(ssd_D, SSD_P, axis=1), snorm=ssd_norm,
        gpre=g_pre_mix, gpost=g_post_mix, gprex=g_pre_x, gpostx=g_post_x, gmem=g_mem)


SB = 8
D_HGP = 4 * D_HG
D_SSP = 4 * D_SSD + 2 * SSD_GROUPS * SSD_N


def _sample_pre_kernel(x_ref, ca_ref, sc_ref, winT_ref, caw_ref, lb_ref, scw_ref, scb_ref, dtb_ref, al_ref,
                       gpre_ref, expand_ref,
                       ya_ref, canew_ref, hgp_ref, ssp_ref, scnew_ref, *, layer):
    row = lambda ref: ref[layer:layer + 1, :]
    h = _rms(x_ref[...], row(gpre_ref)).astype(BF16)
    u = _dot_nt(h, winT_ref[...])
    a_h, a_b, a_c, a_z = (u[:, OFF_A + k * D_A:OFF_A + (k + 1) * D_A] for k in range(4))
    va = a_c * a_h
    p0, p1 = ca_ref[:, 0:D_A], ca_ref[:, D_A:2 * D_A]
    caw = caw_ref[layer]
    conv = va * caw[2:3, :] + p1 * caw[1:2, :] + p0 * caw[0:1, :]
    ya_ref[...] = a_b * conv * _silu(a_z)
    canew_ref[:, 0:D_A] = p1
    canew_ref[:, D_A:2 * D_A] = va
    lb = _hgrn_lower_bound(lb_ref[...], layer)
    gq, gf, gi, gz = (u[:, OFF_G + k * D_HG:OFF_G + (k + 1) * D_HG] for k in range(4))
    hgp_ref[:, 0:D_HG] = gq
    hgp_ref[:, D_HG:2 * D_HG] = lb + (1.0 - lb) * _sigmoid(gf)
    hgp_ref[:, 2 * D_HG:3 * D_HG] = gi
    hgp_ref[:, 3 * D_HG:] = gz
    sxbc = u[:, OFF_XBC:OFF_XBC + SSD_CONV_DIM]
    W = SSD_CONV_DIM
    q0, q1, q2 = sc_ref[:, 0:W], sc_ref[:, W:2 * W], sc_ref[:, 2 * W:3 * W]
    scw = scw_ref[layer]
    xbc = _silu(sxbc * scw[3:4, :] + q2 * scw[2:3, :] + q1 * scw[1:2, :] + q0 * scw[0:1, :] + row(scb_ref))
    scnew_ref[:, 0:W] = q1
    scnew_ref[:, W:2 * W] = q2
    scnew_ref[:, 2 * W:3 * W] = sxbc
    xs = xbc[:, 0:D_SSD]
    dt = _softplus(u[:, OFF_DT:OFF_DT + SSD_HEADS] + row(dtb_ref))
    dec = jnp.exp(dt * -jnp.exp(row(al_ref)))
    n = dt.shape[0]
    e_all = _dot(_split3_cols(jnp.concatenate([dt, dec], axis=0)), expand_ref[...])
    ssp_ref[:, 0:D_SSD] = xs
    ssp_ref[:, D_SSD:2 * D_SSD] = xs * e_all[0:n]
    ssp_ref[:, 2 * D_SSD:3 * D_SSD] = e_all[n:2 * n]
    ssp_ref[:, 3 * D_SSD:4 * D_SSD] = u[:, OFF_SZ:OFF_SZ + D_SSD]
    ssp_ref[:, 4 * D_SSD:] = xbc[:, D_SSD:]


def _sample_pre(x, ca, sc, wts, layer):
    n = x.shape[0]
    args = [x, ca, sc, wts["winT"], wts["caw"], wts["lb"], wts["scw"], wts["scb"], wts["dtb"], wts["al"],
            wts["gpre"], _consts()["expand3"]]
    out_shape = [jax.ShapeDtypeStruct((n, D_A), F32), jax.ShapeDtypeStruct((n, 2 * D_A), F32),
                 jax.ShapeDtypeStruct((n, D_HGP), F32), jax.ShapeDtypeStruct((n, D_SSP), F32),
                 jax.ShapeDtypeStruct((n, 3 * SSD_CONV_DIM), F32)]
    return pl.pallas_call(
        functools.partial(_sample_pre_kernel, layer=layer),
        in_specs=[_full_spec(a) for a in args[:3]] + [_layer_spec(args[3], layer)] + [_full_spec(a) for a in args[4:]],
        out_specs=[_full_spec(s) for s in out_shape],
        out_shape=out_shape, grid=(1,),
        compiler_params=pltpu.CompilerParams(dimension_semantics=("arbitrary",), vmem_limit_bytes=VMEM_LIMIT),
        name=f"sample_pre{layer}",
    )(*args)


def _pad_rows_T(blk):
    w = blk.shape[1]
    return jnp.concatenate([blk, jnp.zeros((128 - blk.shape[0], w), blk.dtype)], axis=0).T


def _sample_state_kernel(hgp_ref, ssp_ref, shg_ref, sss_ref, *rest):
    o_ref, y_ref, shg_out, sss_out = rest[-4:]
    rid_hg = lax.broadcasted_iota(jnp.int32, (SB, HG_DK), 0)
    for hd in range(HG_HEADS):
        cols = slice(hd * HG_DK, (hd + 1) * HG_DK)
        q_b = hgp_ref[:, cols].astype(BF16)
        fT = _pad_rows_T(hgp_ref[:, D_HG + hd * HG_DK:D_HG + (hd + 1) * HG_DK])
        o = jnp.zeros((SB, HG_DK), F32)
        for j in range(SB):
            fcol = fT[:, j:j + 1]
            vrow = hgp_ref[j:j + 1, 2 * D_HG + hd * HG_DK:2 * D_HG + (hd + 1) * HG_DK]
            s_new = fcol * shg_ref[j, hd] + (1.0 - fcol) * vrow
            shg_out[j, hd] = s_new
            o = jnp.where(rid_hg == j, _dot(q_b, s_new.astype(BF16)), o)
        o_ref[:, cols] = o
    gw = (SSD_HEADS // SSD_GROUPS) * SSD_P
    rid_ss = lax.broadcasted_iota(jnp.int32, (SB, gw), 0)
    xdtT = _pad_rows_T(ssp_ref[:, D_SSD:2 * D_SSD])
    decT = _pad_rows_T(ssp_ref[:, 2 * D_SSD:3 * D_SSD])
    for g in range(SSD_GROUPS):
        rows = slice(g * gw, (g + 1) * gw)
        c_b = ssp_ref[:, 4 * D_SSD + (SSD_GROUPS + g) * SSD_N:4 * D_SSD + (SSD_GROUPS + g + 1) * SSD_N].astype(BF16)
        y = jnp.zeros((SB, gw), F32)
        for j in range(SB):
            brow = ssp_ref[j:j + 1, 4 * D_SSD + g * SSD_N:4 * D_SSD + (g + 1) * SSD_N]
            s_new = decT[rows, j:j + 1] * sss_ref[j, rows, :] + xdtT[rows, j:j + 1] * brow
            sss_out[j, rows, :] = s_new
            y = jnp.where(rid_ss == j, _dot_nt(c_b, s_new.astype(BF16)), y)
        y_ref[:, rows] = y


def _sample_state(hgp, ssp, shg_all, sss_all, layer, prev):
    n = hgp.shape[0]
    prev = [] if prev is None else list(prev)
    rowblk = lambda w: pl.BlockSpec((SB, w), lambda i: (i, 0))
    hg_blk = pl.BlockSpec((None, SB, HG_HEADS, HG_DK, HG_DK), lambda i: (layer, i, 0, 0, 0))
    ss_blk = pl.BlockSpec((None, SB, SSD_HEADS * SSD_P, SSD_N), lambda i: (layer, i, 0, 0))
    return pl.pallas_call(
        _sample_state_kernel,
        grid=(n // SB,),
        in_specs=[rowblk(D_HGP), rowblk(D_SSP), hg_blk, ss_blk] + [_ANY] * len(prev),
        out_specs=[rowblk(D_HG), rowblk(D_SSD), hg_blk, ss_blk],
        out_shape=[jax.ShapeDtypeStruct((n, D_HG), F32), jax.ShapeDtypeStruct((n, D_SSD), F32),
                   jax.ShapeDtypeStruct(shg_all.shape, F32), jax.ShapeDtypeStruct(sss_all.shape, F32)],
        input_output_aliases={4 + k: 2 + k for k in range(len(prev))},
        compiler_params=pltpu.CompilerParams(dimension_semantics=("arbitrary",), vmem_limit_bytes=VMEM_LIMIT),
        name="sample_state",
    )(hgp, ssp, shg_all, sss_all, *prev)


def _sample_mid_kernel(x_ref, ya_ref, o_ref, y_ref, hgp_ref, ssp_ref, wout_ref, wq_ref, gn_ref, dx_ref, snorm_ref,
                       gpost_ref, gprex_ref, x1_ref, q_ref, *, layer):
    row = lambda ref: ref[layer:layer + 1, :]
    gz = hgp_ref[:, 3 * D_HG:]
    o = o_ref[...]
    yb = jnp.concatenate([_rms(o[:, hd * HG_DK:(hd + 1) * HG_DK], gn_ref[layer:layer + 1, hd * HG_DK:(hd + 1) * HG_DK])
                          for hd in range(HG_HEADS)], axis=1) * _silu(gz)
    y = y_ref[...] + row(dx_ref) * ssp_ref[:, 0:D_SSD]
    yc = _rms(y * _silu(ssp_ref[:, 3 * D_SSD:4 * D_SSD]), row(snorm_ref))
    mix = jnp.concatenate([ya_ref[...], yb, yc], axis=1).astype(BF16)
    x1 = x_ref[...] + _rms(_dot(mix, wout_ref[...]), row(gpost_ref))
    x1_ref[...] = x1
    q = _dot(_rms(x1, row(gprex_ref)).astype(BF16), wq_ref[...])
    for hd in range(XA_HEADS):
        for k in range(XA_HD // 128):
            q_ref[:, k * XA_HEADS + hd, :] = q[:, hd * XA_HD + k * 128:hd * XA_HD + (k + 1) * 128]


def _sample_mid(x, ya, o, y, hgp, ssp, wts, layer):
    n = x.shape[0]
    args = [x, ya, o, y, hgp, ssp, wts["wout"], wts["wq"], wts["gn"], wts["dx"], wts["snorm"], wts["gpost"],
            wts["gprex"]]
    out_shape = [jax.ShapeDtypeStruct((n, D_MODEL), F32),
                 jax.ShapeDtypeStruct((n, XA_HEADS * (XA_HD // 128), 128), F32)]
    return pl.pallas_call(
        functools.partial(_sample_mid_kernel, layer=layer), grid=(1,),
        in_specs=[_full_spec(a) for a in args[:6]] + [_layer_spec(a, layer) for a in args[6:8]]
        + [_full_spec(a) for a in args[8:]], out_specs=[_full_spec(s) for s in out_shape], out_shape=out_shape,
        compiler_params=pltpu.CompilerParams(dimension_semantics=("arbitrary",), vmem_limit_bytes=VMEM_LIMIT),
        name="sample_mid",
    )(*args)


def _lane_class_reduce(x, op):
    sh = KV_SUB
    while sh < 128:
        x = op(x, pltpu.roll(x, sh, axis=1))
        sh *= 2
    return x


def _sample_attn_kernel(x1_ref, q_ref, k_ref, v_ref, wo_ref, gpostx_ref, x2_ref, *, layer):
    lane = lax.broadcasted_iota(jnp.int32, (KV_SUB, KV_ROWS), 1)
    sub = lax.broadcasted_iota(jnp.int32, (KV_SUB, KV_ROWS), 0)
    own = ((lane & (KV_SUB - 1)) == sub).astype(F32)
    rid = lax.broadcasted_iota(jnp.int32, (SB, KV_ROWS), 0)
    t_all = jnp.zeros((SB, KV_ROWS), F32)
    for j in range(SB):
        r = _dot_nt(q_ref[j].astype(BF16), k_ref[j].astype(BF16))
        t = jnp.sum(r * own, axis=0, keepdims=True)
        t_all = jnp.where(rid == j, t, t_all)
    n_tiles = KV_ROWS // 128
    lane1 = lax.broadcasted_iota(jnp.int32, (SB, 128), 1)
    piece = (lane1 // XA_HEADS) % KV_SPLIT
    chunks = []
    for c in range(n_tiles):
        x = t_all[:, c * 128:(c + 1) * 128]
        tot = x
        for k in range(1, KV_SPLIT):
            fwd = pltpu.roll(x, 128 - k * XA_HEADS, axis=1)
            bwd = pltpu.roll(x, (KV_SPLIT - k) * XA_HEADS, axis=1)
            tot = tot + jnp.where(piece + k < KV_SPLIT, fwd, bwd)
        chunks.append(tot * (XA_HD ** -0.5))
    mx = _lane_class_reduce(functools.reduce(jnp.maximum, chunks), jnp.maximum)
    es = [jnp.exp(ch - mx) for ch in chunks]
    den = _lane_class_reduce(functools.reduce(lambda a, b: a + b, es), lambda a, b: a + b)
    p_all = jnp.concatenate([e * (1.0 / den) for e in es], axis=1)
    rid_o = lax.broadcasted_iota(jnp.int32, (SB, D_MODEL), 0)
    att = jnp.zeros((SB, D_MODEL), F32)
    for j in range(SB):
        p8 = (own * p_all[j:j + 1, :]).astype(BF16)
        o = _dot(p8, v_ref[j].astype(BF16))
        row = jnp.concatenate([o[k * XA_HEADS + hd:k * XA_HEADS + hd + 1, :]
                               for hd in range(XA_HEADS) for k in range(KV_SPLIT)], axis=1)
        att = jnp.where(rid_o == j, row, att)
    x2_ref[...] = x1_ref[...] + _rms(_dot(att.astype(BF16), wo_ref[...]), gpostx_ref[layer:layer + 1, :])


def _sample_attn(x1, q8, ck_rows, cv_rows, wts, layer):
    n = x1.shape[0]
    rowblk = pl.BlockSpec((SB, D_MODEL), lambda i: (i, 0))
    qblk = pl.BlockSpec((SB, KV_SUB, 128), lambda i: (i, 0, 0))
    kvblk = pl.BlockSpec((None, SB, KV_ROWS, 128), lambda i: (layer, i, 0, 0))
    return pl.pallas_call(
        functools.partial(_sample_attn_kernel, layer=layer),
        grid=(n // SB,),
        in_specs=[rowblk, qblk, kvblk, kvblk, _layer_spec(wts["wo"], layer), _full_spec(wts["gpostx"])],
        out_specs=rowblk,
        out_shape=jax.ShapeDtypeStruct((n, D_MODEL), F32),
        compiler_params=pltpu.CompilerParams(dimension_semantics=("arbitrary",), vmem_limit_bytes=VMEM_LIMIT),
        name="sample_attn",
    )(x1, q8, ck_rows, cv_rows, wts["wo"], wts["gpostx"])


def _sample_layer(x, ca, shg_all, sc, sss_all, ck_rows, cv_rows, wts, layer, prev_states):
    n = x.shape[0]
    ya, ca_new, hgp, ssp, sc_new = _sample_pre(x, ca.reshape(n, -1), sc.reshape(n, -1), wts, layer)
    o, y, shg_new, sss_new = _sample_state(hgp, ssp, shg_all, sss_all, layer, prev_states)
    x1, q8 = _sample_mid(x, ya, o, y, hgp, ssp, wts, layer)
    x2 = _sample_attn(x1, q8, ck_rows, cv_rows, wts, layer)
    return x2, ca_new.reshape(ca.shape), sc_new.reshape(sc.shape), (shg_new, sss_new)


PROMPT_TILE = 512


def kernel(x_prompt, x_sample, mem_prompt, state_conv_a, state_hgrn, state_ssd_conv, state_ssd, cache_mem_k,
           cache_mem_v, w_in, conv_a_w, hgrn_lb, hgrn_gnorm, ssd_conv_w, ssd_conv_b, ssd_dt_bias, ssd_A_log, ssd_D,
           ssd_norm, w_out, g_pre_mix, g_post_mix, g_pre_x, g_post_x, g_mem, w_q, w_k, w_v, w_o):
    depth = w_in.shape[0]
    n = x_sample.shape[0]
    yp = x_prompt
    ys = x_sample.reshape(n, D_MODEL)
    ck_rows, cv_rows = _kv_rows_view(cache_mem_k), _kv_rows_view(cache_mem_v)
    sss_all = state_ssd.reshape(depth, n, SSD_HEADS * SSD_P, SSD_N)
    kv_rows = p_states = s_states = None
    s_ca, s_sc = [], []
    wts = _prep_weights(w_in, conv_a_w, hgrn_lb, hgrn_gnorm, ssd_conv_w, ssd_conv_b, ssd_dt_bias, ssd_A_log, ssd_D,
                        ssd_norm, w_out, g_pre_mix, g_post_mix, g_pre_x, g_post_x, g_mem, w_q, w_k, w_v, w_o)
    for l in range(depth):
        *kv_rows, mk, mv = _memory_kv(mem_prompt, wts, l, depth, kv_rows)
        yp, *p_states = _prompt_layer(yp, mk, mv, wts, l, depth, p_states, PROMPT_TILE)
        ys, ca, sc, s_states = _sample_layer(ys, state_conv_a[l], state_hgrn, state_ssd_conv[l], sss_all,
                                             ck_rows, cv_rows, wts, l, s_states)
        s_ca.append(ca)
        s_sc.append(sc)
    p_ca, p_hg, p_sc, p_ss = p_states
    s_hg, s_ss = s_states
    return (yp, ys.reshape(x_sample.shape), p_ca, p_hg, p_sc, p_ss, _kv_from_rows(kv_rows[0]),
            _kv_from_rows(kv_rows[1]), jnp.stack(s_ca), s_hg, jnp.stack(s_sc), s_ss.reshape(state_ssd.shape))
```

```python
import functools

import numpy as np
import jax
import jax.numpy as jnp
from jax import lax
from jax.experimental import pallas as pl
from jax.experimental.pallas import tpu as pltpu

F32 = jnp.float32
BF16 = jnp.bfloat16

D_MODEL = 1024
D_A = 512
CONV_A_W = 3
D_HG = 512
HG_HEADS = 4
HG_DK = 128
D_SSD = 1024
SSD_P = 64
SSD_HEADS = 16
SSD_GROUPS = 2
SSD_N = 128
SSD_CONV_W = 4
SSD_CONV_DIM = D_SSD + 2 * SSD_GROUPS * SSD_N
N_MEM = 256
XA_HEADS = 4
XA_HD = 256
EPS = 1e-6
KV_SPLIT = XA_HD // 128
KV_SUB = XA_HEADS * KV_SPLIT
KV_ROWS = N_MEM * KV_SUB

OFF_A = 0
OFF_G = 2048
OFF_SZ = 4096
OFF_XBC = 5120
OFF_DT = 6656
D_IN = 6672

CH = 128
HG_LEVELS = (1, 2, 4, 8, 16, 32, 64)
VMEM_LIMIT = 56 * 1024 * 1024


def _rms(x, g):
    ms = jnp.mean(x * x, axis=-1, keepdims=True)
    return x * lax.rsqrt(ms + EPS) * g


def _silu(x):
    return x * (1.0 / (1.0 + jnp.exp(-x)))


def _sigmoid(x):
    return 1.0 / (1.0 + jnp.exp(-x))


def _softplus(x):
    return jnp.maximum(x, 0.0) + jnp.log(1.0 + jnp.exp(-jnp.abs(x)))


def _dot(a, b):
    return jnp.dot(a, b, preferred_element_type=F32)


def _dot_nt(a, b):
    return lax.dot_general(a, b, (((1,), (1,)), ((), ())), preferred_element_type=F32)


def _dot_tn(a, b):
    return lax.dot_general(a, b, (((0,), (0,)), ((), ())), preferred_element_type=F32)


def _split3(x):
    hi = x.astype(BF16)
    r = x - hi.astype(F32)
    mid = r.astype(BF16)
    lo = (r - mid.astype(F32)).astype(BF16)
    return hi, mid, lo


def _split3_rows(x):
    return jnp.concatenate(_split3(x), axis=0)


def _split3_cols(x):
    return jnp.concatenate(_split3(x), axis=1)


@functools.lru_cache(maxsize=None)
def _consts():
    r = np.arange(CH)
    i, t = r[:, None], r[None, :]
    masks = [np.eye(CH, dtype=bool)]
    for s in HG_LEVELS:
        up = ((r // s) % 2 == 1)
        same = (i // (2 * s)) == (t // (2 * s))
        masks.append(same & up[:, None] & (~up)[None, :])
    masks = np.stack(masks).astype(np.float32)
    tril = (t <= i).astype(np.float32)
    tril3 = np.tile(tril, (1, 3))
    triu3 = np.tile(tril.T, (3, 1))
    e = (np.arange(D_SSD)[None, :] // SSD_P == np.arange(SSD_HEADS)[:, None]).astype(np.float32)
    expand3 = np.tile(e, (3, 1))
    return dict(
        masks=jnp.asarray(masks, F32),
        tril=jnp.asarray(tril, F32), tril3=jnp.asarray(tril3, BF16), triu3=jnp.asarray(triu3, BF16),
        expand3=jnp.asarray(expand3, BF16))


def _kv_rows_view(c_all):
    depth, n = c_all.shape[:2]
    c = c_all.reshape(depth, n, N_MEM, XA_HEADS, KV_SPLIT, 128)
    return jnp.transpose(c, (0, 1, 2, 4, 3, 5)).reshape(depth, n, KV_ROWS, 128)


def _kv_from_rows(r_all):
    depth, n = r_all.shape[:2]
    c = r_all.reshape(depth, n, N_MEM, KV_SPLIT, XA_HEADS, 128)
    return jnp.transpose(c, (0, 1, 2, 4, 3, 5)).reshape(depth, n, N_MEM, XA_HEADS, XA_HD)


def _store_kv_rows(r_ref, x):
    for hd in range(XA_HEADS):
        for k in range(KV_SPLIT):
            r_ref[:, k * XA_HEADS + hd, :] = x[:, hd * XA_HD + k * 128:hd * XA_HD + (k + 1) * 128]


_ANY = pl.BlockSpec(memory_space=pl.ANY)


def _full_spec(a):
    nd = a.ndim
    return pl.BlockSpec(a.shape, lambda *_, _n=nd: (0,) * _n)


def _layer_spec(a, layer, **kw):
    nd = a.ndim
    return pl.BlockSpec((None,) + tuple(a.shape[1:]), lambda *_, _n=nd: (layer,) + (0,) * (_n - 1), **kw)


def _memkv_kernel(mem_ref, g_ref, wk_ref, wv_ref, *refs, layer):
    kr_ref, vr_ref, kb_ref, vb_ref = refs[-4:]
    m = _rms(mem_ref[0], g_ref[layer:layer + 1, :]).astype(BF16)
    for w_ref, r_ref, b_ref in ((wk_ref, kr_ref, kb_ref), (wv_ref, vr_ref, vb_ref)):
        kv = _dot(m, w_ref[...])
        b_ref[0] = kv.astype(BF16)
        _store_kv_rows(r_ref, kv)


def _memory_kv(mem, wts, layer, depth, prev):
    b = mem.shape[0]
    blk = pl.BlockSpec((1, N_MEM, D_MODEL), lambda i: (i, 0, 0))
    rows_blk = pl.BlockSpec((None, None, N_MEM, KV_SUB, 128), lambda i: (layer, i, 0, 0, 0))
    rows_sds = jax.ShapeDtypeStruct((depth, b, N_MEM, KV_SUB, 128), F32)
    extra, extra_specs, aliases = [], [], {}
    if prev is not None:
        extra, extra_specs, aliases = list(prev), [_ANY, _ANY], {4: 0, 5: 1}
    return pl.pallas_call(
        functools.partial(_memkv_kernel, layer=layer),
        grid=(b,),
        in_specs=[blk, _full_spec(wts["gmem"]), _layer_spec(wts["wk"], layer), _layer_spec(wts["wv"], layer)]
        + extra_specs,
        out_specs=[rows_blk, rows_blk, blk, blk],
        out_shape=[rows_sds, rows_sds] + [jax.ShapeDtypeStruct((b, N_MEM, D_MODEL), BF16)] * 2,
        input_output_aliases=aliases,
        compiler_params=pltpu.CompilerParams(dimension_semantics=("arbitrary",), vmem_limit_bytes=VMEM_LIMIT),
        name="memory_kv",
    )(mem, wts["gmem"], wts["wk"], wts["wv"], *extra)


def _hgrn_lower_bound(lb_all, layer):
    depth = lb_all.shape[0]
    rows = [lb_all[j:j + 1, :] for j in range(depth)]
    mx = functools.reduce(jnp.maximum, rows)
    ex = [jnp.exp(rw - mx) for rw in rows]
    tot = functools.reduce(lambda a, b: a + b, ex)
    acc = jnp.zeros_like(tot)
    for j in range(1, layer + 1):
        acc = acc + ex[j]
    return acc / tot


def _hgrn_level(c, f, q, k, s):
    n, w = c.shape
    if s >= 8:
        nb = n // (2 * s)
        c4, q4, k4 = (a.reshape(nb, 2, s, w) for a in (c, q, k))
        lower, upper = c4[:, 0], c4[:, 1]
        tot = lower[:, s - 1:s, :]
        w_lower = k4[:, 0] * jnp.exp(tot - lower)
        w_upper = q4[:, 1] * jnp.exp(upper)
        wv = jnp.stack([w_lower, w_upper], axis=1).reshape(n, w)
        c_next = jnp.stack([lower, upper + tot], axis=1).reshape(n, w)
        return wv, c_next
    sub = lax.broadcasted_iota(jnp.int32, (1, 8, w), 1)
    c3, f3, q3, k3 = (a.reshape(n // 8, 8, w) for a in (c, f, q, k))
    up = (sub // s) % 2 == 1
    tot = None
    for gi in reversed(range(8 // (2 * s))):
        r = gi * 2 * s + s - 1
        tg = jnp.broadcast_to(c3[:, r:r + 1, :], c3.shape)
        tot = tg if tot is None else jnp.where(sub < (gi + 1) * 2 * s, tg, tot)
    if s == 1:
        e = jnp.where(up, f3, 1.0)
    else:
        e = jnp.exp(jnp.where(up, c3, tot - c3))
    wv = jnp.where(up, q3, k3) * e
    c_next = c3 + jnp.where(up, tot, 0.0)
    return wv.reshape(n, w), c_next.reshape(n, w)


def _causal_conv(x, taps, prev_ref):
    n_taps = len(taps)
    row0 = lax.broadcasted_iota(jnp.int32, (8, x.shape[1]), 0) == 0
    prev = [prev_ref[8 - d:8 - d + 1, :] for d in range(1, n_taps)]
    acc = x * taps[0]
    for k in range(1, n_taps):
        carry = functools.reduce(lambda a, b: a + b, [taps[j] * prev[k - j - 1] for j in range(k)])
        rolled = pltpu.roll(acc, 1, axis=0)
        shifted = jnp.concatenate([jnp.where(row0, carry, rolled[0:8]), rolled[8:]], axis=0)
        acc = x * taps[k] + shifted
    return acc


def _cross_attention(q, mk, mv):
    outs = []
    for hd in range(XA_HEADS):
        sl = slice(hd * XA_HD, (hd + 1) * XA_HD)
        s = _dot_nt(q[:, sl].astype(BF16), mk[:, sl]) * (XA_HD ** -0.5)
        s = s - jnp.max(s, axis=-1, keepdims=True)
        e = jnp.exp(s)
        p = e * (1.0 / jnp.sum(e, axis=-1, keepdims=True))
        outs.append(_dot(p.astype(BF16), mv[:, sl]))
    return jnp.concatenate(outs, axis=1)


def _prompt_kernel(x_ref, mk_ref, mv_ref, winT_ref, wout_ref, wq_ref, wo_ref,
                   caw_ref, lb_ref, gn_ref, scw_ref, scb_ref, dtb_ref, dtbc_ref, al_ref, alc_ref, dx_ref,
                   snorm_ref, gpre_ref, gpost_ref, gprex_ref, gpostx_ref,
                   masks_ref, tril_ref, tril3_ref, triu3_ref, expand_ref,
                   *rest, T, layer, n_prev):
    (y_ref, ca_ref, hg_ref, sc_ref, ss_ref,
     bufa, bufc, ug_s, z_s, xbc_s, dt_s, dtT_s, mix_s, sthg, stssd) = rest[n_prev:]
    ti = pl.program_id(1)
    n_chunks = T // CH

    @pl.when(ti == 0)
    def _():
        bufa[0:8, :] = jnp.zeros((8, D_A), F32)
        bufc[0:8, :] = jnp.zeros((8, SSD_CONV_DIM), F32)
        sthg[...] = jnp.zeros(sthg.shape, F32)
        stssd[...] = jnp.zeros(stssd.shape, F32)

    x = x_ref[0]
    row = lambda ref: ref[layer:layer + 1, :]
    h = _rms(x, row(gpre_ref)).astype(BF16)

    sxbc = _dot_nt(h, winT_ref[OFF_XBC:OFF_XBC + SSD_CONV_DIM, :])
    scw = scw_ref[layer]
    xbc = _causal_conv(sxbc, [scw[k:k + 1, :] for k in range(SSD_CONV_W)], bufc) + row(scb_ref)
    xbc_s[...] = _silu(xbc)
    sc_ref[0] = sxbc[T - 3:T, :]
    bufc[0:8, :] = sxbc[T - 8:T, :]
    wdtT = winT_ref[OFF_DT:OFF_DT + SSD_HEADS, :]
    sdt = _dot_nt(h, wdtT)
    dt_s[...] = _softplus(sdt + row(dtb_ref))
    dtT = _softplus(_dot_nt(wdtT, h) + dtbc_ref[:, layer:layer + 1])
    for c in range(n_chunks):
        dtT_s[c] = dtT[:, c * CH:(c + 1) * CH]
    z_s[...] = _dot_nt(h, winT_ref[OFF_SZ:OFF_SZ + D_SSD, :])

    ua = _dot_nt(h, winT_ref[OFF_A:OFF_A + 4 * D_A, :])
    a_h, a_b, a_c, a_z = (ua[:, k * D_A:(k + 1) * D_A] for k in range(4))
    va = a_c * a_h
    caw = caw_ref[layer]
    conv = _causal_conv(va, [caw[k:k + 1, :] for k in range(CONV_A_W)], bufa)
    mix_s[:, 0:D_A] = (a_b * conv * _silu(a_z)).astype(BF16)
    ca_ref[0] = va[T - 2:T, :]
    bufa[0:8, :] = va[T - 8:T, :]

    ug_s[...] = _dot_nt(h, winT_ref[OFF_G:OFF_G + 4 * D_HG, :])


    lb = _hgrn_lower_bound(lb_ref[...], layer)
    a_row = -jnp.exp(row(al_ref))
    a_col = -jnp.exp(alc_ref[:, layer:layer + 1])
    tril = tril_ref[...]
    first_of_pair = lax.broadcasted_iota(jnp.int32, (1, 2 * SSD_P), 1) < SSD_P

    def chunk(c, carry):
        r0 = pl.multiple_of(c * CH, CH)
        rows = pl.ds(r0, CH)

        ug = ug_s[rows, :]
        gq, gf, gi, gz = (ug[:, k * D_HG:(k + 1) * D_HG] for k in range(4))
        f = lb + (1.0 - lb) * _sigmoid(gf)
        logf = jnp.log(f)
        kk = 1.0 - f
        q_b, k_b, v_b = gq.astype(BF16), kk.astype(BF16), gi.astype(BF16)
        hs = [slice(hd * HG_DK, (hd + 1) * HG_DK) for hd in range(HG_HEADS)]
        A = [masks_ref[0] * _dot_nt(q_b[:, s_], k_b[:, s_]) for s_ in hs]
        G = logf
        for li, s in enumerate(HG_LEVELS):
            w, G = _hgrn_level(G, f, gq, kk, s)
            w = w.astype(BF16)
            m = masks_ref[li + 1]
            A = [A[hd] + m * _dot_nt(w[:, hs[hd]], w[:, hs[hd]]) for hd in range(HG_HEADS)]
        g_last = G[CH - 1:CH, :]
        qg = (gq * jnp.exp(G)).astype(BF16)
        kd = (kk * jnp.exp(g_last - G)).astype(BF16)
        dec = jnp.exp(g_last)
        o_heads = []
        for hd in range(HG_HEADS):
            s_ = hs[hd]
            st = sthg[hd]
            o = _dot_nt(qg[:, s_], st.astype(BF16)) + _dot(A[hd].astype(BF16), v_b[:, s_])
            sthg[hd] = st * dec[:, s_] + _dot_tn(v_b[:, s_], kd[:, s_])
            o_heads.append(_rms(o, gn_ref[layer:layer + 1, s_]))
        yb = jnp.concatenate(o_heads, axis=1) * _silu(gz)
        mix_s[rows, D_A:D_A + D_HG] = yb.astype(BF16)

        xbc_c = xbc_s[rows, :]
        xs = xbc_c[:, 0:D_SSD]
        Bm = xbc_c[:, D_SSD:D_SSD + SSD_GROUPS * SSD_N].astype(BF16)
        Cm = xbc_c[:, D_SSD + SSD_GROUPS * SSD_N:].astype(BF16)
        dt = dt_s[rows, :]
        dtT_c = dtT_s[c]
        cs = _dot(tril3_ref[...], _split3_rows(dt * a_row))
        csT = _dot(_split3_cols(dtT_c * a_col), triu3_ref[...])
        cs_last = cs[CH - 1:CH, :]
        w_all = jnp.concatenate([dt * jnp.exp(cs_last - cs), jnp.exp(cs), dt,
                                 jnp.broadcast_to(jnp.exp(cs_last), (8, SSD_HEADS))], axis=0)
        e_all = _dot(_split3_cols(w_all), expand_ref[...])
        e_dec, e_cs, e_dt, e_last = e_all[0:CH], e_all[CH:2 * CH], e_all[2 * CH:3 * CH], e_all[3 * CH:3 * CH + 1]
        xdt = (xs * e_dt).astype(BF16)
        xdec = (xs * e_dec).astype(BF16)
        y_groups = []
        hpg = SSD_HEADS // SSD_GROUPS
        gw = hpg * SSD_P
        for g in range(SSD_GROUPS):
            Cg = Cm[:, g * SSD_N:(g + 1) * SSD_N]
            Bg = Bm[:, g * SSD_N:(g + 1) * SSD_N]
            cb = _dot_nt(Cg, Bg) * tril
            st = stssd[g]
            gcols = slice(g * gw, (g + 1) * gw)
            y_off = _dot(Cg, st.astype(BF16)) * e_cs[:, gcols]
            stssd[g] = st * e_last[:, gcols] + _dot_tn(Bg, xdec[:, gcols])
            pair_out = []
            for pr in range(hpg // 2):
                h0 = g * hpg + 2 * pr
                ms = []
                for hh in (h0, h0 + 1):
                    diff = cs[:, hh:hh + 1] - csT[hh:hh + 1, :]
                    ms.append((cb * jnp.exp(jnp.minimum(diff, 0.0))).astype(BF16))
                both = _dot(jnp.concatenate(ms, axis=0), xdt[:, h0 * SSD_P:(h0 + 2) * SSD_P])
                pair_out.append(jnp.where(first_of_pair, both[0:CH], both[CH:2 * CH]))
            y_groups.append(y_off + jnp.concatenate(pair_out, axis=1))
        y = jnp.concatenate(y_groups, axis=1) + row(dx_ref) * xs
        yc = _rms(y * _silu(z_s[rows, :]), row(snorm_ref))
        mix_s[rows, D_A + D_HG:] = yc.astype(BF16)
        return carry

    lax.fori_loop(0, n_chunks, chunk, 0, unroll=True)

    x1 = x + _rms(_dot(mix_s[...], wout_ref[...]), row(gpost_ref))
    hx = _rms(x1, row(gprex_ref)).astype(BF16)
    q = _dot(hx, wq_ref[...])
    att = _cross_attention(q, mk_ref[0], mv_ref[0])
    y_ref[0] = x1 + _rms(_dot(att.astype(BF16), wo_ref[...]), row(gpostx_ref))

    @pl.when(ti == pl.num_programs(1) - 1)
    def _():
        for hd in range(HG_HEADS):
            hg_ref[0, hd] = sthg[hd].T
        hpg = SSD_HEADS // SSD_GROUPS
        for g in range(SSD_GROUPS):
            sg = stssd[g].T
            for hh in range(hpg):
                ss_ref[0, g * hpg + hh] = sg[hh * SSD_P:(hh + 1) * SSD_P, :]


def _prompt_layer(x, mk, mv, wts, layer, depth, prev, T):
    b, L, _ = x.shape
    prev = [] if prev is None else list(prev)
    c = _consts()
    n_chunks = T // CH
    const_names = ("masks", "tril", "tril3", "triu3", "expand3")
    consts = [c[k] for k in const_names]
    small = [wts[k] for k in ("caw", "lb", "gn", "scw", "scb", "dtb", "dtbc", "al", "alc", "dx", "snorm",
                              "gpre", "gpost", "gprex", "gpostx")]
    big = [wts[k] for k in ("winT", "wout", "wq", "wo")]

    full = lambda a: pl.BlockSpec(a.shape, lambda bi, ti, _n=a.ndim: (0,) * _n, pipeline_mode=pl.Buffered(1))
    big_spec = lambda a: _layer_spec(a, layer, pipeline_mode=pl.Buffered(1))

    in_specs = ([pl.BlockSpec((1, T, D_MODEL), lambda bi, ti: (bi, ti, 0)),
                 pl.BlockSpec((1, N_MEM, D_MODEL), lambda bi, ti: (bi, 0, 0)),
                 pl.BlockSpec((1, N_MEM, D_MODEL), lambda bi, ti: (bi, 0, 0))]
                + [big_spec(a) for a in big] + [full(a) for a in small] + [full(a) for a in consts]
                + [_ANY] * len(prev))
    n_in = len(in_specs)
    state_shapes = [(CONV_A_W - 1, D_A), (HG_HEADS, HG_DK, HG_DK), (SSD_CONV_W - 1, SSD_CONV_DIM),
                    (SSD_HEADS, SSD_P, SSD_N)]
    out_shape = ([jax.ShapeDtypeStruct((b, L, D_MODEL), F32)]
                 + [jax.ShapeDtypeStruct((depth, b) + s, F32) for s in state_shapes])
    out_specs = ([pl.BlockSpec((1, T, D_MODEL), lambda bi, ti: (bi, ti, 0))]
                 + [pl.BlockSpec((None, 1) + s, lambda bi, ti, _n=len(s): (layer, bi) + (0,) * _n)
                    for s in state_shapes])
    aliases = {n_in - len(prev) + k: 1 + k for k in range(len(prev))}
    scratch = [pltpu.VMEM((8, D_A), F32), pltpu.VMEM((8, SSD_CONV_DIM), F32),
               pltpu.VMEM((T, 4 * D_HG), F32), pltpu.VMEM((T, D_SSD), F32), pltpu.VMEM((T, SSD_CONV_DIM), F32),
               pltpu.VMEM((T, SSD_HEADS), F32), pltpu.VMEM((n_chunks, SSD_HEADS, CH), F32),
               pltpu.VMEM((T, 2 * D_MODEL), BF16),
               pltpu.VMEM((HG_HEADS, HG_DK, HG_DK), F32),
               pltpu.VMEM((SSD_GROUPS, SSD_N, (SSD_HEADS // SSD_GROUPS) * SSD_P), F32)]
    return pl.pallas_call(
        functools.partial(_prompt_kernel, T=T, layer=layer, n_prev=len(prev)),
        grid=(b, L // T),
        in_specs=in_specs, out_specs=out_specs, out_shape=out_shape, scratch_shapes=scratch,
        input_output_aliases=aliases,
        compiler_params=pltpu.CompilerParams(dimension_semantics=("arbitrary", "arbitrary"),
                                             vmem_limit_bytes=VMEM_LIMIT),
        name=f"prompt_layer{layer}",
    )(x, mk, mv, *big, *small, *consts, *prev)


def _prep_weights(w_in, conv_a_w, hgrn_lb, hgrn_gnorm, ssd_conv_w, ssd_conv_b, ssd_dt_bias, ssd_A_log, ssd_D, ssd_norm,
                  w_out, g_pre_mix, g_post_mix, g_pre_x, g_post_x, g_mem, w_q, w_k, w_v, w_o):
    return dict(
        winT=jnp.transpose(w_in, (0, 2, 1)).astype(BF16),
        wout=w_out.astype(BF16), wq=w_q.astype(BF16), wo=w_o.astype(BF16), wk=w_k.astype(BF16), wv=w_v.astype(BF16),
        caw=conv_a_w, lb=hgrn_lb, gn=hgrn_gnorm, scw=ssd_conv_w, scb=ssd_conv_b,
        dtb=ssd_dt_bias, dtbc=ssd_dt_bias.T, al=ssd_A_log, alc=ssd_A_log.T,
        dx=jnp.repeat(ssd_D, SSD_P, axis=1), snorm=ssd_norm,
        gpre=g_pre_mix, gpost=g_post_mix, gprex=g_pre_x, gpostx=g_post_x, gmem=g_mem)


SB = 8
D_HGP = 4 * D_HG
D_SSP = 4 * D_SSD + 2 * SSD_GROUPS * SSD_N


def _sample_pre_kernel(x_ref, ca_ref, sc_ref, winT_ref, caw_ref, lb_ref, scw_ref, scb_ref, dtb_ref, al_ref,
                       gpre_ref, expand_ref,
                       ya_ref, canew_ref, hgp_ref, ssp_ref, scnew_ref, *, layer):
    row = lambda ref: ref[layer:layer + 1, :]
    h = _rms(x_ref[...], row(gpre_ref)).astype(BF16)
    u = _dot_nt(h, winT_ref[...])
    a_h, a_b, a_c, a_z = (u[:, OFF_A + k * D_A:OFF_A + (k + 1) * D_A] for k in range(4))
    va = a_c * a_h
    p0, p1 = ca_ref[:, 0:D_A], ca_ref[:, D_A:2 * D_A]
    caw = caw_ref[layer]
    conv = va * caw[2:3, :] + p1 * caw[1:2, :] + p0 * caw[0:1, :]
    ya_ref[...] = a_b * conv * _silu(a_z)
    canew_ref[:, 0:D_A] = p1
    canew_ref[:, D_A:2 * D_A] = va
    lb = _hgrn_lower_bound(lb_ref[...], layer)
    gq, gf, gi, gz = (u[:, OFF_G + k * D_HG:OFF_G + (k + 1) * D_HG] for k in range(4))
    hgp_ref[:, 0:D_HG] = gq
    hgp_ref[:, D_HG:2 * D_HG] = lb + (1.0 - lb) * _sigmoid(gf)
    hgp_ref[:, 2 * D_HG:3 * D_HG] = gi
    hgp_ref[:, 3 * D_HG:] = gz
    sxbc = u[:, OFF_XBC:OFF_XBC + SSD_CONV_DIM]
    W = SSD_CONV_DIM
    q0, q1, q2 = sc_ref[:, 0:W], sc_ref[:, W:2 * W], sc_ref[:, 2 * W:3 * W]
    scw = scw_ref[layer]
    xbc = _silu(sxbc * scw[3:4, :] + q2 * scw[2:3, :] + q1 * scw[1:2, :] + q0 * scw[0:1, :] + row(scb_ref))
    scnew_ref[:, 0:W] = q1
    scnew_ref[:, W:2 * W] = q2
    scnew_ref[:, 2 * W:3 * W] = sxbc
    xs = xbc[:, 0:D_SSD]
    dt = _softplus(u[:, OFF_DT:OFF_DT + SSD_HEADS] + row(dtb_ref))
    dec = jnp.exp(dt * -jnp.exp(row(al_ref)))
    n = dt.shape[0]
    e_all = _dot(_split3_cols(jnp.concatenate([dt, dec], axis=0)), expand_ref[...])
    ssp_ref[:, 0:D_SSD] = xs
    ssp_ref[:, D_SSD:2 * D_SSD] = xs * e_all[0:n]
    ssp_ref[:, 2 * D_SSD:3 * D_SSD] = e_all[n:2 * n]
    ssp_ref[:, 3 * D_SSD:4 * D_SSD] = u[:, OFF_SZ:OFF_SZ + D_SSD]
    ssp_ref[:, 4 * D_SSD:] = xbc[:, D_SSD:]


def _sample_pre(x, ca, sc, wts, layer):
    n = x.shape[0]
    args = [x, ca, sc, wts["winT"], wts["caw"], wts["lb"], wts["scw"], wts["scb"], wts["dtb"], wts["al"],
            wts["gpre"], _consts()["expand3"]]
    out_shape = [jax.ShapeDtypeStruct((n, D_A), F32), jax.ShapeDtypeStruct((n, 2 * D_A), F32),
                 jax.ShapeDtypeStruct((n, D_HGP), F32), jax.ShapeDtypeStruct((n, D_SSP), F32),
                 jax.ShapeDtypeStruct((n, 3 * SSD_CONV_DIM), F32)]
    return pl.pallas_call(
        functools.partial(_sample_pre_kernel, layer=layer),
        in_specs=[_full_spec(a) for a in args[:3]] + [_layer_spec(args[3], layer)] + [_full_spec(a) for a in args[4:]],
        out_specs=[_full_spec(s) for s in out_shape],
        out_shape=out_shape, grid=(1,),
        compiler_params=pltpu.CompilerParams(dimension_semantics=("arbitrary",), vmem_limit_bytes=VMEM_LIMIT),
        name=f"sample_pre{layer}",
    )(*args)


def _pad_rows_T(blk):
    w = blk.shape[1]
    return jnp.concatenate([blk, jnp.zeros((128 - blk.shape[0], w), blk.dtype)], axis=0).T


def _sample_state_kernel(hgp_ref, ssp_ref, shg_ref, sss_ref, *rest):
    o_ref, y_ref, shg_out, sss_out = rest[-4:]
    rid_hg = lax.broadcasted_iota(jnp.int32, (SB, HG_DK), 0)
    for hd in range(HG_HEADS):
        cols = slice(hd * HG_DK, (hd + 1) * HG_DK)
        q_b = hgp_ref[:, cols].astype(BF16)
        fT = _pad_rows_T(hgp_ref[:, D_HG + hd * HG_DK:D_HG + (hd + 1) * HG_DK])
        o = jnp.zeros((SB, HG_DK), F32)
        for j in range(SB):
            fcol = fT[:, j:j + 1]
            vrow = hgp_ref[j:j + 1, 2 * D_HG + hd * HG_DK:2 * D_HG + (hd + 1) * HG_DK]
            s_new = vrow + fcol * (shg_ref[j, hd] - vrow)
            shg_out[j, hd] = s_new
            o = jnp.where(rid_hg == j, _dot(q_b, s_new.astype(BF16)), o)
        o_ref[:, cols] = o
    gw = (SSD_HEADS // SSD_GROUPS) * SSD_P
    rid_ss = lax.broadcasted_iota(jnp.int32, (SB, gw), 0)
    xdtT = _pad_rows_T(ssp_ref[:, D_SSD:2 * D_SSD])
    decT = _pad_rows_T(ssp_ref[:, 2 * D_SSD:3 * D_SSD])
    for g in range(SSD_GROUPS):
        rows = slice(g * gw, (g + 1) * gw)
        c_b = ssp_ref[:, 4 * D_SSD + (SSD_GROUPS + g) * SSD_N:4 * D_SSD + (SSD_GROUPS + g + 1) * SSD_N].astype(BF16)
        y = jnp.zeros((SB, gw), F32)
        for j in range(SB):
            brow = ssp_ref[j:j + 1, 4 * D_SSD + g * SSD_N:4 * D_SSD + (g + 1) * SSD_N]
            s_new = decT[rows, j:j + 1] * sss_ref[j, rows, :] + xdtT[rows, j:j + 1] * brow
            sss_out[j, rows, :] = s_new
            y = jnp.where(rid_ss == j, _dot_nt(c_b, s_new.astype(BF16)), y)
        y_ref[:, rows] = y


def _sample_state(hgp, ssp, shg_all, sss_all, layer, prev):
    n = hgp.shape[0]
    prev = [] if prev is None else list(prev)
    rowblk = lambda w: pl.BlockSpec((SB, w), lambda i: (i, 0))
    hg_blk = pl.BlockSpec((None, SB, HG_HEADS, HG_DK, HG_DK), lambda i: (layer, i, 0, 0, 0))
    ss_blk = pl.BlockSpec((None, SB, SSD_HEADS * SSD_P, SSD_N), lambda i: (layer, i, 0, 0))
    return pl.pallas_call(
        _sample_state_kernel,
        grid=(n // SB,),
        in_specs=[rowblk(D_HGP), rowblk(D_SSP), hg_blk, ss_blk] + [_ANY] * len(prev),
        out_specs=[rowblk(D_HG), rowblk(D_SSD), hg_blk, ss_blk],
        out_shape=[jax.ShapeDtypeStruct((n, D_HG), F32), jax.ShapeDtypeStruct((n, D_SSD), F32),
                   jax.ShapeDtypeStruct(shg_all.shape, F32), jax.ShapeDtypeStruct(sss_all.shape, F32)],
        input_output_aliases={4 + k: 2 + k for k in range(len(prev))},
        compiler_params=pltpu.CompilerParams(dimension_semantics=("arbitrary",), vmem_limit_bytes=VMEM_LIMIT),
        name="sample_state",
    )(hgp, ssp, shg_all, sss_all, *prev)


def _sample_mid_kernel(x_ref, ya_ref, o_ref, y_ref, hgp_ref, ssp_ref, wout_ref, wq_ref, gn_ref, dx_ref, snorm_ref,
                       gpost_ref, gprex_ref, x1_ref, q_ref, *, layer):
    row = lambda ref: ref[layer:layer + 1, :]
    gz = hgp_ref[:, 3 * D_HG:]
    o = o_ref[...]
    yb = jnp.concatenate([_rms(o[:, hd * HG_DK:(hd + 1) * HG_DK], gn_ref[layer:layer + 1, hd * HG_DK:(hd + 1) * HG_DK])
                          for hd in range(HG_HEADS)], axis=1) * _silu(gz)
    y = y_ref[...] + row(dx_ref) * ssp_ref[:, 0:D_SSD]
    yc = _rms(y * _silu(ssp_ref[:, 3 * D_SSD:4 * D_SSD]), row(snorm_ref))
    mix = jnp.concatenate([ya_ref[...], yb, yc], axis=1).astype(BF16)
    x1 = x_ref[...] + _rms(_dot(mix, wout_ref[...]), row(gpost_ref))
    x1_ref[...] = x1
    q = _dot(_rms(x1, row(gprex_ref)).astype(BF16), wq_ref[...])
    for hd in range(XA_HEADS):
        for k in range(XA_HD // 128):
            q_ref[:, k * XA_HEADS + hd, :] = q[:, hd * XA_HD + k * 128:hd * XA_HD + (k + 1) * 128]


def _sample_mid(x, ya, o, y, hgp, ssp, wts, layer):
    n = x.shape[0]
    args = [x, ya, o, y, hgp, ssp, wts["wout"], wts["wq"], wts["gn"], wts["dx"], wts["snorm"], wts["gpost"],
            wts["gprex"]]
    out_shape = [jax.ShapeDtypeStruct((n, D_MODEL), F32),
                 jax.ShapeDtypeStruct((n, XA_HEADS * (XA_HD // 128), 128), F32)]
    return pl.pallas_call(
        functools.partial(_sample_mid_kernel, layer=layer), grid=(1,),
        in_specs=[_full_spec(a) for a in args[:6]] + [_layer_spec(a, layer) for a in args[6:8]]
        + [_full_spec(a) for a in args[8:]], out_specs=[_full_spec(s) for s in out_shape], out_shape=out_shape,
        compiler_params=pltpu.CompilerParams(dimension_semantics=("arbitrary",), vmem_limit_bytes=VMEM_LIMIT),
        name="sample_mid",
    )(*args)


def _lane_class_reduce(x, op):
    sh = KV_SUB
    while sh < 128:
        x = op(x, pltpu.roll(x, sh, axis=1))
        sh *= 2
    return x


def _sample_attn_kernel(x1_ref, q_ref, k_ref, v_ref, wo_ref, gpostx_ref, x2_ref, *, layer):
    lane = lax.broadcasted_iota(jnp.int32, (KV_SUB, KV_ROWS), 1)
    sub = lax.broadcasted_iota(jnp.int32, (KV_SUB, KV_ROWS), 0)
    own = ((lane & (KV_SUB - 1)) == sub).astype(F32)
    rid = lax.broadcasted_iota(jnp.int32, (SB, KV_ROWS), 0)
    t_all = jnp.zeros((SB, KV_ROWS), F32)
    for j in range(SB):
        r = _dot_nt(q_ref[j].astype(BF16), k_ref[j].astype(BF16))
        t = jnp.sum(r * own, axis=0, keepdims=True)
        t_all = jnp.where(rid == j, t, t_all)
    n_tiles = KV_ROWS // 128
    lane1 = lax.broadcasted_iota(jnp.int32, (SB, 128), 1)
    piece = (lane1 // XA_HEADS) % KV_SPLIT
    chunks = []
    for c in range(n_tiles):
        x = t_all[:, c * 128:(c + 1) * 128]
        tot = x
        for k in range(1, KV_SPLIT):
            fwd = pltpu.roll(x, 128 - k * XA_HEADS, axis=1)
            bwd = pltpu.roll(x, (KV_SPLIT - k) * XA_HEADS, axis=1)
            tot = tot + jnp.where(piece + k < KV_SPLIT, fwd, bwd)
        chunks.append(tot * (XA_HD ** -0.5))
    mx = _lane_class_reduce(functools.reduce(jnp.maximum, chunks), jnp.maximum)
    es = [jnp.exp(ch - mx) for ch in chunks]
    den = _lane_class_reduce(functools.reduce(lambda a, b: a + b, es), lambda a, b: a + b)
    p_all = jnp.concatenate([e * (1.0 / den) for e in es], axis=1)
    rid_o = lax.broadcasted_iota(jnp.int32, (SB, D_MODEL), 0)
    att = jnp.zeros((SB, D_MODEL), F32)
    for j in range(SB):
        p8 = (own * p_all[j:j + 1, :]).astype(BF16)
        o = _dot(p8, v_ref[j].astype(BF16))
        row = jnp.concatenate([o[k * XA_HEADS + hd:k * XA_HEADS + hd + 1, :]
                               for hd in range(XA_HEADS) for k in range(KV_SPLIT)], axis=1)
        att = jnp.where(rid_o == j, row, att)
    x2_ref[...] = x1_ref[...] + _rms(_dot(att.astype(BF16), wo_ref[...]), gpostx_ref[layer:layer + 1, :])


def _sample_attn(x1, q8, ck_rows, cv_rows, wts, layer):
    n = x1.shape[0]
    rowblk = pl.BlockSpec((SB, D_MODEL), lambda i: (i, 0))
    qblk = pl.BlockSpec((SB, KV_SUB, 128), lambda i: (i, 0, 0))
    kvblk = pl.BlockSpec((None, SB, KV_ROWS, 128), lambda i: (layer, i, 0, 0))
    return pl.pallas_call(
        functools.partial(_sample_attn_kernel, layer=layer),
        grid=(n // SB,),
        in_specs=[rowblk, qblk, kvblk, kvblk, _layer_spec(wts["wo"], layer), _full_spec(wts["gpostx"])],
        out_specs=rowblk,
        out_shape=jax.ShapeDtypeStruct((n, D_MODEL), F32),
        compiler_params=pltpu.CompilerParams(dimension_semantics=("arbitrary",), vmem_limit_bytes=VMEM_LIMIT),
        name="sample_attn",
    )(x1, q8, ck_rows, cv_rows, wts["wo"], wts["gpostx"])


def _sample_layer(x, ca, shg_all, sc, sss_all, ck_rows, cv_rows, wts, layer, prev_states):
    n = x.shape[0]
    ya, ca_new, hgp, ssp, sc_new = _sample_pre(x, ca.reshape(n, -1), sc.reshape(n, -1), wts, layer)
    o, y, shg_new, sss_new = _sample_state(hgp, ssp, shg_all, sss_all, layer, prev_states)
    x1, q8 = _sample_mid(x, ya, o, y, hgp, ssp, wts, layer)
    x2 = _sample_attn(x1, q8, ck_rows, cv_rows, wts, layer)
    return x2, ca_new.reshape(ca.shape), sc_new.reshape(sc.shape), (shg_new, sss_new)


PROMPT_TILE = 512


def kernel(x_prompt, x_sample, mem_prompt, state_conv_a, state_hgrn, state_ssd_conv, state_ssd, cache_mem_k,
           cache_mem_v, w_in, conv_a_w, hgrn_lb, hgrn_gnorm, ssd_conv_w, ssd_conv_b, ssd_dt_bias, ssd_A_log, ssd_D,
           ssd_norm, w_out, g_pre_mix, g_post_mix, g_pre_x, g_post_x, g_mem, w_q, w_k, w_v, w_o):
    depth = w_in.shape[0]
    n = x_sample.shape[0]
    yp = x_prompt
    ys = x_sample.reshape(n, D_MODEL)
    ck_rows, cv_rows = _kv_rows_view(cache_mem_k), _kv_rows_view(cache_mem_v)
    sss_all = state_ssd.reshape(depth, n, SSD_HEADS * SSD_P, SSD_N)
    kv_rows = p_states = s_states = None
    s_ca, s_sc = [], []
    wts = _prep_weights(w_in, conv_a_w, hgrn_lb, hgrn_gnorm, ssd_conv_w, ssd_conv_b, ssd_dt_bias, ssd_A_log, ssd_D,
                        ssd_norm, w_out, g_pre_mix, g_post_mix, g_pre_x, g_post_x, g_mem, w_q, w_k, w_v, w_o)
    for l in range(depth):
        *kv_rows, mk, mv = _memory_kv(mem_prompt, wts, l, depth, kv_rows)
        yp, *p_states = _prompt_layer(yp, mk, mv, wts, l, depth, p_states, PROMPT_TILE)
        ys, ca, sc, s_states = _sample_layer(ys, state_conv_a[l], state_hgrn, state_ssd_conv[l], sss_all,
                                             ck_rows, cv_rows, wts, l, s_states)
        s_ca.append(ca)
        s_sc.append(sc)
    p_ca, p_hg, p_sc, p_ss = p_states
    s_hg, s_ss = s_states
    return (yp, ys.reshape(x_sample.shape), p_ca, p_hg, p_sc, p_ss, _kv_from_rows(kv_rows[0]),
            _kv_from_rows(kv_rows[1]), jnp.stack(s_ca), s_hg, jnp.stack(s_sc), s_ss.reshape(state_ssd.shape))
```

```python
import functools

import numpy as np
import jax
import jax.numpy as jnp
from jax import lax
from jax.experimental import pallas as pl
from jax.experimental.pallas import tpu as pltpu

F32 = jnp.float32
BF16 = jnp.bfloat16

D_MODEL = 1024
D_A = 512
CONV_A_W = 3
D_HG = 512
HG_HEADS = 4
HG_DK = 128
D_SSD = 1024
SSD_P = 64
SSD_HEADS = 16
SSD_GROUPS = 2
SSD_N = 128
SSD_CONV_W = 4
SSD_CONV_DIM = D_SSD + 2 * SSD_GROUPS * SSD_N
N_MEM = 256
XA_HEADS = 4
XA_HD = 256
EPS = 1e-6
KV_SPLIT = XA_HD // 128
KV_SUB = XA_HEADS * KV_SPLIT
KV_ROWS = N_MEM * KV_SUB

OFF_A = 0
OFF_G = 2048
OFF_SZ = 4096
OFF_XBC = 5120
OFF_DT = 6656
D_IN = 6672

CH = 128
HG_LEVELS = (1, 2, 4, 8, 16, 32, 64)
VMEM_LIMIT = 56 * 1024 * 1024


def _rms(x, g):
    ms = jnp.mean(x * x, axis=-1, keepdims=True)
    return x * lax.rsqrt(ms + EPS) * g


def _silu(x):
    return x * (1.0 / (1.0 + jnp.exp(-x)))


def _sigmoid(x):
    return 1.0 / (1.0 + jnp.exp(-x))


def _softplus(x):
    return jnp.maximum(x, 0.0) + jnp.log(1.0 + jnp.exp(-jnp.abs(x)))


def _dot(a, b):
    return jnp.dot(a, b, preferred_element_type=F32)


def _dot_nt(a, b):
    return lax.dot_general(a, b, (((1,), (1,)), ((), ())), preferred_element_type=F32)


def _dot_tn(a, b):
    return lax.dot_general(a, b, (((0,), (0,)), ((), ())), preferred_element_type=F32)


def _split3(x):
    hi = x.astype(BF16)
    r = x - hi.astype(F32)
    mid = r.astype(BF16)
    lo = (r - mid.astype(F32)).astype(BF16)
    return hi, mid, lo


def _split3_rows(x):
    return jnp.concatenate(_split3(x), axis=0)


def _split3_cols(x):
    return jnp.concatenate(_split3(x), axis=1)


@functools.lru_cache(maxsize=None)
def _consts():
    r = np.arange(CH)
    i, t = r[:, None], r[None, :]
    masks = [np.eye(CH, dtype=bool)]
    for s in HG_LEVELS:
        up = ((r // s) % 2 == 1)
        same = (i // (2 * s)) == (t // (2 * s))
        masks.append(same & up[:, None] & (~up)[None, :])
    masks = np.stack(masks).astype(np.float32)
    tril = (t <= i).astype(np.float32)
    tril3 = np.tile(tril, (1, 3))
    triu3 = np.tile(tril.T, (3, 1))
    e = (np.arange(D_SSD)[None, :] // SSD_P == np.arange(SSD_HEADS)[:, None]).astype(np.float32)
    expand3 = np.tile(e, (3, 1))
    return dict(
        masks=jnp.asarray(masks, F32),
        tril=jnp.asarray(tril, F32), tril3=jnp.asarray(tril3, BF16), triu3=jnp.asarray(triu3, BF16),
        expand3=jnp.asarray(expand3, BF16))


def _kv_rows_view(c_all):
    depth, n = c_all.shape[:2]
    c = c_all.reshape(depth, n, N_MEM, XA_HEADS, KV_SPLIT, 128)
    return jnp.transpose(c, (0, 1, 2, 4, 3, 5)).reshape(depth, n, KV_ROWS, 128)


def _kv_from_rows(r_all):
    depth, n = r_all.shape[:2]
    c = r_all.reshape(depth, n, N_MEM, KV_SPLIT, XA_HEADS, 128)
    return jnp.transpose(c, (0, 1, 2, 4, 3, 5)).reshape(depth, n, N_MEM, XA_HEADS, XA_HD)


def _store_kv_rows(r_ref, x):
    for hd in range(XA_HEADS):
        for k in range(KV_SPLIT):
            r_ref[:, k * XA_HEADS + hd, :] = x[:, hd * XA_HD + k * 128:hd * XA_HD + (k + 1) * 128]


_ANY = pl.BlockSpec(memory_space=pl.ANY)


def _full_spec(a):
    nd = a.ndim
    return pl.BlockSpec(a.shape, lambda *_, _n=nd: (0,) * _n)


def _layer_spec(a, layer, **kw):
    nd = a.ndim
    return pl.BlockSpec((None,) + tuple(a.shape[1:]), lambda *_, _n=nd: (layer,) + (0,) * (_n - 1), **kw)


def _memkv_kernel(mem_ref, g_ref, wk_ref, wv_ref, *refs, layer):
    kr_ref, vr_ref, kb_ref, vb_ref = refs[-4:]
    m = _rms(mem_ref[0], g_ref[layer:layer + 1, :]).astype(BF16)
    for w_ref, r_ref, b_ref in ((wk_ref, kr_ref, kb_ref), (wv_ref, vr_ref, vb_ref)):
        kv = _dot(m, w_ref[...])
        b_ref[0] = kv.astype(BF16)
        _store_kv_rows(r_ref, kv)


def _memory_kv(mem, wts, layer, depth, prev):
    b = mem.shape[0]
    blk = pl.BlockSpec((1, N_MEM, D_MODEL), lambda i: (i, 0, 0))
    rows_blk = pl.BlockSpec((None, None, N_MEM, KV_SUB, 128), lambda i: (layer, i, 0, 0, 0))
    rows_sds = jax.ShapeDtypeStruct((depth, b, N_MEM, KV_SUB, 128), F32)
    extra, extra_specs, aliases = [], [], {}
    if prev is not None:
        extra, extra_specs, aliases = list(prev), [_ANY, _ANY], {4: 0, 5: 1}
    return pl.pallas_call(
        functools.partial(_memkv_kernel, layer=layer),
        grid=(b,),
        in_specs=[blk, _full_spec(wts["gmem"]), _layer_spec(wts["wk"], layer), _layer_spec(wts["wv"], layer)]
        + extra_specs,
        out_specs=[rows_blk, rows_blk, blk, blk],
        out_shape=[rows_sds, rows_sds] + [jax.ShapeDtypeStruct((b, N_MEM, D_MODEL), BF16)] * 2,
        input_output_aliases=aliases,
        compiler_params=pltpu.CompilerParams(dimension_semantics=("arbitrary",), vmem_limit_bytes=VMEM_LIMIT),
        name="memory_kv",
    )(mem, wts["gmem"], wts["wk"], wts["wv"], *extra)


def _hgrn_lower_bound(lb_all, layer):
    depth = lb_all.shape[0]
    rows = [lb_all[j:j + 1, :] for j in range(depth)]
    mx = functools.reduce(jnp.maximum, rows)
    ex = [jnp.exp(rw - mx) for rw in rows]
    tot = functools.reduce(lambda a, b: a + b, ex)
    acc = jnp.zeros_like(tot)
    for j in range(1, layer + 1):
        acc = acc + ex[j]
    return acc / tot


def _hgrn_level(c, f, q, k, s):
    n, w = c.shape
    if s >= 8:
        nb = n // (2 * s)
        c4, q4, k4 = (a.reshape(nb, 2, s, w) for a in (c, q, k))
        lower, upper = c4[:, 0], c4[:, 1]
        tot = lower[:, s - 1:s, :]
        w_lower = k4[:, 0] * jnp.exp(tot - lower)
        w_upper = q4[:, 1] * jnp.exp(upper)
        wv = jnp.stack([w_lower, w_upper], axis=1).reshape(n, w)
        c_next = jnp.stack([lower, upper + tot], axis=1).reshape(n, w)
        return wv, c_next
    sub = lax.broadcasted_iota(jnp.int32, (1, 8, w), 1)
    c3, f3, q3, k3 = (a.reshape(n // 8, 8, w) for a in (c, f, q, k))
    up = (sub // s) % 2 == 1
    tot = None
    for gi in reversed(range(8 // (2 * s))):
        r = gi * 2 * s + s - 1
        tg = jnp.broadcast_to(c3[:, r:r + 1, :], c3.shape)
        tot = tg if tot is None else jnp.where(sub < (gi + 1) * 2 * s, tg, tot)
    if s == 1:
        e = jnp.where(up, f3, 1.0)
    else:
        e = jnp.exp(jnp.where(up, c3, tot - c3))
    wv = jnp.where(up, q3, k3) * e
    c_next = c3 + jnp.where(up, tot, 0.0)
    return wv.reshape(n, w), c_next.reshape(n, w)


def _causal_conv(x, taps, prev_ref):
    n_taps = len(taps)
    row0 = lax.broadcasted_iota(jnp.int32, (8, x.shape[1]), 0) == 0
    prev = [prev_ref[8 - d:8 - d + 1, :] for d in range(1, n_taps)]
    acc = x * taps[0]
    for k in range(1, n_taps):
        carry = functools.reduce(lambda a, b: a + b, [taps[j] * prev[k - j - 1] for j in range(k)])
        rolled = pltpu.roll(acc, 1, axis=0)
        shifted = jnp.concatenate([jnp.where(row0, carry, rolled[0:8]), rolled[8:]], axis=0)
        acc = x * taps[k] + shifted
    return acc


def _cross_attention(q, mk, mv):
    outs = []
    for hd in range(XA_HEADS):
        sl = slice(hd * XA_HD, (hd + 1) * XA_HD)
        s = _dot_nt(q[:, sl].astype(BF16), mk[:, sl]) * (XA_HD ** -0.5)
        s = s - jnp.max(s, axis=-1, keepdims=True)
        e = jnp.exp(s)
        p = e * (1.0 / jnp.sum(e, axis=-1, keepdims=True))
        outs.append(_dot(p.astype(BF16), mv[:, sl]))
    return jnp.concatenate(outs, axis=1)


def _prompt_kernel(x_ref, mk_ref, mv_ref, winT_ref, wout_ref, wq_ref, wo_ref,
                   caw_ref, lb_ref, gn_ref, scw_ref, scb_ref, dtb_ref, dtbc_ref, al_ref, alc_ref, dx_ref,
                   snorm_ref, gpre_ref, gpost_ref, gprex_ref, gpostx_ref,
                   masks_ref, tril_ref, tril3_ref, triu3_ref, expand_ref,
                   *rest, T, layer, n_prev):
    (y_ref, ca_ref, hg_ref, sc_ref, ss_ref,
     bufa, bufc, ug_s, z_s, xbc_s, dt_s, dtT_s, mix_s, sthg, stssd) = rest[n_prev:]
    ti = pl.program_id(1)
    n_chunks = T // CH

    @pl.when(ti == 0)
    def _():
        bufa[0:8, :] = jnp.zeros((8, D_A), F32)
        bufc[0:8, :] = jnp.zeros((8, SSD_CONV_DIM), F32)
        sthg[...] = jnp.zeros(sthg.shape, F32)
        stssd[...] = jnp.zeros(stssd.shape, F32)

    x = x_ref[0]
    row = lambda ref: ref[layer:layer + 1, :]
    h = _rms(x, row(gpre_ref)).astype(BF16)

    sxbc = _dot_nt(h, winT_ref[OFF_XBC:OFF_XBC + SSD_CONV_DIM, :])
    scw = scw_ref[layer]
    xbc = _causal_conv(sxbc, [scw[k:k + 1, :] for k in range(SSD_CONV_W)], bufc) + row(scb_ref)
    xbc_s[...] = _silu(xbc)
    sc_ref[0] = sxbc[T - 3:T, :]
    bufc[0:8, :] = sxbc[T - 8:T, :]
    wdtT = winT_ref[OFF_DT:OFF_DT + SSD_HEADS, :]
    sdt = _dot_nt(h, wdtT)
    dt_s[...] = _softplus(sdt + row(dtb_ref))
    dtT = _softplus(_dot_nt(wdtT, h) + dtbc_ref[:, layer:layer + 1])
    for c in range(n_chunks):
        dtT_s[c] = dtT[:, c * CH:(c + 1) * CH]
    z_s[...] = _dot_nt(h, winT_ref[OFF_SZ:OFF_SZ + D_SSD, :])

    ua = _dot_nt(h, winT_ref[OFF_A:OFF_A + 4 * D_A, :])
    a_h, a_b, a_c, a_z = (ua[:, k * D_A:(k + 1) * D_A] for k in range(4))
    va = a_c * a_h
    caw = caw_ref[layer]
    conv = _causal_conv(va, [caw[k:k + 1, :] for k in range(CONV_A_W)], bufa)
    mix_s[:, 0:D_A] = (a_b * conv * _silu(a_z)).astype(BF16)
    ca_ref[0] = va[T - 2:T, :]
    bufa[0:8, :] = va[T - 8:T, :]

    ug_s[...] = _dot_nt(h, winT_ref[OFF_G:OFF_G + 4 * D_HG, :])


    lb = _hgrn_lower_bound(lb_ref[...], layer)
    a_row = -jnp.exp(row(al_ref))
    a_col = -jnp.exp(alc_ref[:, layer:layer + 1])
    tril = tril_ref[...]
    first_of_pair = lax.broadcasted_iota(jnp.int32, (1, 2 * SSD_P), 1) < SSD_P

    def chunk(c, carry):
        r0 = pl.multiple_of(c * CH, CH)
        rows = pl.ds(r0, CH)

        ug = ug_s[rows, :]
        gq, gf, gi, gz = (ug[:, k * D_HG:(k + 1) * D_HG] for k in range(4))
        f = lb + (1.0 - lb) * _sigmoid(gf)
        logf = jnp.log(f)
        kk = 1.0 - f
        q_b, k_b, v_b = gq.astype(BF16), kk.astype(BF16), gi.astype(BF16)
        hs = [slice(hd * HG_DK, (hd + 1) * HG_DK) for hd in range(HG_HEADS)]
        A = [masks_ref[0] * _dot_nt(q_b[:, s_], k_b[:, s_]) for s_ in hs]
        G = logf
        for li, s in enumerate(HG_LEVELS):
            w, G = _hgrn_level(G, f, gq, kk, s)
            w = w.astype(BF16)
            m = masks_ref[li + 1]
            A = [A[hd] + m * _dot_nt(w[:, hs[hd]], w[:, hs[hd]]) for hd in range(HG_HEADS)]
        g_last = G[CH - 1:CH, :]
        qg = (gq * jnp.exp(G)).astype(BF16)
        kd = (kk * jnp.exp(g_last - G)).astype(BF16)
        dec = jnp.exp(g_last)
        o_heads = []
        for hd in range(HG_HEADS):
            s_ = hs[hd]
            st = sthg[hd]
            o = _dot_nt(qg[:, s_], st.astype(BF16)) + _dot(A[hd].astype(BF16), v_b[:, s_])
            sthg[hd] = st * dec[:, s_] + _dot_tn(v_b[:, s_], kd[:, s_])
            o_heads.append(_rms(o, gn_ref[layer:layer + 1, s_]))
        yb = jnp.concatenate(o_heads, axis=1) * _silu(gz)
        mix_s[rows, D_A:D_A + D_HG] = yb.astype(BF16)

        xbc_c = xbc_s[rows, :]
        xs = xbc_c[:, 0:D_SSD]
        Bm = xbc_c[:, D_SSD:D_SSD + SSD_GROUPS * SSD_N].astype(BF16)
        Cm = xbc_c[:, D_SSD + SSD_GROUPS * SSD_N:].astype(BF16)
        dt = dt_s[rows, :]
        dtT_c = dtT_s[c]
        cs = _dot(tril3_ref[...], _split3_rows(dt * a_row))
        csT = _dot(_split3_cols(dtT_c * a_col), triu3_ref[...])
        cs_last = cs[CH - 1:CH, :]
        w_all = jnp.concatenate([dt * jnp.exp(cs_last - cs), jnp.exp(cs), dt,
                                 jnp.broadcast_to(jnp.exp(cs_last), (8, SSD_HEADS))], axis=0)
        e_all = _dot(_split3_cols(w_all), expand_ref[...])
        e_dec, e_cs, e_dt, e_last = e_all[0:CH], e_all[CH:2 * CH], e_all[2 * CH:3 * CH], e_all[3 * CH:3 * CH + 1]
        xdt = (xs * e_dt).astype(BF16)
        xdec = (xs * e_dec).astype(BF16)
        y_groups = []
        hpg = SSD_HEADS // SSD_GROUPS
        gw = hpg * SSD_P
        for g in range(SSD_GROUPS):
            Cg = Cm[:, g * SSD_N:(g + 1) * SSD_N]
            Bg = Bm[:, g * SSD_N:(g + 1) * SSD_N]
            cb = _dot_nt(Cg, Bg) * tril
            st = stssd[g]
            gcols = slice(g * gw, (g + 1) * gw)
            y_off = _dot(Cg, st.astype(BF16)) * e_cs[:, gcols]
            stssd[g] = st * e_last[:, gcols] + _dot_tn(Bg, xdec[:, gcols])
            pair_out = []
            for pr in range(hpg // 2):
                h0 = g * hpg + 2 * pr
                ms = []
                for hh in (h0, h0 + 1):
                    diff = cs[:, hh:hh + 1] - csT[hh:hh + 1, :]
                    ms.append((cb * jnp.exp(jnp.minimum(diff, 0.0))).astype(BF16))
                both = _dot(jnp.concatenate(ms, axis=0), xdt[:, h0 * SSD_P:(h0 + 2) * SSD_P])
                pair_out.append(jnp.where(first_of_pair, both[0:CH], both[CH:2 * CH]))
            y_groups.append(y_off + jnp.concatenate(pair_out, axis=1))
        y = jnp.concatenate(y_groups, axis=1) + row(dx_ref) * xs
        yc = _rms(y * _silu(z_s[rows, :]), row(snorm_ref))
        mix_s[rows, D_A + D_HG:] = yc.astype(BF16)
        return carry

    lax.fori_loop(0, n_chunks, chunk, 0, unroll=True)

    x1 = x + _rms(_dot(mix_s[...], wout_ref[...]), row(gpost_ref))
    hx = _rms(x1, row(gprex_ref)).astype(BF16)
    q = _dot(hx, wq_ref[...])
    att = _cross_attention(q, mk_ref[0], mv_ref[0])
    y_ref[0] = x1 + _rms(_dot(att.astype(BF16), wo_ref[...]), row(gpostx_ref))

    @pl.when(ti == pl.num_programs(1) - 1)
    def _():
        for hd in range(HG_HEADS):
            hg_ref[0, hd] = sthg[hd].T
        hpg = SSD_HEADS // SSD_GROUPS
        for g in range(SSD_GROUPS):
            sg = stssd[g].T
            for hh in range(hpg):
                ss_ref[0, g * hpg + hh] = sg[hh * SSD_P:(hh + 1) * SSD_P, :]


def _prompt_layer(x, mk, mv, wts, layer, depth, prev, T):
    b, L, _ = x.shape
    prev = [] if prev is None else list(prev)
    c = _consts()
    n_chunks = T // CH
    const_names = ("masks", "tril", "tril3", "triu3", "expand3")
    consts = [c[k] for k in const_names]
    small = [wts[k] for k in ("caw", "lb", "gn", "scw", "scb", "dtb", "dtbc", "al", "alc", "dx", "snorm",
                              "gpre", "gpost", "gprex", "gpostx")]
    big = [wts[k] for k in ("winT", "wout", "wq", "wo")]

    full = lambda a: pl.BlockSpec(a.shape, lambda bi, ti, _n=a.ndim: (0,) * _n, pipeline_mode=pl.Buffered(1))
    big_spec = lambda a: _layer_spec(a, layer, pipeline_mode=pl.Buffered(1))

    in_specs = ([pl.BlockSpec((1, T, D_MODEL), lambda bi, ti: (bi, ti, 0)),
                 pl.BlockSpec((1, N_MEM, D_MODEL), lambda bi, ti: (bi, 0, 0)),
                 pl.BlockSpec((1, N_MEM, D_MODEL), lambda bi, ti: (bi, 0, 0))]
                + [big_spec(a) for a in big] + [full(a) for a in small] + [full(a) for a in consts]
                + [_ANY] * len(prev))
    n_in = len(in_specs)
    state_shapes = [(CONV_A_W - 1, D_A), (HG_HEADS, HG_DK, HG_DK), (SSD_CONV_W - 1, SSD_CONV_DIM),
                    (SSD_HEADS, SSD_P, SSD_N)]
    out_shape = ([jax.ShapeDtypeStruct((b, L, D_MODEL), F32)]
                 + [jax.ShapeDtypeStruct((depth, b) + s, F32) for s in state_shapes])
    out_specs = ([pl.BlockSpec((1, T, D_MODEL), lambda bi, ti: (bi, ti, 0))]
                 + [pl.BlockSpec((None, 1) + s, lambda bi, ti, _n=len(s): (layer, bi) + (0,) * _n)
                    for s in state_shapes])
    aliases = {n_in - len(prev) + k: 1 + k for k in range(len(prev))}
    scratch = [pltpu.VMEM((8, D_A), F32), pltpu.VMEM((8, SSD_CONV_DIM), F32),
               pltpu.VMEM((T, 4 * D_HG), F32), pltpu.VMEM((T, D_SSD), F32), pltpu.VMEM((T, SSD_CONV_DIM), F32),
               pltpu.VMEM((T, SSD_HEADS), F32), pltpu.VMEM((n_chunks, SSD_HEADS, CH), F32),
               pltpu.VMEM((T, 2 * D_MODEL), BF16),
               pltpu.VMEM((HG_HEADS, HG_DK, HG_DK), F32),
               pltpu.VMEM((SSD_GROUPS, SSD_N, (SSD_HEADS // SSD_GROUPS) * SSD_P), F32)]
    return pl.pallas_call(
        functools.partial(_prompt_kernel, T=T, layer=layer, n_prev=len(prev)),
        grid=(b, L // T),
        in_specs=in_specs, out_specs=out_specs, out_shape=out_shape, scratch_shapes=scratch,
        input_output_aliases=aliases,
        compiler_params=pltpu.CompilerParams(dimension_semantics=("arbitrary", "arbitrary"),
                                             vmem_limit_bytes=VMEM_LIMIT),
        name=f"prompt_layer{layer}",
    )(x, mk, mv, *big, *small, *consts, *prev)


def _prep_weights(w_in, conv_a_w, hgrn_lb, hgrn_gnorm, ssd_conv_w, ssd_conv_b, ssd_dt_bias, ssd_A_log, ssd_D, ssd_norm,
                  w_out, g_pre_mix, g_post_mix, g_pre_x, g_post_x, g_mem, w_q, w_k, w_v, w_o):
    return dict(
        winT=jnp.transpose(w_in, (0, 2, 1)).astype(BF16),
        wout=w_out.astype(BF16), wq=w_q.astype(BF16), wo=w_o.astype(BF16), wk=w_k.astype(BF16), wv=w_v.astype(BF16),
        caw=conv_a_w, lb=hgrn_lb, gn=hgrn_gnorm, scw=ssd_conv_w, scb=ssd_conv_b,
        dtb=ssd_dt_bias, dtbc=ssd_dt_bias.T, al=ssd_A_log, alc=ssd_A_log.T,
        dx=jnp.repeat(ssd_D, SSD_P, axis=1), snorm=ssd_norm,
        gpre=g_pre_mix, gpost=g_post_mix, gprex=g_pre_x, gpostx=g_post_x, gmem=g_mem)


SB = 8
D_HGP = 4 * D_HG
D_SSP = 4 * D_SSD + 2 * SSD_GROUPS * SSD_N


def _sample_pre_kernel(x_ref, ca_ref, sc_ref, winT_ref, caw_ref, lb_ref, scw_ref, scb_ref, dtb_ref, al_ref,
                       gpre_ref, expand_ref,
                       ya_ref, canew_ref, hgp_ref, ssp_ref, scnew_ref, *, layer):
    row = lambda ref: ref[layer:layer + 1, :]
    h = _rms(x_ref[...], row(gpre_ref)).astype(BF16)
    u = _dot_nt(h, winT_ref[...])
    a_h, a_b, a_c, a_z = (u[:, OFF_A + k * D_A:OFF_A + (k + 1) * D_A] for k in range(4))
    va = a_c * a_h
    p0, p1 = ca_ref[:, 0:D_A], ca_ref[:, D_A:2 * D_A]
    caw = caw_ref[layer]
    conv = va * caw[2:3, :] + p1 * caw[1:2, :] + p0 * caw[0:1, :]
    ya_ref[...] = a_b * conv * _silu(a_z)
    canew_ref[:, 0:D_A] = p1
    canew_ref[:, D_A:2 * D_A] = va
    lb = _hgrn_lower_bound(lb_ref[...], layer)
    gq, gf, gi, gz = (u[:, OFF_G + k * D_HG:OFF_G + (k + 1) * D_HG] for k in range(4))
    hgp_ref[:, 0:D_HG] = gq
    hgp_ref[:, D_HG:2 * D_HG] = lb + (1.0 - lb) * _sigmoid(gf)
    hgp_ref[:, 2 * D_HG:3 * D_HG] = gi
    hgp_ref[:, 3 * D_HG:] = gz
    sxbc = u[:, OFF_XBC:OFF_XBC + SSD_CONV_DIM]
    W = SSD_CONV_DIM
    q0, q1, q2 = sc_ref[:, 0:W], sc_ref[:, W:2 * W], sc_ref[:, 2 * W:3 * W]
    scw = scw_ref[layer]
    xbc = _silu(sxbc * scw[3:4, :] + q2 * scw[2:3, :] + q1 * scw[1:2, :] + q0 * scw[0:1, :] + row(scb_ref))
    scnew_ref[:, 0:W] = q1
    scnew_ref[:, W:2 * W] = q2
    scnew_ref[:, 2 * W:3 * W] = sxbc
    xs = xbc[:, 0:D_SSD]
    dt = _softplus(u[:, OFF_DT:OFF_DT + SSD_HEADS] + row(dtb_ref))
    dec = jnp.exp(dt * -jnp.exp(row(al_ref)))
    n = dt.shape[0]
    e_all = _dot(_split3_cols(jnp.concatenate([dt, dec], axis=0)), expand_ref[...])
    ssp_ref[:, 0:D_SSD] = xs
    ssp_ref[:, D_SSD:2 * D_SSD] = xs * e_all[0:n]
    ssp_ref[:, 2 * D_SSD:3 * D_SSD] = e_all[n:2 * n]
    ssp_ref[:, 3 * D_SSD:4 * D_SSD] = u[:, OFF_SZ:OFF_SZ + D_SSD]
    ssp_ref[:, 4 * D_SSD:] = xbc[:, D_SSD:]


def _sample_pre(x, ca, sc, wts, layer):
    n = x.shape[0]
    args = [x, ca, sc, wts["winT"], wts["caw"], wts["lb"], wts["scw"], wts["scb"], wts["dtb"], wts["al"],
            wts["gpre"], _consts()["expand3"]]
    out_shape = [jax.ShapeDtypeStruct((n, D_A), F32), jax.ShapeDtypeStruct((n, 2 * D_A), F32),
                 jax.ShapeDtypeStruct((n, D_HGP), F32), jax.ShapeDtypeStruct((n, D_SSP), F32),
                 jax.ShapeDtypeStruct((n, 3 * SSD_CONV_DIM), F32)]
    return pl.pallas_call(
        functools.partial(_sample_pre_kernel, layer=layer),
        in_specs=[_full_spec(a) for a in args[:3]] + [_layer_spec(args[3], layer)] + [_full_spec(a) for a in args[4:]],
        out_specs=[_full_spec(s) for s in out_shape],
        out_shape=out_shape, grid=(1,),
        compiler_params=pltpu.CompilerParams(dimension_semantics=("arbitrary",), vmem_limit_bytes=VMEM_LIMIT),
        name=f"sample_pre{layer}",
    )(*args)


def _pad_rows_T(blk):
    w = blk.shape[1]
    return jnp.concatenate([blk, jnp.zeros((128 - blk.shape[0], w), blk.dtype)], axis=0).T


def _sample_state_kernel(hgp_ref, ssp_ref, shg_ref, sss_ref, *rest):
    o_ref, y_ref, shg_out, sss_out = rest[-4:]
    rid_hg = lax.broadcasted_iota(jnp.int32, (SB, HG_DK), 0)
    for hd in range(HG_HEADS):
        cols = slice(hd * HG_DK, (hd + 1) * HG_DK)
        q_b = hgp_ref[:, cols].astype(BF16)
        fT = _pad_rows_T(hgp_ref[:, D_HG + hd * HG_DK:D_HG + (hd + 1) * HG_DK])
        o = jnp.zeros((SB, HG_DK), F32)
        for j in range(SB):
            fcol = fT[:, j:j + 1]
            vrow = hgp_ref[j:j + 1, 2 * D_HG + hd * HG_DK:2 * D_HG + (hd + 1) * HG_DK]
            s_new = vrow + fcol * (shg_ref[j, hd] - vrow)
            shg_out[j, hd] = s_new
            o = jnp.where(rid_hg == j, _dot(q_b, s_new.astype(BF16)), o)
        o_ref[:, cols] = o
    gw = (SSD_HEADS // SSD_GROUPS) * SSD_P
    rid_ss = lax.broadcasted_iota(jnp.int32, (SB, gw), 0)
    xdtT = _pad_rows_T(ssp_ref[:, D_SSD:2 * D_SSD])
    decT = _pad_rows_T(ssp_ref[:, 2 * D_SSD:3 * D_SSD])
    for g in range(SSD_GROUPS):
        rows = slice(g * gw, (g + 1) * gw)
        c_b = ssp_ref[:, 4 * D_SSD + (SSD_GROUPS + g) * SSD_N:4 * D_SSD + (SSD_GROUPS + g + 1) * SSD_N].astype(BF16)
        y = jnp.zeros((SB, gw), F32)
        for j in range(SB):
            brow = ssp_ref[j:j + 1, 4 * D_SSD + g * SSD_N:4 * D_SSD + (g + 1) * SSD_N]
            decayed = []
            for hh in range(SSD_HEADS // SSD_GROUPS):
                r0 = g * gw + hh * SSD_P
                dec_h = jnp.broadcast_to(decT[r0:r0 + 8, j:j + 1], (8, SSD_N))
                s_old = sss_ref[j, r0:r0 + SSD_P, :].reshape(SSD_P // 8, 8, SSD_N)
                decayed.append((s_old * dec_h[None]).reshape(SSD_P, SSD_N))
            s_new = jnp.concatenate(decayed, axis=0) + xdtT[rows, j:j + 1] * brow
            sss_out[j, rows, :] = s_new
            y = jnp.where(rid_ss == j, _dot_nt(c_b, s_new.astype(BF16)), y)
        y_ref[:, rows] = y


def _sample_state(hgp, ssp, shg_all, sss_all, layer, prev):
    n = hgp.shape[0]
    prev = [] if prev is None else list(prev)
    rowblk = lambda w: pl.BlockSpec((SB, w), lambda i: (i, 0))
    hg_blk = pl.BlockSpec((None, SB, HG_HEADS, HG_DK, HG_DK), lambda i: (layer, i, 0, 0, 0))
    ss_blk = pl.BlockSpec((None, SB, SSD_HEADS * SSD_P, SSD_N), lambda i: (layer, i, 0, 0))
    return pl.pallas_call(
        _sample_state_kernel,
        grid=(n // SB,),
        in_specs=[rowblk(D_HGP), rowblk(D_SSP), hg_blk, ss_blk] + [_ANY] * len(prev),
        out_specs=[rowblk(D_HG), rowblk(D_SSD), hg_blk, ss_blk],
        out_shape=[jax.ShapeDtypeStruct((n, D_HG), F32), jax.ShapeDtypeStruct((n, D_SSD), F32),
                   jax.ShapeDtypeStruct(shg_all.shape, F32), jax.ShapeDtypeStruct(sss_all.shape, F32)],
        input_output_aliases={4 + k: 2 + k for k in range(len(prev))},
        compiler_params=pltpu.CompilerParams(dimension_semantics=("arbitrary",), vmem_limit_bytes=VMEM_LIMIT),
        name="sample_state",
    )(hgp, ssp, shg_all, sss_all, *prev)


def _sample_mid_kernel(x_ref, ya_ref, o_ref, y_ref, hgp_ref, ssp_ref, wout_ref, wq_ref, gn_ref, dx_ref, snorm_ref,
                       gpost_ref, gprex_ref, x1_ref, q_ref, *, layer):
    row = lambda ref: ref[layer:layer + 1, :]
    gz = hgp_ref[:, 3 * D_HG:]
    o = o_ref[...]
    yb = jnp.concatenate([_rms(o[:, hd * HG_DK:(hd + 1) * HG_DK], gn_ref[layer:layer + 1, hd * HG_DK:(hd + 1) * HG_DK])
                          for hd in range(HG_HEADS)], axis=1) * _silu(gz)
    y = y_ref[...] + row(dx_ref) * ssp_ref[:, 0:D_SSD]
    yc = _rms(y * _silu(ssp_ref[:, 3 * D_SSD:4 * D_SSD]), row(snorm_ref))
    mix = jnp.concatenate([ya_ref[...], yb, yc], axis=1).astype(BF16)
    x1 = x_ref[...] + _rms(_dot(mix, wout_ref[...]), row(gpost_ref))
    x1_ref[...] = x1
    q = _dot(_rms(x1, row(gprex_ref)).astype(BF16), wq_ref[...])
    for hd in range(XA_HEADS):
        for k in range(XA_HD // 128):
            q_ref[:, k * XA_HEADS + hd, :] = q[:, hd * XA_HD + k * 128:hd * XA_HD + (k + 1) * 128]


def _sample_mid(x, ya, o, y, hgp, ssp, wts, layer):
    n = x.shape[0]
    args = [x, ya, o, y, hgp, ssp, wts["wout"], wts["wq"], wts["gn"], wts["dx"], wts["snorm"], wts["gpost"],
            wts["gprex"]]
    out_shape = [jax.ShapeDtypeStruct((n, D_MODEL), F32),
                 jax.ShapeDtypeStruct((n, XA_HEADS * (XA_HD // 128), 128), F32)]
    return pl.pallas_call(
        functools.partial(_sample_mid_kernel, layer=layer), grid=(1,),
        in_specs=[_full_spec(a) for a in args[:6]] + [_layer_spec(a, layer) for a in args[6:8]]
        + [_full_spec(a) for a in args[8:]], out_specs=[_full_spec(s) for s in out_shape], out_shape=out_shape,
        compiler_params=pltpu.CompilerParams(dimension_semantics=("arbitrary",), vmem_limit_bytes=VMEM_LIMIT),
        name="sample_mid",
    )(*args)


def _lane_class_reduce(x, op):
    sh = KV_SUB
    while sh < 128:
        x = op(x, pltpu.roll(x, sh, axis=1))
        sh *= 2
    return x


def _sample_attn_kernel(x1_ref, q_ref, k_ref, v_ref, wo_ref, gpostx_ref, x2_ref, *, layer):
    lane = lax.broadcasted_iota(jnp.int32, (KV_SUB, KV_ROWS), 1)
    sub = lax.broadcasted_iota(jnp.int32, (KV_SUB, KV_ROWS), 0)
    own = ((lane & (KV_SUB - 1)) == sub).astype(F32)
    rid = lax.broadcasted_iota(jnp.int32, (SB, KV_ROWS), 0)
    t_all = jnp.zeros((SB, KV_ROWS), F32)
    for j in range(SB):
        r = _dot_nt(q_ref[j].astype(BF16), k_ref[j].astype(BF16))
        t = jnp.sum(r * own, axis=0, keepdims=True)
        t_all = jnp.where(rid == j, t, t_all)
    n_tiles = KV_ROWS // 128
    lane1 = lax.broadcasted_iota(jnp.int32, (SB, 128), 1)
    piece = (lane1 // XA_HEADS) % KV_SPLIT
    chunks = []
    for c in range(n_tiles):
        x = t_all[:, c * 128:(c + 1) * 128]
        tot = x
        for k in range(1, KV_SPLIT):
            fwd = pltpu.roll(x, 128 - k * XA_HEADS, axis=1)
            bwd = pltpu.roll(x, (KV_SPLIT - k) * XA_HEADS, axis=1)
            tot = tot + jnp.where(piece + k < KV_SPLIT, fwd, bwd)
        chunks.append(tot * (XA_HD ** -0.5))
    mx = _lane_class_reduce(functools.reduce(jnp.maximum, chunks), jnp.maximum)
    es = [jnp.exp(ch - mx) for ch in chunks]
    den = _lane_class_reduce(functools.reduce(lambda a, b: a + b, es), lambda a, b: a + b)
    p_all = jnp.concatenate([e * (1.0 / den) for e in es], axis=1)
    rid_o = lax.broadcasted_iota(jnp.int32, (SB, D_MODEL), 0)
    att = jnp.zeros((SB, D_MODEL), F32)
    for j in range(SB):
        p8 = (own * p_all[j:j + 1, :]).astype(BF16)
        o = _dot(p8, v_ref[j].astype(BF16))
        row = jnp.concatenate([o[k * XA_HEADS + hd:k * XA_HEADS + hd + 1, :]
                               for hd in range(XA_HEADS) for k in range(KV_SPLIT)], axis=1)
        att = jnp.where(rid_o == j, row, att)
    x2_ref[...] = x1_ref[...] + _rms(_dot(att.astype(BF16), wo_ref[...]), gpostx_ref[layer:layer + 1, :])


def _sample_attn(x1, q8, ck_rows, cv_rows, wts, layer):
    n = x1.shape[0]
    rowblk = pl.BlockSpec((SB, D_MODEL), lambda i: (i, 0))
    qblk = pl.BlockSpec((SB, KV_SUB, 128), lambda i: (i, 0, 0))
    kvblk = pl.BlockSpec((None, SB, KV_ROWS, 128), lambda i: (layer, i, 0, 0))
    return pl.pallas_call(
        functools.partial(_sample_attn_kernel, layer=layer),
        grid=(n // SB,),
        in_specs=[rowblk, qblk, kvblk, kvblk, _layer_spec(wts["wo"], layer), _full_spec(wts["gpostx"])],
        out_specs=rowblk,
        out_shape=jax.ShapeDtypeStruct((n, D_MODEL), F32),
        compiler_params=pltpu.CompilerParams(dimension_semantics=("arbitrary",), vmem_limit_bytes=VMEM_LIMIT),
        name="sample_attn",
    )(x1, q8, ck_rows, cv_rows, wts["wo"], wts["gpostx"])


def _sample_layer(x, ca, shg_all, sc, sss_all, ck_rows, cv_rows, wts, layer, prev_states):
    n = x.shape[0]
    ya, ca_new, hgp, ssp, sc_new = _sample_pre(x, ca.reshape(n, -1), sc.reshape(n, -1), wts, layer)
    o, y, shg_new, sss_new = _sample_state(hgp, ssp, shg_all, sss_all, layer, prev_states)
    x1, q8 = _sample_mid(x, ya, o, y, hgp, ssp, wts, layer)
    x2 = _sample_attn(x1, q8, ck_rows, cv_rows, wts, layer)
    return x2, ca_new.reshape(ca.shape), sc_new.reshape(sc.shape), (shg_new, sss_new)


PROMPT_TILE = 512


def kernel(x_prompt, x_sample, mem_prompt, state_conv_a, state_hgrn, state_ssd_conv, state_ssd, cache_mem_k,
           cache_mem_v, w_in, conv_a_w, hgrn_lb, hgrn_gnorm, ssd_conv_w, ssd_conv_b, ssd_dt_bias, ssd_A_log, ssd_D,
           ssd_norm, w_out, g_pre_mix, g_post_mix, g_pre_x, g_post_x, g_mem, w_q, w_k, w_v, w_o):
    depth = w_in.shape[0]
    n = x_sample.shape[0]
    yp = x_prompt
    ys = x_sample.reshape(n, D_MODEL)
    ck_rows, cv_rows = _kv_rows_view(cache_mem_k), _kv_rows_view(cache_mem_v)
    sss_all = state_ssd.reshape(depth, n, SSD_HEADS * SSD_P, SSD_N)
    kv_rows = p_states = s_states = None
    s_ca, s_sc = [], []
    wts = _prep_weights(w_in, conv_a_w, hgrn_lb, hgrn_gnorm, ssd_conv_w, ssd_conv_b, ssd_dt_bias, ssd_A_log, ssd_D,
                        ssd_norm, w_out, g_pre_mix, g_post_mix, g_pre_x, g_post_x, g_mem, w_q, w_k, w_v, w_o)
    for l in range(depth):
        *kv_rows, mk, mv = _memory_kv(mem_prompt, wts, l, depth, kv_rows)
        yp, *p_states = _prompt_layer(yp, mk, mv, wts, l, depth, p_states, PROMPT_TILE)
        ys, ca, sc, s_states = _sample_layer(ys, state_conv_a[l], state_hgrn, state_ssd_conv[l], sss_all,
                                             ck_rows, cv_rows, wts, l, s_states)
        s_ca.append(ca)
        s_sc.append(sc)
    p_ca, p_hg, p_sc, p_ss = p_states
    s_hg, s_ss = s_states
    return (yp, ys.reshape(x_sample.shape), p_ca, p_hg, p_sc, p_ss, _kv_from_rows(kv_rows[0]),
            _kv_from_rows(kv_rows[1]), jnp.stack(s_ca), s_hg, jnp.stack(s_sc), s_ss.reshape(state_ssd.shape))
```

```python
import functools

import numpy as np
import jax
import jax.numpy as jnp
from jax import lax
from jax.experimental import pallas as pl
from jax.experimental.pallas import tpu as pltpu

F32 = jnp.float32
BF16 = jnp.bfloat16

D_MODEL = 1024
D_A = 512
CONV_A_W = 3
D_HG = 512
HG_HEADS = 4
HG_DK = 128
D_SSD = 1024
SSD_P = 64
SSD_HEADS = 16
SSD_GROUPS = 2
SSD_N = 128
SSD_CONV_W = 4
SSD_CONV_DIM = D_SSD + 2 * SSD_GROUPS * SSD_N
N_MEM = 256
XA_HEADS = 4
XA_HD = 256
EPS = 1e-6
KV_SPLIT = XA_HD // 128
KV_SUB = XA_HEADS * KV_SPLIT
KV_ROWS = N_MEM * KV_SUB

OFF_A = 0
OFF_G = 2048
OFF_SZ = 4096
OFF_XBC = 5120
OFF_DT = 6656
D_IN = 6672

CH = 128
HG_LEVELS = (1, 2, 4, 8, 16, 32, 64)
VMEM_LIMIT = 56 * 1024 * 1024


def _rms(x, g):
    ms = jnp.mean(x * x, axis=-1, keepdims=True)
    return x * lax.rsqrt(ms + EPS) * g


def _silu(x):
    return x * (1.0 / (1.0 + jnp.exp(-x)))


def _sigmoid(x):
    return 1.0 / (1.0 + jnp.exp(-x))


def _softplus(x):
    return jnp.maximum(x, 0.0) + jnp.log(1.0 + jnp.exp(-jnp.abs(x)))


def _dot(a, b):
    return jnp.dot(a, b, preferred_element_type=F32)


def _dot_nt(a, b):
    return lax.dot_general(a, b, (((1,), (1,)), ((), ())), preferred_element_type=F32)


def _dot_tn(a, b):
    return lax.dot_general(a, b, (((0,), (0,)), ((), ())), preferred_element_type=F32)


def _split3(x):
    hi = x.astype(BF16)
    r = x - hi.astype(F32)
    mid = r.astype(BF16)
    lo = (r - mid.astype(F32)).astype(BF16)
    return hi, mid, lo


def _split3_rows(x):
    return jnp.concatenate(_split3(x), axis=0)


def _split3_cols(x):
    return jnp.concatenate(_split3(x), axis=1)


@functools.lru_cache(maxsize=None)
def _consts():
    r = np.arange(CH)
    i, t = r[:, None], r[None, :]
    masks = [np.eye(CH, dtype=bool)]
    for s in HG_LEVELS:
        up = ((r // s) % 2 == 1)
        same = (i // (2 * s)) == (t // (2 * s))
        masks.append(same & up[:, None] & (~up)[None, :])
    masks = np.stack(masks).astype(np.float32)
    tril = (t <= i).astype(np.float32)
    tril3 = np.tile(tril, (1, 3))
    triu3 = np.tile(tril.T, (3, 1))
    e = (np.arange(D_SSD)[None, :] // SSD_P == np.arange(SSD_HEADS)[:, None]).astype(np.float32)
    expand3 = np.tile(e, (3, 1))
    return dict(
        masks=jnp.asarray(masks, F32),
        tril=jnp.asarray(tril, F32), tril3=jnp.asarray(tril3, BF16), triu3=jnp.asarray(triu3, BF16),
        expand3=jnp.asarray(expand3, BF16))


def _kv_rows_view(c_all):
    depth, n = c_all.shape[:2]
    c = c_all.reshape(depth, n, N_MEM, XA_HEADS, KV_SPLIT, 128)
    return jnp.transpose(c, (0, 1, 2, 4, 3, 5)).reshape(depth, n, KV_ROWS, 128)


def _kv_from_rows(r_all):
    depth, n = r_all.shape[:2]
    c = r_all.reshape(depth, n, N_MEM, KV_SPLIT, XA_HEADS, 128)
    return jnp.transpose(c, (0, 1, 2, 4, 3, 5)).reshape(depth, n, N_MEM, XA_HEADS, XA_HD)


def _to_kv_rows(x):
    pieces = [x[:, hd * XA_HD + k * 128:hd * XA_HD + (k + 1) * 128] for k in range(KV_SPLIT) for hd in range(XA_HEADS)]
    return jnp.concatenate(pieces, axis=1).reshape(x.shape[0] * KV_SUB, 128)


_ANY = pl.BlockSpec(memory_space=pl.ANY)


def _full_spec(a):
    nd = a.ndim
    return pl.BlockSpec(a.shape, lambda *_, _n=nd: (0,) * _n)


def _layer_spec(a, layer, **kw):
    nd = a.ndim
    return pl.BlockSpec((None,) + tuple(a.shape[1:]), lambda *_, _n=nd: (layer,) + (0,) * (_n - 1), **kw)


def _memkv_kernel(mem_ref, g_ref, wk_ref, wv_ref, *refs, layer):
    kr_ref, vr_ref, kb_ref, vb_ref = refs[-4:]
    m = _rms(mem_ref[0], g_ref[layer:layer + 1, :]).astype(BF16)
    for w_ref, r_ref, b_ref in ((wk_ref, kr_ref, kb_ref), (wv_ref, vr_ref, vb_ref)):
        kv = _dot(m, w_ref[...])
        b_ref[0] = kv.astype(BF16)
        r_ref[...] = _to_kv_rows(kv)


def _memory_kv(mem, wts, layer, depth, prev):
    b = mem.shape[0]
    blk = pl.BlockSpec((1, N_MEM, D_MODEL), lambda i: (i, 0, 0))
    rows_blk = pl.BlockSpec((None, None, KV_ROWS, 128), lambda i: (layer, i, 0, 0))
    rows_sds = jax.ShapeDtypeStruct((depth, b, KV_ROWS, 128), F32)
    extra, extra_specs, aliases = [], [], {}
    if prev is not None:
        extra, extra_specs, aliases = list(prev), [_ANY, _ANY], {4: 0, 5: 1}
    return pl.pallas_call(
        functools.partial(_memkv_kernel, layer=layer),
        grid=(b,),
        in_specs=[blk, _full_spec(wts["gmem"]), _layer_spec(wts["wk"], layer), _layer_spec(wts["wv"], layer)]
        + extra_specs,
        out_specs=[rows_blk, rows_blk, blk, blk],
        out_shape=[rows_sds, rows_sds] + [jax.ShapeDtypeStruct((b, N_MEM, D_MODEL), BF16)] * 2,
        input_output_aliases=aliases,
        compiler_params=pltpu.CompilerParams(dimension_semantics=("arbitrary",), vmem_limit_bytes=VMEM_LIMIT),
        name="memory_kv",
    )(mem, wts["gmem"], wts["wk"], wts["wv"], *extra)


def _hgrn_lower_bound(lb_all, layer):
    depth = lb_all.shape[0]
    rows = [lb_all[j:j + 1, :] for j in range(depth)]
    mx = functools.reduce(jnp.maximum, rows)
    ex = [jnp.exp(rw - mx) for rw in rows]
    tot = functools.reduce(lambda a, b: a + b, ex)
    acc = jnp.zeros_like(tot)
    for j in range(1, layer + 1):
        acc = acc + ex[j]
    return acc / tot


def _hgrn_level(c, f, q, k, s):
    n, w = c.shape
    if s >= 8:
        nb = n // (2 * s)
        c4, q4, k4 = (a.reshape(nb, 2, s, w) for a in (c, q, k))
        lower, upper = c4[:, 0], c4[:, 1]
        tot = lower[:, s - 1:s, :]
        w_lower = k4[:, 0] * jnp.exp(tot - lower)
        w_upper = q4[:, 1] * jnp.exp(upper)
        wv = jnp.stack([w_lower, w_upper], axis=1).reshape(n, w)
        c_next = jnp.stack([lower, upper + tot], axis=1).reshape(n, w)
        return wv, c_next
    sub = lax.broadcasted_iota(jnp.int32, (1, 8, w), 1)
    c3, f3, q3, k3 = (a.reshape(n // 8, 8, w) for a in (c, f, q, k))
    up = (sub // s) % 2 == 1
    tot = None
    for gi in reversed(range(8 // (2 * s))):
        r = gi * 2 * s + s - 1
        tg = jnp.broadcast_to(c3[:, r:r + 1, :], c3.shape)
        tot = tg if tot is None else jnp.where(sub < (gi + 1) * 2 * s, tg, tot)
    if s == 1:
        e = jnp.where(up, f3, 1.0)
    else:
        e = jnp.exp(jnp.where(up, c3, tot - c3))
    wv = jnp.where(up, q3, k3) * e
    c_next = c3 + jnp.where(up, tot, 0.0)
    return wv.reshape(n, w), c_next.reshape(n, w)


def _causal_conv(x, taps, prev_ref):
    n_taps = len(taps)
    row0 = lax.broadcasted_iota(jnp.int32, (8, x.shape[1]), 0) == 0
    prev = [prev_ref[8 - d:8 - d + 1, :] for d in range(1, n_taps)]
    acc = x * taps[0]
    for k in range(1, n_taps):
        carry = functools.reduce(lambda a, b: a + b, [taps[j] * prev[k - j - 1] for j in range(k)])
        rolled = pltpu.roll(acc, 1, axis=0)
        shifted = jnp.concatenate([jnp.where(row0, carry, rolled[0:8]), rolled[8:]], axis=0)
        acc = x * taps[k] + shifted
    return acc


def _cross_attention(q, mk, mv):
    outs = []
    for hd in range(XA_HEADS):
        sl = slice(hd * XA_HD, (hd + 1) * XA_HD)
        s = _dot_nt(q[:, sl].astype(BF16), mk[:, sl]) * (XA_HD ** -0.5)
        s = s - jnp.max(s, axis=-1, keepdims=True)
        e = jnp.exp(s)
        p = e * (1.0 / jnp.sum(e, axis=-1, keepdims=True))
        outs.append(_dot(p.astype(BF16), mv[:, sl]))
    return jnp.concatenate(outs, axis=1)


def _prompt_kernel(x_ref, mk_ref, mv_ref, winT_ref, wout_ref, wq_ref, wo_ref,
                   caw_ref, lb_ref, gn_ref, scw_ref, scb_ref, dtb_ref, dtbc_ref, al_ref, alc_ref, dx_ref,
                   snorm_ref, gpre_ref, gpost_ref, gprex_ref, gpostx_ref,
                   masks_ref, tril_ref, tril3_ref, triu3_ref, expand_ref,
                   *rest, T, layer, n_prev):
    (y_ref, ca_ref, hg_ref, sc_ref, ss_ref,
     bufa, bufc, ug_s, z_s, xbc_s, dt_s, dtT_s, mix_s, sthg, stssd) = rest[n_prev:]
    ti = pl.program_id(1)
    n_chunks = T // CH

    @pl.when(ti == 0)
    def _():
        bufa[0:8, :] = jnp.zeros((8, D_A), F32)
        bufc[0:8, :] = jnp.zeros((8, SSD_CONV_DIM), F32)
        sthg[...] = jnp.zeros(sthg.shape, F32)
        stssd[...] = jnp.zeros(stssd.shape, F32)

    x = x_ref[0]
    row = lambda ref: ref[layer:layer + 1, :]
    h = _rms(x, row(gpre_ref)).astype(BF16)

    sxbc = _dot_nt(h, winT_ref[OFF_XBC:OFF_XBC + SSD_CONV_DIM, :])
    scw = scw_ref[layer]
    xbc = _causal_conv(sxbc, [scw[k:k + 1, :] for k in range(SSD_CONV_W)], bufc) + row(scb_ref)
    xbc_s[...] = _silu(xbc)
    sc_ref[0] = sxbc[T - 3:T, :]
    bufc[0:8, :] = sxbc[T - 8:T, :]
    wdtT = winT_ref[OFF_DT:OFF_DT + SSD_HEADS, :]
    sdt = _dot_nt(h, wdtT)
    dt_s[...] = _softplus(sdt + row(dtb_ref))
    dtT = _softplus(_dot_nt(wdtT, h) + dtbc_ref[:, layer:layer + 1])
    for c in range(n_chunks):
        dtT_s[c] = dtT[:, c * CH:(c + 1) * CH]
    z_s[...] = _dot_nt(h, winT_ref[OFF_SZ:OFF_SZ + D_SSD, :])

    ua = _dot_nt(h, winT_ref[OFF_A:OFF_A + 4 * D_A, :])
    a_h, a_b, a_c, a_z = (ua[:, k * D_A:(k + 1) * D_A] for k in range(4))
    va = a_c * a_h
    caw = caw_ref[layer]
    conv = _causal_conv(va, [caw[k:k + 1, :] for k in range(CONV_A_W)], bufa)
    mix_s[:, 0:D_A] = (a_b * conv * _silu(a_z)).astype(BF16)
    ca_ref[0] = va[T - 2:T, :]
    bufa[0:8, :] = va[T - 8:T, :]

    ug_s[...] = _dot_nt(h, winT_ref[OFF_G:OFF_G + 4 * D_HG, :])


    lb = _hgrn_lower_bound(lb_ref[...], layer)
    a_row = -jnp.exp(row(al_ref))
    a_col = -jnp.exp(alc_ref[:, layer:layer + 1])
    tril = tril_ref[...]
    first_of_pair = lax.broadcasted_iota(jnp.int32, (1, 2 * SSD_P), 1) < SSD_P

    def chunk(c, carry):
        r0 = pl.multiple_of(c * CH, CH)
        rows = pl.ds(r0, CH)

        ug = ug_s[rows, :]
        gq, gf, gi, gz = (ug[:, k * D_HG:(k + 1) * D_HG] for k in range(4))
        f = lb + (1.0 - lb) * _sigmoid(gf)
        logf = jnp.log(f)
        kk = 1.0 - f
        q_b, k_b, v_b = gq.astype(BF16), kk.astype(BF16), gi.astype(BF16)
        hs = [slice(hd * HG_DK, (hd + 1) * HG_DK) for hd in range(HG_HEADS)]
        A = [masks_ref[0] * _dot_nt(q_b[:, s_], k_b[:, s_]) for s_ in hs]
        G = logf
        for li, s in enumerate(HG_LEVELS):
            w, G = _hgrn_level(G, f, gq, kk, s)
            w = w.astype(BF16)
            m = masks_ref[li + 1]
            A = [A[hd] + m * _dot_nt(w[:, hs[hd]], w[:, hs[hd]]) for hd in range(HG_HEADS)]
        g_last = G[CH - 1:CH, :]
        qg = (gq * jnp.exp(G)).astype(BF16)
        kd = (kk * jnp.exp(g_last - G)).astype(BF16)
        dec = jnp.exp(g_last)
        o_heads = []
        for hd in range(HG_HEADS):
            s_ = hs[hd]
            st = sthg[hd]
            o = _dot_nt(qg[:, s_], st.astype(BF16)) + _dot(A[hd].astype(BF16), v_b[:, s_])
            sthg[hd] = st * dec[:, s_] + _dot_tn(v_b[:, s_], kd[:, s_])
            o_heads.append(_rms(o, gn_ref[layer:layer + 1, s_]))
        yb = jnp.concatenate(o_heads, axis=1) * _silu(gz)
        mix_s[rows, D_A:D_A + D_HG] = yb.astype(BF16)

        xbc_c = xbc_s[rows, :]
        xs = xbc_c[:, 0:D_SSD]
        Bm = xbc_c[:, D_SSD:D_SSD + SSD_GROUPS * SSD_N].astype(BF16)
        Cm = xbc_c[:, D_SSD + SSD_GROUPS * SSD_N:].astype(BF16)
        dt = dt_s[rows, :]
        dtT_c = dtT_s[c]
        cs = _dot(tril3_ref[...], _split3_rows(dt * a_row))
        csT = _dot(_split3_cols(dtT_c * a_col), triu3_ref[...])
        cs_last = cs[CH - 1:CH, :]
        w_all = jnp.concatenate([dt * jnp.exp(cs_last - cs), jnp.exp(cs), dt,
                                 jnp.broadcast_to(jnp.exp(cs_last), (8, SSD_HEADS))], axis=0)
        e_all = _dot(_split3_cols(w_all), expand_ref[...])
        e_dec, e_cs, e_dt, e_last = e_all[0:CH], e_all[CH:2 * CH], e_all[2 * CH:3 * CH], e_all[3 * CH:3 * CH + 1]
        xdt = (xs * e_dt).astype(BF16)
        xdec = (xs * e_dec).astype(BF16)
        y_groups = []
        hpg = SSD_HEADS // SSD_GROUPS
        gw = hpg * SSD_P
        for g in range(SSD_GROUPS):
            Cg = Cm[:, g * SSD_N:(g + 1) * SSD_N]
            Bg = Bm[:, g * SSD_N:(g + 1) * SSD_N]
            cb = _dot_nt(Cg, Bg) * tril
            st = stssd[g]
            gcols = slice(g * gw, (g + 1) * gw)
            y_off = _dot(Cg, st.astype(BF16)) * e_cs[:, gcols]
            stssd[g] = st * e_last[:, gcols] + _dot_tn(Bg, xdec[:, gcols])
            pair_out = []
            for pr in range(hpg // 2):
                h0 = g * hpg + 2 * pr
                ms = []
                for hh in (h0, h0 + 1):
                    diff = cs[:, hh:hh + 1] - csT[hh:hh + 1, :]
                    ms.append((cb * jnp.exp(jnp.minimum(diff, 0.0))).astype(BF16))
                both = _dot(jnp.concatenate(ms, axis=0), xdt[:, h0 * SSD_P:(h0 + 2) * SSD_P])
                pair_out.append(jnp.where(first_of_pair, both[0:CH], both[CH:2 * CH]))
            y_groups.append(y_off + jnp.concatenate(pair_out, axis=1))
        y = jnp.concatenate(y_groups, axis=1) + row(dx_ref) * xs
        yc = _rms(y * _silu(z_s[rows, :]), row(snorm_ref))
        mix_s[rows, D_A + D_HG:] = yc.astype(BF16)
        return carry

    lax.fori_loop(0, n_chunks, chunk, 0, unroll=True)

    x1 = x + _rms(_dot(mix_s[...], wout_ref[...]), row(gpost_ref))
    hx = _rms(x1, row(gprex_ref)).astype(BF16)
    q = _dot(hx, wq_ref[...])
    att = _cross_attention(q, mk_ref[0], mv_ref[0])
    y_ref[0] = x1 + _rms(_dot(att.astype(BF16), wo_ref[...]), row(gpostx_ref))

    @pl.when(ti == pl.num_programs(1) - 1)
    def _():
        for hd in range(HG_HEADS):
            hg_ref[0, hd] = sthg[hd].T
        hpg = SSD_HEADS // SSD_GROUPS
        for g in range(SSD_GROUPS):
            sg = stssd[g].T
            for hh in range(hpg):
                ss_ref[0, g * hpg + hh] = sg[hh * SSD_P:(hh + 1) * SSD_P, :]


def _prompt_layer(x, mk, mv, wts, layer, depth, prev, T):
    b, L, _ = x.shape
    prev = [] if prev is None else list(prev)
    c = _consts()
    n_chunks = T // CH
    const_names = ("masks", "tril", "tril3", "triu3", "expand3")
    consts = [c[k] for k in const_names]
    small = [wts[k] for k in ("caw", "lb", "gn", "scw", "scb", "dtb", "dtbc", "al", "alc", "dx", "snorm",
                              "gpre", "gpost", "gprex", "gpostx")]
    big = [wts[k] for k in ("winT", "wout", "wq", "wo")]

    full = lambda a: pl.BlockSpec(a.shape, lambda bi, ti, _n=a.ndim: (0,) * _n, pipeline_mode=pl.Buffered(1))
    big_spec = lambda a: _layer_spec(a, layer, pipeline_mode=pl.Buffered(1))

    in_specs = ([pl.BlockSpec((1, T, D_MODEL), lambda bi, ti: (bi, ti, 0)),
                 pl.BlockSpec((1, N_MEM, D_MODEL), lambda bi, ti: (bi, 0, 0)),
                 pl.BlockSpec((1, N_MEM, D_MODEL), lambda bi, ti: (bi, 0, 0))]
                + [big_spec(a) for a in big] + [full(a) for a in small] + [full(a) for a in consts]
                + [_ANY] * len(prev))
    n_in = len(in_specs)
    state_shapes = [(CONV_A_W - 1, D_A), (HG_HEADS, HG_DK, HG_DK), (SSD_CONV_W - 1, SSD_CONV_DIM),
                    (SSD_HEADS, SSD_P, SSD_N)]
    out_shape = ([jax.ShapeDtypeStruct((b, L, D_MODEL), F32)]
                 + [jax.ShapeDtypeStruct((depth, b) + s, F32) for s in state_shapes])
    out_specs = ([pl.BlockSpec((1, T, D_MODEL), lambda bi, ti: (bi, ti, 0))]
                 + [pl.BlockSpec((None, 1) + s, lambda bi, ti, _n=len(s): (layer, bi) + (0,) * _n)
                    for s in state_shapes])
    aliases = {n_in - len(prev) + k: 1 + k for k in range(len(prev))}
    scratch = [pltpu.VMEM((8, D_A), F32), pltpu.VMEM((8, SSD_CONV_DIM), F32),
               pltpu.VMEM((T, 4 * D_HG), F32), pltpu.VMEM((T, D_SSD), F32), pltpu.VMEM((T, SSD_CONV_DIM), F32),
               pltpu.VMEM((T, SSD_HEADS), F32), pltpu.VMEM((n_chunks, SSD_HEADS, CH), F32),
               pltpu.VMEM((T, 2 * D_MODEL), BF16),
               pltpu.VMEM((HG_HEADS, HG_DK, HG_DK), F32),
               pltpu.VMEM((SSD_GROUPS, SSD_N, (SSD_HEADS // SSD_GROUPS) * SSD_P), F32)]
    return pl.pallas_call(
        functools.partial(_prompt_kernel, T=T, layer=layer, n_prev=len(prev)),
        grid=(b, L // T),
        in_specs=in_specs, out_specs=out_specs, out_shape=out_shape, scratch_shapes=scratch,
        input_output_aliases=aliases,
        compiler_params=pltpu.CompilerParams(dimension_semantics=("arbitrary", "arbitrary"),
                                             vmem_limit_bytes=VMEM_LIMIT),
        name=f"prompt_layer{layer}",
    )(x, mk, mv, *big, *small, *consts, *prev)


def _prep_weights(w_in, conv_a_w, hgrn_lb, hgrn_gnorm, ssd_conv_w, ssd_conv_b, ssd_dt_bias, ssd_A_log, ssd_D, ssd_norm,
                  w_out, g_pre_mix, g_post_mix, g_pre_x, g_post_x, g_mem, w_q, w_k, w_v, w_o):
    return dict(
        winT=jnp.transpose(w_in, (0, 2, 1)).astype(BF16),
        wout=w_out.astype(BF16), wq=w_q.astype(BF16), wo=w_o.astype(BF16), wk=w_k.astype(BF16), wv=w_v.astype(BF16),
        caw=conv_a_w, lb=hgrn_lb, gn=hgrn_gnorm, scw=ssd_conv_w, scb=ssd_conv_b,
        dtb=ssd_dt_bias, dtbc=ssd_dt_bias.T, al=ssd_A_log, alc=ssd_A_log.T,
        dx=jnp.repeat(ssd_D, SSD_P, axis=1), snorm=ssd_norm,
        gpre=g_pre_mix, gpost=g_post_mix, gprex=g_pre_x, gpostx=g_post_x, gmem=g_mem)


SB = 8
D_HGP = 4 * D_HG
D_SSP = 4 * D_SSD + 2 * SSD_GROUPS * SSD_N


def _sample_pre_kernel(x_ref, ca_ref, sc_ref, winT_ref, caw_ref, lb_ref, scw_ref, scb_ref, dtb_ref, al_ref,
                       gpre_ref, expand_ref,
                       ya_ref, canew_ref, hgp_ref, ssp_ref, scnew_ref, *, layer):
    row = lambda ref: ref[layer:layer + 1, :]
    h = _rms(x_ref[...], row(gpre_ref)).astype(BF16)
    u = _dot_nt(h, winT_ref[...])
    a_h, a_b, a_c, a_z = (u[:, OFF_A + k * D_A:OFF_A + (k + 1) * D_A] for k in range(4))
    va = a_c * a_h
    p0, p1 = ca_ref[:, 0:D_A], ca_ref[:, D_A:2 * D_A]
    caw = caw_ref[layer]
    conv = va * caw[2:3, :] + p1 * caw[1:2, :] + p0 * caw[0:1, :]
    ya_ref[...] = a_b * conv * _silu(a_z)
    canew_ref[:, 0:D_A] = p1
    canew_ref[:, D_A:2 * D_A] = va
    lb = _hgrn_lower_bound(lb_ref[...], layer)
    gq, gf, gi, gz = (u[:, OFF_G + k * D_HG:OFF_G + (k + 1) * D_HG] for k in range(4))
    hgp_ref[:, 0:D_HG] = gq
    hgp_ref[:, D_HG:2 * D_HG] = lb + (1.0 - lb) * _sigmoid(gf)
    hgp_ref[:, 2 * D_HG:3 * D_HG] = gi
    hgp_ref[:, 3 * D_HG:] = gz
    sxbc = u[:, OFF_XBC:OFF_XBC + SSD_CONV_DIM]
    W = SSD_CONV_DIM
    q0, q1, q2 = sc_ref[:, 0:W], sc_ref[:, W:2 * W], sc_ref[:, 2 * W:3 * W]
    scw = scw_ref[layer]
    xbc = _silu(sxbc * scw[3:4, :] + q2 * scw[2:3, :] + q1 * scw[1:2, :] + q0 * scw[0:1, :] + row(scb_ref))
    scnew_ref[:, 0:W] = q1
    scnew_ref[:, W:2 * W] = q2
    scnew_ref[:, 2 * W:3 * W] = sxbc
    xs = xbc[:, 0:D_SSD]
    dt = _softplus(u[:, OFF_DT:OFF_DT + SSD_HEADS] + row(dtb_ref))
    dec = jnp.exp(dt * -jnp.exp(row(al_ref)))
    n = dt.shape[0]
    e_all = _dot(_split3_cols(jnp.concatenate([dt, dec], axis=0)), expand_ref[...])
    ssp_ref[:, 0:D_SSD] = xs
    ssp_ref[:, D_SSD:2 * D_SSD] = xs * e_all[0:n]
    ssp_ref[:, 2 * D_SSD:3 * D_SSD] = e_all[n:2 * n]
    ssp_ref[:, 3 * D_SSD:4 * D_SSD] = u[:, OFF_SZ:OFF_SZ + D_SSD]
    ssp_ref[:, 4 * D_SSD:] = xbc[:, D_SSD:]


def _sample_pre(x, ca, sc, wts, layer):
    n = x.shape[0]
    args = [x, ca, sc, wts["winT"], wts["caw"], wts["lb"], wts["scw"], wts["scb"], wts["dtb"], wts["al"],
            wts["gpre"], _consts()["expand3"]]
    out_shape = [jax.ShapeDtypeStruct((n, D_A), F32), jax.ShapeDtypeStruct((n, 2 * D_A), F32),
                 jax.ShapeDtypeStruct((n, D_HGP), F32), jax.ShapeDtypeStruct((n, D_SSP), F32),
                 jax.ShapeDtypeStruct((n, 3 * SSD_CONV_DIM), F32)]
    return pl.pallas_call(
        functools.partial(_sample_pre_kernel, layer=layer),
        in_specs=[_full_spec(a) for a in args[:3]] + [_layer_spec(args[3], layer)] + [_full_spec(a) for a in args[4:]],
        out_specs=[_full_spec(s) for s in out_shape],
        out_shape=out_shape, grid=(1,),
        compiler_params=pltpu.CompilerParams(dimension_semantics=("arbitrary",), vmem_limit_bytes=VMEM_LIMIT),
        name=f"sample_pre{layer}",
    )(*args)


def _pad_rows_T(blk):
    w = blk.shape[1]
    return jnp.concatenate([blk, jnp.zeros((128 - blk.shape[0], w), blk.dtype)], axis=0).T


def _sample_state_kernel(hgp_ref, ssp_ref, shg_ref, sss_ref, *rest):
    o_ref, y_ref, shg_out, sss_out = rest[-4:]
    rid_hg = lax.broadcasted_iota(jnp.int32, (SB, HG_DK), 0)
    for hd in range(HG_HEADS):
        cols = slice(hd * HG_DK, (hd + 1) * HG_DK)
        q_b = hgp_ref[:, cols].astype(BF16)
        fT = _pad_rows_T(hgp_ref[:, D_HG + hd * HG_DK:D_HG + (hd + 1) * HG_DK])
        o = jnp.zeros((SB, HG_DK), F32)
        for j in range(SB):
            fcol = fT[:, j:j + 1]
            vrow = hgp_ref[j:j + 1, 2 * D_HG + hd * HG_DK:2 * D_HG + (hd + 1) * HG_DK]
            s_new = vrow + fcol * (shg_ref[j, hd] - vrow)
            shg_out[j, hd] = s_new
            o = jnp.where(rid_hg == j, _dot(q_b, s_new.astype(BF16)), o)
        o_ref[:, cols] = o
    gw = (SSD_HEADS // SSD_GROUPS) * SSD_P
    rid_ss = lax.broadcasted_iota(jnp.int32, (SB, gw), 0)
    xdtT = _pad_rows_T(ssp_ref[:, D_SSD:2 * D_SSD])
    decT = _pad_rows_T(ssp_ref[:, 2 * D_SSD:3 * D_SSD])
    for g in range(SSD_GROUPS):
        rows = slice(g * gw, (g + 1) * gw)
        c_b = ssp_ref[:, 4 * D_SSD + (SSD_GROUPS + g) * SSD_N:4 * D_SSD + (SSD_GROUPS + g + 1) * SSD_N].astype(BF16)
        y = jnp.zeros((SB, gw), F32)
        for j in range(SB):
            brow = ssp_ref[j:j + 1, 4 * D_SSD + g * SSD_N:4 * D_SSD + (g + 1) * SSD_N]
            decayed = []
            for hh in range(SSD_HEADS // SSD_GROUPS):
                r0 = g * gw + hh * SSD_P
                dec_h = jnp.broadcast_to(decT[r0:r0 + 8, j:j + 1], (8, SSD_N))
                s_old = sss_ref[j, r0:r0 + SSD_P, :].reshape(SSD_P // 8, 8, SSD_N)
                decayed.append((s_old * dec_h[None]).reshape(SSD_P, SSD_N))
            s_new = jnp.concatenate(decayed, axis=0) + xdtT[rows, j:j + 1] * brow
            sss_out[j, rows, :] = s_new
            y = jnp.where(rid_ss == j, _dot_nt(c_b, s_new.astype(BF16)), y)
        y_ref[:, rows] = y


def _sample_state(hgp, ssp, shg_all, sss_all, layer, prev):
    n = hgp.shape[0]
    prev = [] if prev is None else list(prev)
    rowblk = lambda w: pl.BlockSpec((SB, w), lambda i: (i, 0))
    hg_blk = pl.BlockSpec((None, SB, HG_HEADS, HG_DK, HG_DK), lambda i: (layer, i, 0, 0, 0))
    ss_blk = pl.BlockSpec((None, SB, SSD_HEADS * SSD_P, SSD_N), lambda i: (layer, i, 0, 0))
    return pl.pallas_call(
        _sample_state_kernel,
        grid=(n // SB,),
        in_specs=[rowblk(D_HGP), rowblk(D_SSP), hg_blk, ss_blk] + [_ANY] * len(prev),
        out_specs=[rowblk(D_HG), rowblk(D_SSD), hg_blk, ss_blk],
        out_shape=[jax.ShapeDtypeStruct((n, D_HG), F32), jax.ShapeDtypeStruct((n, D_SSD), F32),
                   jax.ShapeDtypeStruct(shg_all.shape, F32), jax.ShapeDtypeStruct(sss_all.shape, F32)],
        input_output_aliases={4 + k: 2 + k for k in range(len(prev))},
        compiler_params=pltpu.CompilerParams(dimension_semantics=("arbitrary",), vmem_limit_bytes=VMEM_LIMIT),
        name="sample_state",
    )(hgp, ssp, shg_all, sss_all, *prev)


def _sample_mid_kernel(x_ref, ya_ref, o_ref, y_ref, hgp_ref, ssp_ref, wout_ref, wq_ref, gn_ref, dx_ref, snorm_ref,
                       gpost_ref, gprex_ref, x1_ref, q_ref, *, layer):
    row = lambda ref: ref[layer:layer + 1, :]
    gz = hgp_ref[:, 3 * D_HG:]
    o = o_ref[...]
    yb = jnp.concatenate([_rms(o[:, hd * HG_DK:(hd + 1) * HG_DK], gn_ref[layer:layer + 1, hd * HG_DK:(hd + 1) * HG_DK])
                          for hd in range(HG_HEADS)], axis=1) * _silu(gz)
    y = y_ref[...] + row(dx_ref) * ssp_ref[:, 0:D_SSD]
    yc = _rms(y * _silu(ssp_ref[:, 3 * D_SSD:4 * D_SSD]), row(snorm_ref))
    mix = jnp.concatenate([ya_ref[...], yb, yc], axis=1).astype(BF16)
    x1 = x_ref[...] + _rms(_dot(mix, wout_ref[...]), row(gpost_ref))
    x1_ref[...] = x1
    q = _dot(_rms(x1, row(gprex_ref)).astype(BF16), wq_ref[...])
    q_ref[...] = _to_kv_rows(q)


def _sample_mid(x, ya, o, y, hgp, ssp, wts, layer):
    n = x.shape[0]
    args = [x, ya, o, y, hgp, ssp, wts["wout"], wts["wq"], wts["gn"], wts["dx"], wts["snorm"], wts["gpost"],
            wts["gprex"]]
    out_shape = [jax.ShapeDtypeStruct((n, D_MODEL), F32),
                 jax.ShapeDtypeStruct((n * KV_SUB, 128), F32)]
    return pl.pallas_call(
        functools.partial(_sample_mid_kernel, layer=layer), grid=(1,),
        in_specs=[_full_spec(a) for a in args[:6]] + [_layer_spec(a, layer) for a in args[6:8]]
        + [_full_spec(a) for a in args[8:]], out_specs=[_full_spec(s) for s in out_shape], out_shape=out_shape,
        compiler_params=pltpu.CompilerParams(dimension_semantics=("arbitrary",), vmem_limit_bytes=VMEM_LIMIT),
        name="sample_mid",
    )(*args)


def _lane_class_reduce(x, op):
    sh = KV_SUB
    while sh < 128:
        x = op(x, pltpu.roll(x, sh, axis=1))
        sh *= 2
    return x


def _sample_attn_kernel(x1_ref, q_ref, k_ref, v_ref, wo_ref, gpostx_ref, x2_ref, *, layer):
    lane = lax.broadcasted_iota(jnp.int32, (KV_SUB, KV_ROWS), 1)
    sub = lax.broadcasted_iota(jnp.int32, (KV_SUB, KV_ROWS), 0)
    own = ((lane & (KV_SUB - 1)) == sub).astype(F32)
    rid = lax.broadcasted_iota(jnp.int32, (SB, KV_ROWS), 0)
    t_all = jnp.zeros((SB, KV_ROWS), F32)
    for j in range(SB):
        r = _dot_nt(q_ref[j].astype(BF16), k_ref[j].astype(BF16))
        t = jnp.sum(r * own, axis=0, keepdims=True)
        t_all = jnp.where(rid == j, t, t_all)
    n_tiles = KV_ROWS // 128
    lane1 = lax.broadcasted_iota(jnp.int32, (SB, 128), 1)
    piece = (lane1 // XA_HEADS) % KV_SPLIT
    chunks = []
    for c in range(n_tiles):
        x = t_all[:, c * 128:(c + 1) * 128]
        tot = x
        for k in range(1, KV_SPLIT):
            fwd = pltpu.roll(x, 128 - k * XA_HEADS, axis=1)
            bwd = pltpu.roll(x, (KV_SPLIT - k) * XA_HEADS, axis=1)
            tot = tot + jnp.where(piece + k < KV_SPLIT, fwd, bwd)
        chunks.append(tot * (XA_HD ** -0.5))
    mx = _lane_class_reduce(functools.reduce(jnp.maximum, chunks), jnp.maximum)
    es = [jnp.exp(ch - mx) for ch in chunks]
    den = _lane_class_reduce(functools.reduce(lambda a, b: a + b, es), lambda a, b: a + b)
    p_all = jnp.concatenate([e * (1.0 / den) for e in es], axis=1)
    rid_o = lax.broadcasted_iota(jnp.int32, (SB, D_MODEL), 0)
    att = jnp.zeros((SB, D_MODEL), F32)
    for j in range(SB):
        p8 = (own * p_all[j:j + 1, :]).astype(BF16)
        o = _dot(p8, v_ref[j].astype(BF16))
        row = jnp.concatenate([o[k * XA_HEADS + hd:k * XA_HEADS + hd + 1, :]
                               for hd in range(XA_HEADS) for k in range(KV_SPLIT)], axis=1)
        att = jnp.where(rid_o == j, row, att)
    x2_ref[...] = x1_ref[...] + _rms(_dot(att.astype(BF16), wo_ref[...]), gpostx_ref[layer:layer + 1, :])


def _sample_attn(x1, q8, ck_rows, cv_rows, wts, layer):
    n = x1.shape[0]
    rowblk = pl.BlockSpec((SB, D_MODEL), lambda i: (i, 0))
    qblk = pl.BlockSpec((SB, KV_SUB, 128), lambda i: (i, 0, 0))
    kvblk = pl.BlockSpec((None, SB, KV_ROWS, 128), lambda i: (layer, i, 0, 0))
    return pl.pallas_call(
        functools.partial(_sample_attn_kernel, layer=layer),
        grid=(n // SB,),
        in_specs=[rowblk, qblk, kvblk, kvblk, _layer_spec(wts["wo"], layer), _full_spec(wts["gpostx"])],
        out_specs=rowblk,
        out_shape=jax.ShapeDtypeStruct((n, D_MODEL), F32),
        compiler_params=pltpu.CompilerParams(dimension_semantics=("arbitrary",), vmem_limit_bytes=VMEM_LIMIT),
        name="sample_attn",
    )(x1, q8, ck_rows, cv_rows, wts["wo"], wts["gpostx"])


def _sample_layer(x, ca, shg_all, sc, sss_all, ck_rows, cv_rows, wts, layer, prev_states):
    n = x.shape[0]
    ya, ca_new, hgp, ssp, sc_new = _sample_pre(x, ca.reshape(n, -1), sc.reshape(n, -1), wts, layer)
    o, y, shg_new, sss_new = _sample_state(hgp, ssp, shg_all, sss_all, layer, prev_states)
    x1, q8 = _sample_mid(x, ya, o, y, hgp, ssp, wts, layer)
    x2 = _sample_attn(x1, q8.reshape(n, KV_SUB, 128), ck_rows, cv_rows, wts, layer)
    return x2, ca_new.reshape(ca.shape), sc_new.reshape(sc.shape), (shg_new, sss_new)


PROMPT_TILE = 512


def kernel(x_prompt, x_sample, mem_prompt, state_conv_a, state_hgrn, state_ssd_conv, state_ssd, cache_mem_k,
           cache_mem_v, w_in, conv_a_w, hgrn_lb, hgrn_gnorm, ssd_conv_w, ssd_conv_b, ssd_dt_bias, ssd_A_log, ssd_D,
           ssd_norm, w_out, g_pre_mix, g_post_mix, g_pre_x, g_post_x, g_mem, w_q, w_k, w_v, w_o):
    depth = w_in.shape[0]
    n = x_sample.shape[0]
    yp = x_prompt
    ys = x_sample.reshape(n, D_MODEL)
    ck_rows, cv_rows = _kv_rows_view(cache_mem_k), _kv_rows_view(cache_mem_v)
    sss_all = state_ssd.reshape(depth, n, SSD_HEADS * SSD_P, SSD_N)
    kv_rows = p_states = s_states = None
    s_ca, s_sc = [], []
    wts = _prep_weights(w_in, conv_a_w, hgrn_lb, hgrn_gnorm, ssd_conv_w, ssd_conv_b, ssd_dt_bias, ssd_A_log, ssd_D,
                        ssd_norm, w_out, g_pre_mix, g_post_mix, g_pre_x, g_post_x, g_mem, w_q, w_k, w_v, w_o)
    for l in range(depth):
        *kv_rows, mk, mv = _memory_kv(mem_prompt, wts, l, depth, kv_rows)
        yp, *p_states = _prompt_layer(yp, mk, mv, wts, l, depth, p_states, PROMPT_TILE)
        ys, ca, sc, s_states = _sample_layer(ys, state_conv_a[l], state_hgrn, state_ssd_conv[l], sss_all,
                                             ck_rows, cv_rows, wts, l, s_states)
        s_ca.append(ca)
        s_sc.append(sc)
    p_ca, p_hg, p_sc, p_ss = p_states
    s_hg, s_ss = s_states
    return (yp, ys.reshape(x_sample.shape), p_ca, p_hg, p_sc, p_ss, _kv_from_rows(kv_rows[0]),
            _kv_from_rows(kv_rows[1]), jnp.stack(s_ca), s_hg, jnp.stack(s_sc), s_ss.reshape(state_ssd.shape))
```

```python
import functools

import numpy as np
import jax
import jax.numpy as jnp
from jax import lax
from jax.experimental import pallas as pl
from jax.experimental.pallas import tpu as pltpu

F32 = jnp.float32
BF16 = jnp.bfloat16

D_MODEL = 1024
D_A = 512
CONV_A_W = 3
D_HG = 512
HG_HEADS = 4
HG_DK = 128
D_SSD = 1024
SSD_P = 64
SSD_HEADS = 16
SSD_GROUPS = 2
SSD_N = 128
SSD_CONV_W = 4
SSD_CONV_DIM = D_SSD + 2 * SSD_GROUPS * SSD_N
N_MEM = 256
XA_HEADS = 4
XA_HD = 256
EPS = 1e-6
KV_SPLIT = XA_HD // 128
KV_SUB = XA_HEADS * KV_SPLIT
KV_ROWS = N_MEM * KV_SUB

OFF_A = 0
OFF_G = 2048
OFF_SZ = 4096
OFF_XBC = 5120
OFF_DT = 6656
D_IN = 6672

CH = 128
HG_LEVELS = (1, 2, 4, 8, 16, 32, 64)
VMEM_LIMIT = 56 * 1024 * 1024


def _rms(x, g):
    ms = jnp.mean(x * x, axis=-1, keepdims=True)
    return x * lax.rsqrt(ms + EPS) * g


def _silu(x):
    return x * (1.0 / (1.0 + jnp.exp(-x)))


def _sigmoid(x):
    return 1.0 / (1.0 + jnp.exp(-x))


def _softplus(x):
    return jnp.maximum(x, 0.0) + jnp.log(1.0 + jnp.exp(-jnp.abs(x)))


def _dot(a, b):
    return jnp.dot(a, b, preferred_element_type=F32)


def _dot_nt(a, b):
    return lax.dot_general(a, b, (((1,), (1,)), ((), ())), preferred_element_type=F32)


def _dot_tn(a, b):
    return lax.dot_general(a, b, (((0,), (0,)), ((), ())), preferred_element_type=F32)


def _split3(x):
    hi = x.astype(BF16)
    r = x - hi.astype(F32)
    mid = r.astype(BF16)
    lo = (r - mid.astype(F32)).astype(BF16)
    return hi, mid, lo


def _split3_rows(x):
    return jnp.concatenate(_split3(x), axis=0)


def _split3_cols(x):
    return jnp.concatenate(_split3(x), axis=1)


@functools.lru_cache(maxsize=None)
def _consts():
    r = np.arange(CH)
    i, t = r[:, None], r[None, :]
    masks = [np.eye(CH, dtype=bool)]
    for s in HG_LEVELS:
        up = ((r // s) % 2 == 1)
        same = (i // (2 * s)) == (t // (2 * s))
        masks.append(same & up[:, None] & (~up)[None, :])
    masks = np.stack(masks).astype(np.float32)
    tril = (t <= i).astype(np.float32)
    tril3 = np.tile(tril, (1, 3))
    triu3 = np.tile(tril.T, (3, 1))
    e = (np.arange(D_SSD)[None, :] // SSD_P == np.arange(SSD_HEADS)[:, None]).astype(np.float32)
    expand3 = np.tile(e, (3, 1))
    return dict(
        masks=jnp.asarray(masks, F32),
        tril=jnp.asarray(tril, F32), tril3=jnp.asarray(tril3, BF16), triu3=jnp.asarray(triu3, BF16),
        expand3=jnp.asarray(expand3, BF16))


def _kv_rows_view(c_all):
    depth, n = c_all.shape[:2]
    c = c_all.reshape(depth, n, N_MEM, XA_HEADS, KV_SPLIT, 128)
    return jnp.transpose(c, (0, 1, 2, 4, 3, 5)).reshape(depth, n, KV_ROWS, 128)


def _kv_from_rows(r_all):
    depth, n = r_all.shape[:2]
    c = r_all.reshape(depth, n, N_MEM, KV_SPLIT, XA_HEADS, 128)
    return jnp.transpose(c, (0, 1, 2, 4, 3, 5)).reshape(depth, n, N_MEM, XA_HEADS, XA_HD)


def _to_kv_rows(x):
    pieces = [x[:, hd * XA_HD + k * 128:hd * XA_HD + (k + 1) * 128] for k in range(KV_SPLIT) for hd in range(XA_HEADS)]
    return jnp.concatenate(pieces, axis=1).reshape(x.shape[0] * KV_SUB, 128)


_ANY = pl.BlockSpec(memory_space=pl.ANY)


def _full_spec(a):
    nd = a.ndim
    return pl.BlockSpec(a.shape, lambda *_, _n=nd: (0,) * _n)


def _layer_spec(a, layer, **kw):
    nd = a.ndim
    return pl.BlockSpec((None,) + tuple(a.shape[1:]), lambda *_, _n=nd: (layer,) + (0,) * (_n - 1), **kw)


def _memkv_kernel(mem_ref, g_ref, wk_ref, wv_ref, *refs, layer):
    kr_ref, vr_ref, kb_ref, vb_ref = refs[-4:]
    m = _rms(mem_ref[0], g_ref[layer:layer + 1, :]).astype(BF16)
    for w_ref, r_ref, b_ref in ((wk_ref, kr_ref, kb_ref), (wv_ref, vr_ref, vb_ref)):
        kv = _dot(m, w_ref[...])
        b_ref[0] = kv.astype(BF16)
        r_ref[...] = _to_kv_rows(kv)


def _memory_kv(mem, wts, layer, depth, prev):
    b = mem.shape[0]
    blk = pl.BlockSpec((1, N_MEM, D_MODEL), lambda i: (i, 0, 0))
    rows_blk = pl.BlockSpec((None, None, KV_ROWS, 128), lambda i: (layer, i, 0, 0))
    rows_sds = jax.ShapeDtypeStruct((depth, b, KV_ROWS, 128), F32)
    extra, extra_specs, aliases = [], [], {}
    if prev is not None:
        extra, extra_specs, aliases = list(prev), [_ANY, _ANY], {4: 0, 5: 1}
    return pl.pallas_call(
        functools.partial(_memkv_kernel, layer=layer),
        grid=(b,),
        in_specs=[blk, _full_spec(wts["gmem"]), _layer_spec(wts["wk"], layer), _layer_spec(wts["wv"], layer)]
        + extra_specs,
        out_specs=[rows_blk, rows_blk, blk, blk],
        out_shape=[rows_sds, rows_sds] + [jax.ShapeDtypeStruct((b, N_MEM, D_MODEL), BF16)] * 2,
        input_output_aliases=aliases,
        compiler_params=pltpu.CompilerParams(dimension_semantics=("arbitrary",), vmem_limit_bytes=VMEM_LIMIT),
        name="memory_kv",
    )(mem, wts["gmem"], wts["wk"], wts["wv"], *extra)


def _hgrn_lower_bound(lb_all, layer):
    depth = lb_all.shape[0]
    rows = [lb_all[j:j + 1, :] for j in range(depth)]
    mx = functools.reduce(jnp.maximum, rows)
    ex = [jnp.exp(rw - mx) for rw in rows]
    tot = functools.reduce(lambda a, b: a + b, ex)
    acc = jnp.zeros_like(tot)
    for j in range(1, layer + 1):
        acc = acc + ex[j]
    return acc / tot


def _hgrn_level(c, f, q, k, s):
    n, w = c.shape
    if s >= 8:
        nb = n // (2 * s)
        c4, q4, k4 = (a.reshape(nb, 2, s, w) for a in (c, q, k))
        lower, upper = c4[:, 0], c4[:, 1]
        tot = lower[:, s - 1:s, :]
        w_lower = k4[:, 0] * jnp.exp(tot - lower)
        w_upper = q4[:, 1] * jnp.exp(upper)
        wv = jnp.stack([w_lower, w_upper], axis=1).reshape(n, w)
        c_next = jnp.stack([lower, upper + tot], axis=1).reshape(n, w)
        return wv, c_next
    sub = lax.broadcasted_iota(jnp.int32, (1, 8, w), 1)
    c3, f3, q3, k3 = (a.reshape(n // 8, 8, w) for a in (c, f, q, k))
    up = (sub // s) % 2 == 1
    tot = None
    for gi in reversed(range(8 // (2 * s))):
        r = gi * 2 * s + s - 1
        tg = jnp.broadcast_to(c3[:, r:r + 1, :], c3.shape)
        tot = tg if tot is None else jnp.where(sub < (gi + 1) * 2 * s, tg, tot)
    if s == 1:
        e = jnp.where(up, f3, 1.0)
    else:
        e = jnp.exp(jnp.where(up, c3, tot - c3))
    wv = jnp.where(up, q3, k3) * e
    c_next = c3 + jnp.where(up, tot, 0.0)
    return wv.reshape(n, w), c_next.reshape(n, w)


def _causal_conv(x, taps, prev_ref):
    n_taps = len(taps)
    row0 = lax.broadcasted_iota(jnp.int32, (8, x.shape[1]), 0) == 0
    prev = [prev_ref[8 - d:8 - d + 1, :] for d in range(1, n_taps)]
    acc = x * taps[0]
    for k in range(1, n_taps):
        carry = functools.reduce(lambda a, b: a + b, [taps[j] * prev[k - j - 1] for j in range(k)])
        rolled = pltpu.roll(acc, 1, axis=0)
        shifted = jnp.concatenate([jnp.where(row0, carry, rolled[0:8]), rolled[8:]], axis=0)
        acc = x * taps[k] + shifted
    return acc


def _cross_attention(q, mk, mv):
    outs = []
    for hd in range(XA_HEADS):
        sl = slice(hd * XA_HD, (hd + 1) * XA_HD)
        s = _dot_nt(q[:, sl].astype(BF16), mk[:, sl]) * (XA_HD ** -0.5)
        s = s - jnp.max(s, axis=-1, keepdims=True)
        e = jnp.exp(s)
        p = e * (1.0 / jnp.sum(e, axis=-1, keepdims=True))
        outs.append(_dot(p.astype(BF16), mv[:, sl]))
    return jnp.concatenate(outs, axis=1)


def _prompt_kernel(x_ref, mk_ref, mv_ref, winT_ref, wout_ref, wq_ref, wo_ref,
                   caw_ref, lb_ref, gn_ref, scw_ref, scb_ref, dtb_ref, dtbc_ref, al_ref, alc_ref, dx_ref,
                   snorm_ref, gpre_ref, gpost_ref, gprex_ref, gpostx_ref,
                   masks_ref, tril_ref, tril3_ref, triu3_ref, expand_ref,
                   *rest, T, layer, n_prev):
    (y_ref, ca_ref, hg_ref, sc_ref, ss_ref,
     bufa, bufc, ug_s, z_s, xbc_s, dt_s, dtT_s, mix_s, sthg, stssd) = rest[n_prev:]
    ti = pl.program_id(1)
    n_chunks = T // CH

    @pl.when(ti == 0)
    def _():
        bufa[0:8, :] = jnp.zeros((8, D_A), F32)
        bufc[0:8, :] = jnp.zeros((8, SSD_CONV_DIM), F32)
        sthg[...] = jnp.zeros(sthg.shape, F32)
        stssd[...] = jnp.zeros(stssd.shape, F32)

    x = x_ref[0]
    row = lambda ref: ref[layer:layer + 1, :]
    h = _rms(x, row(gpre_ref)).astype(BF16)

    sxbc = _dot_nt(h, winT_ref[OFF_XBC:OFF_XBC + SSD_CONV_DIM, :])
    scw = scw_ref[layer]
    xbc = _causal_conv(sxbc, [scw[k:k + 1, :] for k in range(SSD_CONV_W)], bufc) + row(scb_ref)
    xbc_s[...] = _silu(xbc)
    sc_ref[0] = sxbc[T - 3:T, :]
    bufc[0:8, :] = sxbc[T - 8:T, :]
    wdtT = winT_ref[OFF_DT:OFF_DT + SSD_HEADS, :]
    sdt = _dot_nt(h, wdtT)
    dt_s[...] = _softplus(sdt + row(dtb_ref))
    dtT = _softplus(_dot_nt(wdtT, h) + dtbc_ref[:, layer:layer + 1])
    for c in range(n_chunks):
        dtT_s[c] = dtT[:, c * CH:(c + 1) * CH]
    z_s[...] = _dot_nt(h, winT_ref[OFF_SZ:OFF_SZ + D_SSD, :])

    ua = _dot_nt(h, winT_ref[OFF_A:OFF_A + 4 * D_A, :])
    a_h, a_b, a_c, a_z = (ua[:, k * D_A:(k + 1) * D_A] for k in range(4))
    va = a_c * a_h
    caw = caw_ref[layer]
    conv = _causal_conv(va, [caw[k:k + 1, :] for k in range(CONV_A_W)], bufa)
    mix_s[:, 0:D_A] = (a_b * conv * _silu(a_z)).astype(BF16)
    ca_ref[0] = va[T - 2:T, :]
    bufa[0:8, :] = va[T - 8:T, :]

    ug_s[...] = _dot_nt(h, winT_ref[OFF_G:OFF_G + 4 * D_HG, :])


    lb = _hgrn_lower_bound(lb_ref[...], layer)
    a_row = -jnp.exp(row(al_ref))
    a_col = -jnp.exp(alc_ref[:, layer:layer + 1])
    tril = tril_ref[...]
    first_of_pair = lax.broadcasted_iota(jnp.int32, (1, 2 * SSD_P), 1) < SSD_P

    def chunk(c, carry):
        r0 = pl.multiple_of(c * CH, CH)
        rows = pl.ds(r0, CH)

        ug = ug_s[rows, :]
        gq, gf, gi, gz = (ug[:, k * D_HG:(k + 1) * D_HG] for k in range(4))
        f = lb + (1.0 - lb) * _sigmoid(gf)
        logf = jnp.log(f)
        kk = 1.0 - f
        q_b, k_b, v_b = gq.astype(BF16), kk.astype(BF16), gi.astype(BF16)
        hs = [slice(hd * HG_DK, (hd + 1) * HG_DK) for hd in range(HG_HEADS)]
        A = [masks_ref[0] * _dot_nt(q_b[:, s_], k_b[:, s_]) for s_ in hs]
        G = logf
        for li, s in enumerate(HG_LEVELS):
            w, G = _hgrn_level(G, f, gq, kk, s)
            w = w.astype(BF16)
            m = masks_ref[li + 1]
            A = [A[hd] + m * _dot_nt(w[:, hs[hd]], w[:, hs[hd]]) for hd in range(HG_HEADS)]
        g_last = G[CH - 1:CH, :]
        qg = (gq * jnp.exp(G)).astype(BF16)
        kd = (kk * jnp.exp(g_last - G)).astype(BF16)
        dec = jnp.exp(g_last)
        o_heads = []
        for hd in range(HG_HEADS):
            s_ = hs[hd]
            st = sthg[hd]
            o = _dot_nt(qg[:, s_], st.astype(BF16)) + _dot(A[hd].astype(BF16), v_b[:, s_])
            sthg[hd] = st * dec[:, s_] + _dot_tn(v_b[:, s_], kd[:, s_])
            o_heads.append(_rms(o, gn_ref[layer:layer + 1, s_]))
        yb = jnp.concatenate(o_heads, axis=1) * _silu(gz)
        mix_s[rows, D_A:D_A + D_HG] = yb.astype(BF16)

        xbc_c = xbc_s[rows, :]
        xs = xbc_c[:, 0:D_SSD]
        Bm = xbc_c[:, D_SSD:D_SSD + SSD_GROUPS * SSD_N].astype(BF16)
        Cm = xbc_c[:, D_SSD + SSD_GROUPS * SSD_N:].astype(BF16)
        dt = dt_s[rows, :]
        dtT_c = dtT_s[c]
        cs = _dot(tril3_ref[...], _split3_rows(dt * a_row))
        csT = _dot(_split3_cols(dtT_c * a_col), triu3_ref[...])
        cs_last = cs[CH - 1:CH, :]
        w_all = jnp.concatenate([dt * jnp.exp(cs_last - cs), jnp.exp(cs), dt,
                                 jnp.broadcast_to(jnp.exp(cs_last), (8, SSD_HEADS))], axis=0)
        e_all = _dot(_split3_cols(w_all), expand_ref[...])
        e_dec, e_cs, e_dt, e_last = e_all[0:CH], e_all[CH:2 * CH], e_all[2 * CH:3 * CH], e_all[3 * CH:3 * CH + 1]
        xdt = (xs * e_dt).astype(BF16)
        xdec = (xs * e_dec).astype(BF16)
        y_groups = []
        hpg = SSD_HEADS // SSD_GROUPS
        gw = hpg * SSD_P
        for g in range(SSD_GROUPS):
            Cg = Cm[:, g * SSD_N:(g + 1) * SSD_N]
            Bg = Bm[:, g * SSD_N:(g + 1) * SSD_N]
            cb = _dot_nt(Cg, Bg) * tril
            st = stssd[g]
            gcols = slice(g * gw, (g + 1) * gw)
            y_off = _dot(Cg, st.astype(BF16)) * e_cs[:, gcols]
            stssd[g] = st * e_last[:, gcols] + _dot_tn(Bg, xdec[:, gcols])
            pair_out = []
            for pr in range(hpg // 2):
                h0 = g * hpg + 2 * pr
                ms = []
                for hh in (h0, h0 + 1):
                    diff = cs[:, hh:hh + 1] - csT[hh:hh + 1, :]
                    ms.append((cb * jnp.exp(jnp.minimum(diff, 0.0))).astype(BF16))
                both = _dot(jnp.concatenate(ms, axis=0), xdt[:, h0 * SSD_P:(h0 + 2) * SSD_P])
                pair_out.append(jnp.where(first_of_pair, both[0:CH], both[CH:2 * CH]))
            y_groups.append(y_off + jnp.concatenate(pair_out, axis=1))
        y = jnp.concatenate(y_groups, axis=1) + row(dx_ref) * xs
        yc = _rms(y * _silu(z_s[rows, :]), row(snorm_ref))
        mix_s[rows, D_A + D_HG:] = yc.astype(BF16)
        return carry

    lax.fori_loop(0, n_chunks, chunk, 0, unroll=True)

    x1 = x + _rms(_dot(mix_s[...], wout_ref[...]), row(gpost_ref))
    hx = _rms(x1, row(gprex_ref)).astype(BF16)
    q = _dot(hx, wq_ref[...])
    att = _cross_attention(q, mk_ref[0], mv_ref[0])
    y_ref[0] = x1 + _rms(_dot(att.astype(BF16), wo_ref[...]), row(gpostx_ref))

    @pl.when(ti == pl.num_programs(1) - 1)
    def _():
        for hd in range(HG_HEADS):
            hg_ref[0, hd] = sthg[hd].T
        hpg = SSD_HEADS // SSD_GROUPS
        for g in range(SSD_GROUPS):
            sg = stssd[g].T
            for hh in range(hpg):
                ss_ref[0, g * hpg + hh] = sg[hh * SSD_P:(hh + 1) * SSD_P, :]


def _prompt_layer(x, mk, mv, wts, layer, depth, prev, T):
    b, L, _ = x.shape
    prev = [] if prev is None else list(prev)
    c = _consts()
    n_chunks = T // CH
    const_names = ("masks", "tril", "tril3", "triu3", "expand3")
    consts = [c[k] for k in const_names]
    small = [wts[k] for k in ("caw", "lb", "gn", "scw", "scb", "dtb", "dtbc", "al", "alc", "dx", "snorm",
                              "gpre", "gpost", "gprex", "gpostx")]
    big = [wts[k] for k in ("winT", "wout", "wq", "wo")]

    full = lambda a: pl.BlockSpec(a.shape, lambda bi, ti, _n=a.ndim: (0,) * _n, pipeline_mode=pl.Buffered(1))
    big_spec = lambda a: _layer_spec(a, layer, pipeline_mode=pl.Buffered(1))

    in_specs = ([pl.BlockSpec((1, T, D_MODEL), lambda bi, ti: (bi, ti, 0)),
                 pl.BlockSpec((1, N_MEM, D_MODEL), lambda bi, ti: (bi, 0, 0)),
                 pl.BlockSpec((1, N_MEM, D_MODEL), lambda bi, ti: (bi, 0, 0))]
                + [big_spec(a) for a in big] + [full(a) for a in small] + [full(a) for a in consts]
                + [_ANY] * len(prev))
    n_in = len(in_specs)
    state_shapes = [(CONV_A_W - 1, D_A), (HG_HEADS, HG_DK, HG_DK), (SSD_CONV_W - 1, SSD_CONV_DIM),
                    (SSD_HEADS, SSD_P, SSD_N)]
    out_shape = ([jax.ShapeDtypeStruct((b, L, D_MODEL), F32)]
                 + [jax.ShapeDtypeStruct((depth, b) + s, F32) for s in state_shapes])
    out_specs = ([pl.BlockSpec((1, T, D_MODEL), lambda bi, ti: (bi, ti, 0))]
                 + [pl.BlockSpec((None, 1) + s, lambda bi, ti, _n=len(s): (layer, bi) + (0,) * _n)
                    for s in state_shapes])
    aliases = {n_in - len(prev) + k: 1 + k for k in range(len(prev))}
    scratch = [pltpu.VMEM((8, D_A), F32), pltpu.VMEM((8, SSD_CONV_DIM), F32),
               pltpu.VMEM((T, 4 * D_HG), F32), pltpu.VMEM((T, D_SSD), F32), pltpu.VMEM((T, SSD_CONV_DIM), F32),
               pltpu.VMEM((T, SSD_HEADS), F32), pltpu.VMEM((n_chunks, SSD_HEADS, CH), F32),
               pltpu.VMEM((T, 2 * D_MODEL), BF16),
               pltpu.VMEM((HG_HEADS, HG_DK, HG_DK), F32),
               pltpu.VMEM((SSD_GROUPS, SSD_N, (SSD_HEADS // SSD_GROUPS) * SSD_P), F32)]
    return pl.pallas_call(
        functools.partial(_prompt_kernel, T=T, layer=layer, n_prev=len(prev)),
        grid=(b, L // T),
        in_specs=in_specs, out_specs=out_specs, out_shape=out_shape, scratch_shapes=scratch,
        input_output_aliases=aliases,
        compiler_params=pltpu.CompilerParams(dimension_semantics=("arbitrary", "arbitrary"),
                                             vmem_limit_bytes=VMEM_LIMIT),
        name=f"prompt_layer{layer}",
    )(x, mk, mv, *big, *small, *consts, *prev)


def _prep_weights(w_in, conv_a_w, hgrn_lb, hgrn_gnorm, ssd_conv_w, ssd_conv_b, ssd_dt_bias, ssd_A_log, ssd_D, ssd_norm,
                  w_out, g_pre_mix, g_post_mix, g_pre_x, g_post_x, g_mem, w_q, w_k, w_v, w_o):
    return dict(
        winT=jnp.transpose(w_in, (0, 2, 1)).astype(BF16),
        wout=w_out.astype(BF16), wq=w_q.astype(BF16), wo=w_o.astype(BF16), wk=w_k.astype(BF16), wv=w_v.astype(BF16),
        caw=conv_a_w, lb=hgrn_lb, gn=hgrn_gnorm, scw=ssd_conv_w, scb=ssd_conv_b,
        dtb=ssd_dt_bias, dtbc=ssd_dt_bias.T, al=ssd_A_log, alc=ssd_A_log.T,
        dx=jnp.repeat(ssd_D, SSD_P, axis=1), snorm=ssd_norm,
        gpre=g_pre_mix, gpost=g_post_mix, gprex=g_pre_x, gpostx=g_post_x, gmem=g_mem)


SB = 8
D_HGP = 4 * D_HG
D_SSP = 4 * D_SSD + 2 * SSD_GROUPS * SSD_N


def _sample_pre_kernel(x_ref, ca_ref, sc_ref, winT_ref, caw_ref, lb_ref, scw_ref, scb_ref, dtb_ref, al_ref,
                       gpre_ref, expand_ref,
                       *rest, layer):
    ya_ref, canew_ref, hgp_ref, ssp_ref, scnew_ref = rest[-5:]
    row = lambda ref: ref[layer:layer + 1, :]
    h = _rms(x_ref[...], row(gpre_ref)).astype(BF16)
    u = _dot_nt(h, winT_ref[...])
    a_h, a_b, a_c, a_z = (u[:, OFF_A + k * D_A:OFF_A + (k + 1) * D_A] for k in range(4))
    va = a_c * a_h
    p0, p1 = ca_ref[:, 0:D_A], ca_ref[:, D_A:2 * D_A]
    caw = caw_ref[layer]
    conv = va * caw[2:3, :] + p1 * caw[1:2, :] + p0 * caw[0:1, :]
    ya_ref[...] = a_b * conv * _silu(a_z)
    canew_ref[:, 0:D_A] = p1
    canew_ref[:, D_A:2 * D_A] = va
    lb = _hgrn_lower_bound(lb_ref[...], layer)
    gq, gf, gi, gz = (u[:, OFF_G + k * D_HG:OFF_G + (k + 1) * D_HG] for k in range(4))
    hgp_ref[:, 0:D_HG] = gq
    hgp_ref[:, D_HG:2 * D_HG] = lb + (1.0 - lb) * _sigmoid(gf)
    hgp_ref[:, 2 * D_HG:3 * D_HG] = gi
    hgp_ref[:, 3 * D_HG:] = gz
    sxbc = u[:, OFF_XBC:OFF_XBC + SSD_CONV_DIM]
    q0, q1, q2 = sc_ref[0], sc_ref[1], sc_ref[2]
    scw = scw_ref[layer]
    xbc = _silu(sxbc * scw[3:4, :] + q2 * scw[2:3, :] + q1 * scw[1:2, :] + q0 * scw[0:1, :] + row(scb_ref))
    scnew_ref[0] = q1
    scnew_ref[1] = q2
    scnew_ref[2] = sxbc
    xs = xbc[:, 0:D_SSD]
    dt = _softplus(u[:, OFF_DT:OFF_DT + SSD_HEADS] + row(dtb_ref))
    dec = jnp.exp(dt * -jnp.exp(row(al_ref)))
    n = dt.shape[0]
    e_all = _dot(_split3_cols(jnp.concatenate([dt, dec], axis=0)), expand_ref[...])
    ssp_ref[:, 0:D_SSD] = xs
    ssp_ref[:, D_SSD:2 * D_SSD] = xs * e_all[0:n]
    ssp_ref[:, 2 * D_SSD:3 * D_SSD] = e_all[n:2 * n]
    ssp_ref[:, 3 * D_SSD:4 * D_SSD] = u[:, OFF_SZ:OFF_SZ + D_SSD]
    ssp_ref[:, 4 * D_SSD:] = xbc[:, D_SSD:]


def _sample_pre(x, ca, sc_all, wts, layer, prev_sc):
    n = x.shape[0]
    prev = [] if prev_sc is None else [prev_sc]
    args = [x, ca, sc_all, wts["winT"], wts["caw"], wts["lb"], wts["scw"], wts["scb"], wts["dtb"], wts["al"],
            wts["gpre"], _consts()["expand3"]]
    out_shape = [jax.ShapeDtypeStruct((n, D_A), F32), jax.ShapeDtypeStruct((n, 2 * D_A), F32),
                 jax.ShapeDtypeStruct((n, D_HGP), F32), jax.ShapeDtypeStruct((n, D_SSP), F32),
                 jax.ShapeDtypeStruct(sc_all.shape, F32)]
    return pl.pallas_call(
        functools.partial(_sample_pre_kernel, layer=layer),
        in_specs=[_full_spec(a) for a in args[:2]] + [_layer_spec(args[2], layer), _layer_spec(args[3], layer)]
        + [_full_spec(a) for a in args[4:]] + [_ANY] * len(prev),
        out_specs=[_full_spec(s) for s in out_shape[:4]] + [_layer_spec(out_shape[4], layer)],
        out_shape=out_shape, grid=(1,),
        input_output_aliases={len(args) + k: 4 + k for k in range(len(prev))},
        compiler_params=pltpu.CompilerParams(dimension_semantics=("arbitrary",), vmem_limit_bytes=VMEM_LIMIT),
        name=f"sample_pre{layer}",
    )(*args, *prev)


def _pad_rows_T(blk):
    w = blk.shape[1]
    return jnp.concatenate([blk, jnp.zeros((128 - blk.shape[0], w), blk.dtype)], axis=0).T


def _sample_state_kernel(hgp_ref, ssp_ref, shg_ref, sss_ref, *rest):
    o_ref, y_ref, shg_out, sss_out = rest[-4:]
    rid_hg = lax.broadcasted_iota(jnp.int32, (SB, HG_DK), 0)
    for hd in range(HG_HEADS):
        cols = slice(hd * HG_DK, (hd + 1) * HG_DK)
        q_b = hgp_ref[:, cols].astype(BF16)
        fT = _pad_rows_T(hgp_ref[:, D_HG + hd * HG_DK:D_HG + (hd + 1) * HG_DK])
        o = jnp.zeros((SB, HG_DK), F32)
        for j in range(SB):
            fcol = fT[:, j:j + 1]
            vrow = hgp_ref[j:j + 1, 2 * D_HG + hd * HG_DK:2 * D_HG + (hd + 1) * HG_DK]
            s_new = vrow + fcol * (shg_ref[j, hd] - vrow)
            shg_out[j, hd] = s_new
            o = jnp.where(rid_hg == j, _dot(q_b, s_new.astype(BF16)), o)
        o_ref[:, cols] = o
    gw = (SSD_HEADS // SSD_GROUPS) * SSD_P
    rid_ss = lax.broadcasted_iota(jnp.int32, (SB, gw), 0)
    xdtT = _pad_rows_T(ssp_ref[:, D_SSD:2 * D_SSD])
    decT = _pad_rows_T(ssp_ref[:, 2 * D_SSD:3 * D_SSD])
    for g in range(SSD_GROUPS):
        rows = slice(g * gw, (g + 1) * gw)
        c_b = ssp_ref[:, 4 * D_SSD + (SSD_GROUPS + g) * SSD_N:4 * D_SSD + (SSD_GROUPS + g + 1) * SSD_N].astype(BF16)
        y = jnp.zeros((SB, gw), F32)
        for j in range(SB):
            brow = ssp_ref[j:j + 1, 4 * D_SSD + g * SSD_N:4 * D_SSD + (g + 1) * SSD_N]
            decayed = []
            for hh in range(SSD_HEADS // SSD_GROUPS):
                r0 = g * gw + hh * SSD_P
                dec_h = jnp.broadcast_to(decT[r0:r0 + 8, j:j + 1], (8, SSD_N))
                s_old = sss_ref[j, r0:r0 + SSD_P, :].reshape(SSD_P // 8, 8, SSD_N)
                decayed.append((s_old * dec_h[None]).reshape(SSD_P, SSD_N))
            s_new = jnp.concatenate(decayed, axis=0) + xdtT[rows, j:j + 1] * brow
            sss_out[j, rows, :] = s_new
            y = jnp.where(rid_ss == j, _dot_nt(c_b, s_new.astype(BF16)), y)
        y_ref[:, rows] = y


def _sample_state(hgp, ssp, shg_all, sss_all, layer, prev):
    n = hgp.shape[0]
    prev = [] if prev is None else list(prev)
    rowblk = lambda w: pl.BlockSpec((SB, w), lambda i: (i, 0))
    hg_blk = pl.BlockSpec((None, SB, HG_HEADS, HG_DK, HG_DK), lambda i: (layer, i, 0, 0, 0))
    ss_blk = pl.BlockSpec((None, SB, SSD_HEADS * SSD_P, SSD_N), lambda i: (layer, i, 0, 0))
    return pl.pallas_call(
        _sample_state_kernel,
        grid=(n // SB,),
        in_specs=[rowblk(D_HGP), rowblk(D_SSP), hg_blk, ss_blk] + [_ANY] * len(prev),
        out_specs=[rowblk(D_HG), rowblk(D_SSD), hg_blk, ss_blk],
        out_shape=[jax.ShapeDtypeStruct((n, D_HG), F32), jax.ShapeDtypeStruct((n, D_SSD), F32),
                   jax.ShapeDtypeStruct(shg_all.shape, F32), jax.ShapeDtypeStruct(sss_all.shape, F32)],
        input_output_aliases={4 + k: 2 + k for k in range(len(prev))},
        compiler_params=pltpu.CompilerParams(dimension_semantics=("arbitrary",), vmem_limit_bytes=VMEM_LIMIT),
        name="sample_state",
    )(hgp, ssp, shg_all, sss_all, *prev)


def _sample_mid_kernel(x_ref, ya_ref, o_ref, y_ref, hgp_ref, ssp_ref, wout_ref, wq_ref, gn_ref, dx_ref, snorm_ref,
                       gpost_ref, gprex_ref, x1_ref, q_ref, *, layer):
    row = lambda ref: ref[layer:layer + 1, :]
    gz = hgp_ref[:, 3 * D_HG:]
    o = o_ref[...]
    yb = jnp.concatenate([_rms(o[:, hd * HG_DK:(hd + 1) * HG_DK], gn_ref[layer:layer + 1, hd * HG_DK:(hd + 1) * HG_DK])
                          for hd in range(HG_HEADS)], axis=1) * _silu(gz)
    y = y_ref[...] + row(dx_ref) * ssp_ref[:, 0:D_SSD]
    yc = _rms(y * _silu(ssp_ref[:, 3 * D_SSD:4 * D_SSD]), row(snorm_ref))
    mix = jnp.concatenate([ya_ref[...], yb, yc], axis=1).astype(BF16)
    x1 = x_ref[...] + _rms(_dot(mix, wout_ref[...]), row(gpost_ref))
    x1_ref[...] = x1
    q = _dot(_rms(x1, row(gprex_ref)).astype(BF16), wq_ref[...])
    q_ref[...] = _to_kv_rows(q)


def _sample_mid(x, ya, o, y, hgp, ssp, wts, layer):
    n = x.shape[0]
    args = [x, ya, o, y, hgp, ssp, wts["wout"], wts["wq"], wts["gn"], wts["dx"], wts["snorm"], wts["gpost"],
            wts["gprex"]]
    out_shape = [jax.ShapeDtypeStruct((n, D_MODEL), F32),
                 jax.ShapeDtypeStruct((n * KV_SUB, 128), F32)]
    return pl.pallas_call(
        functools.partial(_sample_mid_kernel, layer=layer), grid=(1,),
        in_specs=[_full_spec(a) for a in args[:6]] + [_layer_spec(a, layer) for a in args[6:8]]
        + [_full_spec(a) for a in args[8:]], out_specs=[_full_spec(s) for s in out_shape], out_shape=out_shape,
        compiler_params=pltpu.CompilerParams(dimension_semantics=("arbitrary",), vmem_limit_bytes=VMEM_LIMIT),
        name="sample_mid",
    )(*args)


def _lane_class_reduce(x, op):
    sh = KV_SUB
    while sh < 128:
        x = op(x, pltpu.roll(x, sh, axis=1))
        sh *= 2
    return x


def _sample_attn_kernel(x1_ref, q_ref, k_ref, v_ref, wo_ref, gpostx_ref, x2_ref, *, layer):
    lane = lax.broadcasted_iota(jnp.int32, (KV_SUB, KV_ROWS), 1)
    sub = lax.broadcasted_iota(jnp.int32, (KV_SUB, KV_ROWS), 0)
    own = ((lane & (KV_SUB - 1)) == sub).astype(F32)
    rid = lax.broadcasted_iota(jnp.int32, (SB, KV_ROWS), 0)
    t_all = jnp.zeros((SB, KV_ROWS), F32)
    for j in range(SB):
        r = _dot_nt(q_ref[j].astype(BF16), k_ref[j].astype(BF16))
        t = jnp.sum(r * own, axis=0, keepdims=True)
        t_all = jnp.where(rid == j, t, t_all)
    n_tiles = KV_ROWS // 128
    lane1 = lax.broadcasted_iota(jnp.int32, (SB, 128), 1)
    piece = (lane1 // XA_HEADS) % KV_SPLIT
    chunks = []
    for c in range(n_tiles):
        x = t_all[:, c * 128:(c + 1) * 128]
        tot = x
        for k in range(1, KV_SPLIT):
            fwd = pltpu.roll(x, 128 - k * XA_HEADS, axis=1)
            bwd = pltpu.roll(x, (KV_SPLIT - k) * XA_HEADS, axis=1)
            tot = tot + jnp.where(piece + k < KV_SPLIT, fwd, bwd)
        chunks.append(tot * (XA_HD ** -0.5))
    mx = _lane_class_reduce(functools.reduce(jnp.maximum, chunks), jnp.maximum)
    es = [jnp.exp(ch - mx) for ch in chunks]
    den = _lane_class_reduce(functools.reduce(lambda a, b: a + b, es), lambda a, b: a + b)
    p_all = jnp.concatenate([e * (1.0 / den) for e in es], axis=1)
    rid_o = lax.broadcasted_iota(jnp.int32, (SB, D_MODEL), 0)
    att = jnp.zeros((SB, D_MODEL), F32)
    for j in range(SB):
        p8 = (own * p_all[j:j + 1, :]).astype(BF16)
        o = _dot(p8, v_ref[j].astype(BF16))
        row = jnp.concatenate([o[k * XA_HEADS + hd:k * XA_HEADS + hd + 1, :]
                               for hd in range(XA_HEADS) for k in range(KV_SPLIT)], axis=1)
        att = jnp.where(rid_o == j, row, att)
    x2_ref[...] = x1_ref[...] + _rms(_dot(att.astype(BF16), wo_ref[...]), gpostx_ref[layer:layer + 1, :])


def _sample_attn(x1, q8, ck_rows, cv_rows, wts, layer):
    n = x1.shape[0]
    rowblk = pl.BlockSpec((SB, D_MODEL), lambda i: (i, 0))
    qblk = pl.BlockSpec((SB, KV_SUB, 128), lambda i: (i, 0, 0))
    kvblk = pl.BlockSpec((None, SB, KV_ROWS, 128), lambda i: (layer, i, 0, 0))
    return pl.pallas_call(
        functools.partial(_sample_attn_kernel, layer=layer),
        grid=(n // SB,),
        in_specs=[rowblk, qblk, kvblk, kvblk, _layer_spec(wts["wo"], layer), _full_spec(wts["gpostx"])],
        out_specs=rowblk,
        out_shape=jax.ShapeDtypeStruct((n, D_MODEL), F32),
        compiler_params=pltpu.CompilerParams(dimension_semantics=("arbitrary",), vmem_limit_bytes=VMEM_LIMIT),
        name="sample_attn",
    )(x1, q8, ck_rows, cv_rows, wts["wo"], wts["gpostx"])


def _sample_layer(x, ca, shg_all, sc_all, sss_all, ck_rows, cv_rows, wts, layer, prev):
    n = x.shape[0]
    prev_sc, prev_states = (None, None) if prev is None else (prev[0], prev[1:])
    ya, ca_new, hgp, ssp, sc_new = _sample_pre(x, ca.reshape(n, -1), sc_all, wts, layer, prev_sc)
    o, y, shg_new, sss_new = _sample_state(hgp, ssp, shg_all, sss_all, layer, prev_states)
    x1, q8 = _sample_mid(x, ya, o, y, hgp, ssp, wts, layer)
    x2 = _sample_attn(x1, q8.reshape(n, KV_SUB, 128), ck_rows, cv_rows, wts, layer)
    return x2, ca_new.reshape(ca.shape), (sc_new, shg_new, sss_new)


PROMPT_TILE = 512


def kernel(x_prompt, x_sample, mem_prompt, state_conv_a, state_hgrn, state_ssd_conv, state_ssd, cache_mem_k,
           cache_mem_v, w_in, conv_a_w, hgrn_lb, hgrn_gnorm, ssd_conv_w, ssd_conv_b, ssd_dt_bias, ssd_A_log, ssd_D,
           ssd_norm, w_out, g_pre_mix, g_post_mix, g_pre_x, g_post_x, g_mem, w_q, w_k, w_v, w_o):
    depth = w_in.shape[0]
    n = x_sample.shape[0]
    yp = x_prompt
    ys = x_sample.reshape(n, D_MODEL)
    ck_rows, cv_rows = _kv_rows_view(cache_mem_k), _kv_rows_view(cache_mem_v)
    sss_all = state_ssd.reshape(depth, n, SSD_HEADS * SSD_P, SSD_N)
    sc_all = jnp.transpose(state_ssd_conv, (0, 2, 1, 3))
    kv_rows = p_states = s_states = None
    s_ca = []
    wts = _prep_weights(w_in, conv_a_w, hgrn_lb, hgrn_gnorm, ssd_conv_w, ssd_conv_b, ssd_dt_bias, ssd_A_log, ssd_D,
                        ssd_norm, w_out, g_pre_mix, g_post_mix, g_pre_x, g_post_x, g_mem, w_q, w_k, w_v, w_o)
    for l in range(depth):
        *kv_rows, mk, mv = _memory_kv(mem_prompt, wts, l, depth, kv_rows)
        yp, *p_states = _prompt_layer(yp, mk, mv, wts, l, depth, p_states, PROMPT_TILE)
        ys, ca, s_states = _sample_layer(ys, state_conv_a[l], state_hgrn, sc_all, sss_all,
                                         ck_rows, cv_rows, wts, l, s_states)
        s_ca.append(ca)
    p_ca, p_hg, p_sc, p_ss = p_states
    s_sc, s_hg, s_ss = s_states
    return (yp, ys.reshape(x_sample.shape), p_ca, p_hg, p_sc, p_ss, _kv_from_rows(kv_rows[0]),
            _kv_from_rows(kv_rows[1]), jnp.stack(s_ca), s_hg, jnp.transpose(s_sc, (0, 2, 1, 3)),
            s_ss.reshape(state_ssd.shape))
```

```python
import functools

import numpy as np
import jax
import jax.numpy as jnp
from jax import lax
from jax.experimental import pallas as pl
from jax.experimental.pallas import tpu as pltpu

F32 = jnp.float32
BF16 = jnp.bfloat16

D_MODEL = 1024
D_A = 512
CONV_A_W = 3
D_HG = 512
HG_HEADS = 4
HG_DK = 128
D_SSD = 1024
SSD_P = 64
SSD_HEADS = 16
SSD_GROUPS = 2
SSD_N = 128
SSD_CONV_W = 4
SSD_CONV_DIM = D_SSD + 2 * SSD_GROUPS * SSD_N
N_MEM = 256
XA_HEADS = 4
XA_HD = 256
EPS = 1e-6
KV_SPLIT = XA_HD // 128
KV_SUB = XA_HEADS * KV_SPLIT
KV_ROWS = N_MEM * KV_SUB

OFF_A = 0
OFF_G = 2048
OFF_SZ = 4096
OFF_XBC = 5120
OFF_DT = 6656
D_IN = 6672

CH = 128
HG_LEVELS = (1, 2, 4, 8, 16, 32, 64)
VMEM_LIMIT = 56 * 1024 * 1024


def _rms(x, g):
    ms = jnp.mean(x * x, axis=-1, keepdims=True)
    return x * lax.rsqrt(ms + EPS) * g


def _silu(x):
    return x * (1.0 / (1.0 + jnp.exp(-x)))


def _sigmoid(x):
    return 1.0 / (1.0 + jnp.exp(-x))


def _softplus(x):
    return jnp.maximum(x, 0.0) + jnp.log(1.0 + jnp.exp(-jnp.abs(x)))


def _dot(a, b):
    return jnp.dot(a, b, preferred_element_type=F32)


def _dot_nt(a, b):
    return lax.dot_general(a, b, (((1,), (1,)), ((), ())), preferred_element_type=F32)


def _dot_tn(a, b):
    return lax.dot_general(a, b, (((0,), (0,)), ((), ())), preferred_element_type=F32)


def _split3(x):
    hi = x.astype(BF16)
    r = x - hi.astype(F32)
    mid = r.astype(BF16)
    lo = (r - mid.astype(F32)).astype(BF16)
    return hi, mid, lo


def _split3_rows(x):
    return jnp.concatenate(_split3(x), axis=0)


def _split3_cols(x):
    return jnp.concatenate(_split3(x), axis=1)


@functools.lru_cache(maxsize=None)
def _consts():
    r = np.arange(CH)
    i, t = r[:, None], r[None, :]
    masks = [np.eye(CH, dtype=bool)]
    for s in HG_LEVELS:
        up = ((r // s) % 2 == 1)
        same = (i // (2 * s)) == (t // (2 * s))
        masks.append(same & up[:, None] & (~up)[None, :])
    masks = np.stack(masks).astype(np.float32)
    tril = (t <= i).astype(np.float32)
    tril3 = np.tile(tril, (1, 3))
    triu3 = np.tile(tril.T, (3, 1))
    e = (np.arange(D_SSD)[None, :] // SSD_P == np.arange(SSD_HEADS)[:, None]).astype(np.float32)
    expand3 = np.tile(e, (3, 1))
    return dict(
        masks=jnp.asarray(masks, F32),
        tril=jnp.asarray(tril, F32), tril3=jnp.asarray(tril3, BF16), triu3=jnp.asarray(triu3, BF16),
        expand3=jnp.asarray(expand3, BF16))


def _kv_rows_view(c_all):
    depth, n = c_all.shape[:2]
    c = c_all.reshape(depth, n, N_MEM, XA_HEADS, KV_SPLIT, 128)
    return jnp.transpose(c, (0, 1, 2, 4, 3, 5)).reshape(depth, n, KV_ROWS, 128)


def _kv_from_rows(r_all):
    depth, n = r_all.shape[:2]
    c = r_all.reshape(depth, n, N_MEM, KV_SPLIT, XA_HEADS, 128)
    return jnp.transpose(c, (0, 1, 2, 4, 3, 5)).reshape(depth, n, N_MEM, XA_HEADS, XA_HD)


def _to_kv_rows(x):
    pieces = [x[:, hd * XA_HD + k * 128:hd * XA_HD + (k + 1) * 128] for k in range(KV_SPLIT) for hd in range(XA_HEADS)]
    return jnp.concatenate(pieces, axis=1).reshape(x.shape[0] * KV_SUB, 128)


_ANY = pl.BlockSpec(memory_space=pl.ANY)


def _full_spec(a):
    nd = a.ndim
    return pl.BlockSpec(a.shape, lambda *_, _n=nd: (0,) * _n)


def _layer_spec(a, layer, **kw):
    nd = a.ndim
    return pl.BlockSpec((None,) + tuple(a.shape[1:]), lambda *_, _n=nd: (layer,) + (0,) * (_n - 1), **kw)


def _memkv_kernel(mem_ref, g_ref, wk_ref, wv_ref, *refs, layer):
    kr_ref, vr_ref, kb_ref, vb_ref = refs[-4:]
    m = _rms(mem_ref[0], g_ref[layer:layer + 1, :]).astype(BF16)
    for w_ref, r_ref, b_ref in ((wk_ref, kr_ref, kb_ref), (wv_ref, vr_ref, vb_ref)):
        kv = _dot(m, w_ref[...])
        b_ref[0] = kv.astype(BF16)
        r_ref[...] = _to_kv_rows(kv)


def _memory_kv(mem, wts, layer, depth, prev):
    b = mem.shape[0]
    blk = pl.BlockSpec((1, N_MEM, D_MODEL), lambda i: (i, 0, 0))
    rows_blk = pl.BlockSpec((None, None, KV_ROWS, 128), lambda i: (layer, i, 0, 0))
    rows_sds = jax.ShapeDtypeStruct((depth, b, KV_ROWS, 128), F32)
    extra, extra_specs, aliases = [], [], {}
    if prev is not None:
        extra, extra_specs, aliases = list(prev), [_ANY, _ANY], {4: 0, 5: 1}
    return pl.pallas_call(
        functools.partial(_memkv_kernel, layer=layer),
        grid=(b,),
        in_specs=[blk, _full_spec(wts["gmem"]), _layer_spec(wts["wk"], layer), _layer_spec(wts["wv"], layer)]
        + extra_specs,
        out_specs=[rows_blk, rows_blk, blk, blk],
        out_shape=[rows_sds, rows_sds] + [jax.ShapeDtypeStruct((b, N_MEM, D_MODEL), BF16)] * 2,
        input_output_aliases=aliases,
        compiler_params=pltpu.CompilerParams(dimension_semantics=("arbitrary",), vmem_limit_bytes=VMEM_LIMIT),
        name="memory_kv",
    )(mem, wts["gmem"], wts["wk"], wts["wv"], *extra)


def _hgrn_lower_bound(lb_all, layer):
    depth = lb_all.shape[0]
    rows = [lb_all[j:j + 1, :] for j in range(depth)]
    mx = functools.reduce(jnp.maximum, rows)
    ex = [jnp.exp(rw - mx) for rw in rows]
    tot = functools.reduce(lambda a, b: a + b, ex)
    acc = jnp.zeros_like(tot)
    for j in range(1, layer + 1):
        acc = acc + ex[j]
    return acc / tot


def _hgrn_level(c, f, q, k, s):
    n, w = c.shape
    if s >= 8:
        nb = n // (2 * s)
        c4, q4, k4 = (a.reshape(nb, 2, s, w) for a in (c, q, k))
        lower, upper = c4[:, 0], c4[:, 1]
        tot = lower[:, s - 1:s, :]
        w_lower = k4[:, 0] * jnp.exp(tot - lower)
        w_upper = q4[:, 1] * jnp.exp(upper)
        wv = jnp.stack([w_lower, w_upper], axis=1).reshape(n, w)
        c_next = jnp.stack([lower, upper + tot], axis=1).reshape(n, w)
        return wv, c_next
    sub = lax.broadcasted_iota(jnp.int32, (1, 8, w), 1)
    c3, f3, q3, k3 = (a.reshape(n // 8, 8, w) for a in (c, f, q, k))
    up = (sub // s) % 2 == 1
    tot = None
    for gi in reversed(range(8 // (2 * s))):
        r = gi * 2 * s + s - 1
        tg = jnp.broadcast_to(c3[:, r:r + 1, :], c3.shape)
        tot = tg if tot is None else jnp.where(sub < (gi + 1) * 2 * s, tg, tot)
    if s == 1:
        e = jnp.where(up, f3, 1.0)
    else:
        e = jnp.exp(jnp.where(up, c3, tot - c3))
    wv = jnp.where(up, q3, k3) * e
    c_next = c3 + jnp.where(up, tot, 0.0)
    return wv.reshape(n, w), c_next.reshape(n, w)


def _causal_conv(x, taps, prev_ref):
    n_taps = len(taps)
    row0 = lax.broadcasted_iota(jnp.int32, (8, x.shape[1]), 0) == 0
    prev = [prev_ref[8 - d:8 - d + 1, :] for d in range(1, n_taps)]
    acc = x * taps[0]
    for k in range(1, n_taps):
        carry = functools.reduce(lambda a, b: a + b, [taps[j] * prev[k - j - 1] for j in range(k)])
        rolled = pltpu.roll(acc, 1, axis=0)
        shifted = jnp.concatenate([jnp.where(row0, carry, rolled[0:8]), rolled[8:]], axis=0)
        acc = x * taps[k] + shifted
    return acc


def _cross_attention(q, mk, mv):
    outs = []
    for hd in range(XA_HEADS):
        sl = slice(hd * XA_HD, (hd + 1) * XA_HD)
        s = _dot_nt(q[:, sl].astype(BF16), mk[:, sl]) * (XA_HD ** -0.5)
        s = s - jnp.max(s, axis=-1, keepdims=True)
        e = jnp.exp(s)
        p = e * (1.0 / jnp.sum(e, axis=-1, keepdims=True))
        outs.append(_dot(p.astype(BF16), mv[:, sl]))
    return jnp.concatenate(outs, axis=1)


def _prompt_kernel(x_ref, mk_ref, mv_ref, winT_ref, wout_ref, wq_ref, wo_ref,
                   caw_ref, lb_ref, gn_ref, scw_ref, scb_ref, dtb_ref, dtbc_ref, al_ref, alc_ref, dx_ref,
                   snorm_ref, gpre_ref, gpost_ref, gprex_ref, gpostx_ref,
                   masks_ref, tril_ref, tril3_ref, triu3_ref, expand_ref,
                   *rest, T, layer, n_prev):
    (y_ref, ca_ref, hg_ref, sc_ref, ss_ref,
     bufa, bufc, ug_s, z_s, xbc_s, dt_s, dtT_s, mix_s, sthg, stssd) = rest[n_prev:]
    ti = pl.program_id(1)
    n_chunks = T // CH

    @pl.when(ti == 0)
    def _():
        bufa[0:8, :] = jnp.zeros((8, D_A), F32)
        bufc[0:8, :] = jnp.zeros((8, SSD_CONV_DIM), F32)
        sthg[...] = jnp.zeros(sthg.shape, F32)
        stssd[...] = jnp.zeros(stssd.shape, F32)

    x = x_ref[0]
    row = lambda ref: ref[layer:layer + 1, :]
    h = _rms(x, row(gpre_ref)).astype(BF16)

    sxbc = _dot_nt(h, winT_ref[OFF_XBC:OFF_XBC + SSD_CONV_DIM, :])
    scw = scw_ref[layer]
    xbc = _causal_conv(sxbc, [scw[k:k + 1, :] for k in range(SSD_CONV_W)], bufc) + row(scb_ref)
    xbc_s[...] = _silu(xbc)
    sc_ref[0] = sxbc[T - 3:T, :]
    bufc[0:8, :] = sxbc[T - 8:T, :]
    wdtT = winT_ref[OFF_DT:OFF_DT + SSD_HEADS, :]
    sdt = _dot_nt(h, wdtT)
    dt_s[...] = _softplus(sdt + row(dtb_ref))
    dtT = _softplus(_dot_nt(wdtT, h) + dtbc_ref[:, layer:layer + 1])
    for c in range(n_chunks):
        dtT_s[c] = dtT[:, c * CH:(c + 1) * CH]
    z_s[...] = _dot_nt(h, winT_ref[OFF_SZ:OFF_SZ + D_SSD, :])

    ua = _dot_nt(h, winT_ref[OFF_A:OFF_A + 4 * D_A, :])
    a_h, a_b, a_c, a_z = (ua[:, k * D_A:(k + 1) * D_A] for k in range(4))
    va = a_c * a_h
    caw = caw_ref[layer]
    conv = _causal_conv(va, [caw[k:k + 1, :] for k in range(CONV_A_W)], bufa)
    mix_s[:, 0:D_A] = (a_b * conv * _silu(a_z)).astype(BF16)
    ca_ref[0] = va[T - 2:T, :]
    bufa[0:8, :] = va[T - 8:T, :]

    ug_s[...] = _dot_nt(h, winT_ref[OFF_G:OFF_G + 4 * D_HG, :])


    lb = _hgrn_lower_bound(lb_ref[...], layer)
    a_row = -jnp.exp(row(al_ref))
    a_col = -jnp.exp(alc_ref[:, layer:layer + 1])
    tril = tril_ref[...]
    first_of_pair = lax.broadcasted_iota(jnp.int32, (1, 2 * SSD_P), 1) < SSD_P

    def chunk(c, carry):
        r0 = pl.multiple_of(c * CH, CH)
        rows = pl.ds(r0, CH)

        ug = ug_s[rows, :]
        gq, gf, gi, gz = (ug[:, k * D_HG:(k + 1) * D_HG] for k in range(4))
        f = lb + (1.0 - lb) * _sigmoid(gf)
        logf = jnp.log(f)
        kk = 1.0 - f
        q_b, k_b, v_b = gq.astype(BF16), kk.astype(BF16), gi.astype(BF16)
        hs = [slice(hd * HG_DK, (hd + 1) * HG_DK) for hd in range(HG_HEADS)]
        A = [masks_ref[0] * _dot_nt(q_b[:, s_], k_b[:, s_]) for s_ in hs]
        G = logf
        for li, s in enumerate(HG_LEVELS):
            w, G = _hgrn_level(G, f, gq, kk, s)
            w = w.astype(BF16)
            m = masks_ref[li + 1]
            A = [A[hd] + m * _dot_nt(w[:, hs[hd]], w[:, hs[hd]]) for hd in range(HG_HEADS)]
        g_last = G[CH - 1:CH, :]
        qg = (gq * jnp.exp(G)).astype(BF16)
        kd = (kk * jnp.exp(g_last - G)).astype(BF16)
        dec = jnp.exp(g_last)
        o_heads = []
        for hd in range(HG_HEADS):
            s_ = hs[hd]
            st = sthg[hd]
            o = _dot_nt(qg[:, s_], st.astype(BF16)) + _dot(A[hd].astype(BF16), v_b[:, s_])
            sthg[hd] = st * dec[:, s_] + _dot_tn(v_b[:, s_], kd[:, s_])
            o_heads.append(_rms(o, gn_ref[layer:layer + 1, s_]))
        yb = jnp.concatenate(o_heads, axis=1) * _silu(gz)
        mix_s[rows, D_A:D_A + D_HG] = yb.astype(BF16)

        xbc_c = xbc_s[rows, :]
        xs = xbc_c[:, 0:D_SSD]
        Bm = xbc_c[:, D_SSD:D_SSD + SSD_GROUPS * SSD_N].astype(BF16)
        Cm = xbc_c[:, D_SSD + SSD_GROUPS * SSD_N:].astype(BF16)
        dt = dt_s[rows, :]
        dtT_c = dtT_s[c]
        cs = _dot(tril3_ref[...], _split3_rows(dt * a_row))
        csT = _dot(_split3_cols(dtT_c * a_col), triu3_ref[...])
        cs_last = cs[CH - 1:CH, :]
        w_all = jnp.concatenate([dt * jnp.exp(cs_last - cs), jnp.exp(cs), dt,
                                 jnp.broadcast_to(jnp.exp(cs_last), (8, SSD_HEADS))], axis=0)
        e_all = _dot(_split3_cols(w_all), expand_ref[...])
        e_dec, e_cs, e_dt, e_last = e_all[0:CH], e_all[CH:2 * CH], e_all[2 * CH:3 * CH], e_all[3 * CH:3 * CH + 1]
        xdt = (xs * e_dt).astype(BF16)
        xdec = (xs * e_dec).astype(BF16)
        y_groups = []
        hpg = SSD_HEADS // SSD_GROUPS
        gw = hpg * SSD_P
        for g in range(SSD_GROUPS):
            Cg = Cm[:, g * SSD_N:(g + 1) * SSD_N]
            Bg = Bm[:, g * SSD_N:(g + 1) * SSD_N]
            cb = _dot_nt(Cg, Bg) * tril
            st = stssd[g]
            gcols = slice(g * gw, (g + 1) * gw)
            y_off = _dot(Cg, st.astype(BF16)) * e_cs[:, gcols]
            stssd[g] = st * e_last[:, gcols] + _dot_tn(Bg, xdec[:, gcols])
            pair_out = []
            for pr in range(hpg // 2):
                h0 = g * hpg + 2 * pr
                ms = []
                for hh in (h0, h0 + 1):
                    diff = cs[:, hh:hh + 1] - csT[hh:hh + 1, :]
                    ms.append((cb * jnp.exp(jnp.minimum(diff, 0.0))).astype(BF16))
                both = _dot(jnp.concatenate(ms, axis=0), xdt[:, h0 * SSD_P:(h0 + 2) * SSD_P])
                pair_out.append(jnp.where(first_of_pair, both[0:CH], both[CH:2 * CH]))
            y_groups.append(y_off + jnp.concatenate(pair_out, axis=1))
        y = jnp.concatenate(y_groups, axis=1) + row(dx_ref) * xs
        yc = _rms(y * _silu(z_s[rows, :]), row(snorm_ref))
        mix_s[rows, D_A + D_HG:] = yc.astype(BF16)
        return carry

    lax.fori_loop(0, n_chunks, chunk, 0, unroll=True)

    x1 = x + _rms(_dot(mix_s[...], wout_ref[...]), row(gpost_ref))
    hx = _rms(x1, row(gprex_ref)).astype(BF16)
    q = _dot(hx, wq_ref[...])
    att = _cross_attention(q, mk_ref[0], mv_ref[0])
    y_ref[0] = x1 + _rms(_dot(att.astype(BF16), wo_ref[...]), row(gpostx_ref))

    @pl.when(ti == pl.num_programs(1) - 1)
    def _():
        for hd in range(HG_HEADS):
            hg_ref[0, hd] = sthg[hd].T
        hpg = SSD_HEADS // SSD_GROUPS
        for g in range(SSD_GROUPS):
            sg = stssd[g].T
            for hh in range(hpg):
                ss_ref[0, g * hpg + hh] = sg[hh * SSD_P:(hh + 1) * SSD_P, :]


def _prompt_layer(x, mk, mv, wts, layer, depth, prev, T):
    b, L, _ = x.shape
    prev = [] if prev is None else list(prev)
    c = _consts()
    n_chunks = T // CH
    const_names = ("masks", "tril", "tril3", "triu3", "expand3")
    consts = [c[k] for k in const_names]
    small = [wts[k] for k in ("caw", "lb", "gn", "scw", "scb", "dtb", "dtbc", "al", "alc", "dx", "snorm",
                              "gpre", "gpost", "gprex", "gpostx")]
    big = [wts[k] for k in ("winT", "wout", "wq", "wo")]

    full = lambda a: pl.BlockSpec(a.shape, lambda bi, ti, _n=a.ndim: (0,) * _n, pipeline_mode=pl.Buffered(1))
    big_spec = lambda a: _layer_spec(a, layer, pipeline_mode=pl.Buffered(1))

    in_specs = ([pl.BlockSpec((1, T, D_MODEL), lambda bi, ti: (bi, ti, 0)),
                 pl.BlockSpec((1, N_MEM, D_MODEL), lambda bi, ti: (bi, 0, 0)),
                 pl.BlockSpec((1, N_MEM, D_MODEL), lambda bi, ti: (bi, 0, 0))]
                + [big_spec(a) for a in big] + [full(a) for a in small] + [full(a) for a in consts]
                + [_ANY] * len(prev))
    n_in = len(in_specs)
    state_shapes = [(CONV_A_W - 1, D_A), (HG_HEADS, HG_DK, HG_DK), (SSD_CONV_W - 1, SSD_CONV_DIM),
                    (SSD_HEADS, SSD_P, SSD_N)]
    out_shape = ([jax.ShapeDtypeStruct((b, L, D_MODEL), F32)]
                 + [jax.ShapeDtypeStruct((depth, b) + s, F32) for s in state_shapes])
    out_specs = ([pl.BlockSpec((1, T, D_MODEL), lambda bi, ti: (bi, ti, 0))]
                 + [pl.BlockSpec((None, 1) + s, lambda bi, ti, _n=len(s): (layer, bi) + (0,) * _n)
                    for s in state_shapes])
    aliases = {n_in - len(prev) + k: 1 + k for k in range(len(prev))}
    scratch = [pltpu.VMEM((8, D_A), F32), pltpu.VMEM((8, SSD_CONV_DIM), F32),
               pltpu.VMEM((T, 4 * D_HG), F32), pltpu.VMEM((T, D_SSD), F32), pltpu.VMEM((T, SSD_CONV_DIM), F32),
               pltpu.VMEM((T, SSD_HEADS), F32), pltpu.VMEM((n_chunks, SSD_HEADS, CH), F32),
               pltpu.VMEM((T, 2 * D_MODEL), BF16),
               pltpu.VMEM((HG_HEADS, HG_DK, HG_DK), F32),
               pltpu.VMEM((SSD_GROUPS, SSD_N, (SSD_HEADS // SSD_GROUPS) * SSD_P), F32)]
    return pl.pallas_call(
        functools.partial(_prompt_kernel, T=T, layer=layer, n_prev=len(prev)),
        grid=(b, L // T),
        in_specs=in_specs, out_specs=out_specs, out_shape=out_shape, scratch_shapes=scratch,
        input_output_aliases=aliases,
        compiler_params=pltpu.CompilerParams(dimension_semantics=("arbitrary", "arbitrary"),
                                             vmem_limit_bytes=VMEM_LIMIT),
        name=f"prompt_layer{layer}",
    )(x, mk, mv, *big, *small, *consts, *prev)


def _prep_weights(w_in, conv_a_w, hgrn_lb, hgrn_gnorm, ssd_conv_w, ssd_conv_b, ssd_dt_bias, ssd_A_log, ssd_D, ssd_norm,
                  w_out, g_pre_mix, g_post_mix, g_pre_x, g_post_x, g_mem, w_q, w_k, w_v, w_o):
    return dict(
        winT=jnp.transpose(w_in, (0, 2, 1)).astype(BF16),
        wout=w_out.astype(BF16), wq=w_q.astype(BF16), wo=w_o.astype(BF16), wk=w_k.astype(BF16), wv=w_v.astype(BF16),
        caw=conv_a_w, lb=hgrn_lb, gn=hgrn_gnorm, scw=ssd_conv_w, scb=ssd_conv_b,
        dtb=ssd_dt_bias, dtbc=ssd_dt_bias.T, al=ssd_A_log, alc=ssd_A_log.T,
        dx=jnp.repeat(ssd_D, SSD_P, axis=1), snorm=ssd_norm,
        gpre=g_pre_mix, gpost=g_post_mix, gprex=g_pre_x, gpostx=g_post_x, gmem=g_mem)


SB = 8
D_HGP = 4 * D_HG
D_SSP = 4 * D_SSD + 2 * SSD_GROUPS * SSD_N


def _sample_pre_kernel(x_ref, ca_ref, sc_ref, winT_ref, caw_ref, lb_ref, scw_ref, scb_ref, dtb_ref, al_ref,
                       gpre_ref, expand_ref,
                       *rest, layer):
    ya_ref, canew_ref, hgp_ref, ssp_ref, scnew_ref = rest[-5:]
    row = lambda ref: ref[layer:layer + 1, :]
    h = _rms(x_ref[...], row(gpre_ref)).astype(BF16)
    u = _dot_nt(h, winT_ref[...])
    a_h, a_b, a_c, a_z = (u[:, OFF_A + k * D_A:OFF_A + (k + 1) * D_A] for k in range(4))
    va = a_c * a_h
    p0, p1 = ca_ref[:, 0, :], ca_ref[:, 1, :]
    caw = caw_ref[layer]
    conv = va * caw[2:3, :] + p1 * caw[1:2, :] + p0 * caw[0:1, :]
    ya_ref[...] = a_b * conv * _silu(a_z)
    canew_ref[:, 0, :] = p1
    canew_ref[:, 1, :] = va
    lb = _hgrn_lower_bound(lb_ref[...], layer)
    gq, gf, gi, gz = (u[:, OFF_G + k * D_HG:OFF_G + (k + 1) * D_HG] for k in range(4))
    hgp_ref[:, 0:D_HG] = gq
    hgp_ref[:, D_HG:2 * D_HG] = lb + (1.0 - lb) * _sigmoid(gf)
    hgp_ref[:, 2 * D_HG:3 * D_HG] = gi
    hgp_ref[:, 3 * D_HG:] = gz
    sxbc = u[:, OFF_XBC:OFF_XBC + SSD_CONV_DIM]
    q0, q1, q2 = sc_ref[0], sc_ref[1], sc_ref[2]
    scw = scw_ref[layer]
    xbc = _silu(sxbc * scw[3:4, :] + q2 * scw[2:3, :] + q1 * scw[1:2, :] + q0 * scw[0:1, :] + row(scb_ref))
    scnew_ref[0] = q1
    scnew_ref[1] = q2
    scnew_ref[2] = sxbc
    xs = xbc[:, 0:D_SSD]
    dt = _softplus(u[:, OFF_DT:OFF_DT + SSD_HEADS] + row(dtb_ref))
    dec = jnp.exp(dt * -jnp.exp(row(al_ref)))
    n = dt.shape[0]
    e_all = _dot(_split3_cols(jnp.concatenate([dt, dec], axis=0)), expand_ref[...])
    ssp_ref[:, 0:D_SSD] = xs
    ssp_ref[:, D_SSD:2 * D_SSD] = xs * e_all[0:n]
    ssp_ref[:, 2 * D_SSD:3 * D_SSD] = e_all[n:2 * n]
    ssp_ref[:, 3 * D_SSD:4 * D_SSD] = u[:, OFF_SZ:OFF_SZ + D_SSD]
    ssp_ref[:, 4 * D_SSD:] = xbc[:, D_SSD:]


def _sample_pre(x, ca_all, sc_all, wts, layer, prev):
    n = x.shape[0]
    prev = [] if prev is None else list(prev)
    args = [x, ca_all, sc_all, wts["winT"], wts["caw"], wts["lb"], wts["scw"], wts["scb"], wts["dtb"], wts["al"],
            wts["gpre"], _consts()["expand3"]]
    out_shape = [jax.ShapeDtypeStruct((n, D_A), F32), jax.ShapeDtypeStruct(ca_all.shape, F32),
                 jax.ShapeDtypeStruct((n, D_HGP), F32), jax.ShapeDtypeStruct((n, D_SSP), F32),
                 jax.ShapeDtypeStruct(sc_all.shape, F32)]
    return pl.pallas_call(
        functools.partial(_sample_pre_kernel, layer=layer),
        in_specs=[_full_spec(x)] + [_layer_spec(a, layer) for a in args[1:4]]
        + [_full_spec(a) for a in args[4:]] + [_ANY] * len(prev),
        out_specs=[_full_spec(out_shape[0]), _layer_spec(out_shape[1], layer), _full_spec(out_shape[2]),
                   _full_spec(out_shape[3]), _layer_spec(out_shape[4], layer)],
        out_shape=out_shape, grid=(1,),
        input_output_aliases=dict(zip(range(len(args), len(args) + len(prev)), (1, 4))),
        compiler_params=pltpu.CompilerParams(dimension_semantics=("arbitrary",), vmem_limit_bytes=VMEM_LIMIT),
        name=f"sample_pre{layer}",
    )(*args, *prev)


def _pad_rows_T(blk):
    w = blk.shape[1]
    return jnp.concatenate([blk, jnp.zeros((128 - blk.shape[0], w), blk.dtype)], axis=0).T


def _sample_state_kernel(hgp_ref, ssp_ref, shg_ref, sss_ref, *rest):
    o_ref, y_ref, shg_out, sss_out = rest[-4:]
    rid_hg = lax.broadcasted_iota(jnp.int32, (SB, HG_DK), 0)
    for hd in range(HG_HEADS):
        cols = slice(hd * HG_DK, (hd + 1) * HG_DK)
        q_b = hgp_ref[:, cols].astype(BF16)
        fT = _pad_rows_T(hgp_ref[:, D_HG + hd * HG_DK:D_HG + (hd + 1) * HG_DK])
        o = jnp.zeros((SB, HG_DK), F32)
        for j in range(SB):
            fcol = fT[:, j:j + 1]
            vrow = hgp_ref[j:j + 1, 2 * D_HG + hd * HG_DK:2 * D_HG + (hd + 1) * HG_DK]
            s_new = vrow + fcol * (shg_ref[j, hd] - vrow)
            shg_out[j, hd] = s_new
            o = jnp.where(rid_hg == j, _dot(q_b, s_new.astype(BF16)), o)
        o_ref[:, cols] = o
    gw = (SSD_HEADS // SSD_GROUPS) * SSD_P
    rid_ss = lax.broadcasted_iota(jnp.int32, (SB, gw), 0)
    xdtT = _pad_rows_T(ssp_ref[:, D_SSD:2 * D_SSD])
    decT = _pad_rows_T(ssp_ref[:, 2 * D_SSD:3 * D_SSD])
    for g in range(SSD_GROUPS):
        rows = slice(g * gw, (g + 1) * gw)
        c_b = ssp_ref[:, 4 * D_SSD + (SSD_GROUPS + g) * SSD_N:4 * D_SSD + (SSD_GROUPS + g + 1) * SSD_N].astype(BF16)
        y = jnp.zeros((SB, gw), F32)
        for j in range(SB):
            brow = ssp_ref[j:j + 1, 4 * D_SSD + g * SSD_N:4 * D_SSD + (g + 1) * SSD_N]
            decayed = []
            for hh in range(SSD_HEADS // SSD_GROUPS):
                r0 = g * gw + hh * SSD_P
                dec_h = jnp.broadcast_to(decT[r0:r0 + 8, j:j + 1], (8, SSD_N))
                s_old = sss_ref[j, r0:r0 + SSD_P, :].reshape(SSD_P // 8, 8, SSD_N)
                decayed.append((s_old * dec_h[None]).reshape(SSD_P, SSD_N))
            s_new = jnp.concatenate(decayed, axis=0) + xdtT[rows, j:j + 1] * brow
            sss_out[j, rows, :] = s_new
            y = jnp.where(rid_ss == j, _dot_nt(c_b, s_new.astype(BF16)), y)
        y_ref[:, rows] = y


def _sample_state(hgp, ssp, shg_all, sss_all, layer, prev):
    n = hgp.shape[0]
    prev = [] if prev is None else list(prev)
    rowblk = lambda w: pl.BlockSpec((SB, w), lambda i: (i, 0))
    hg_blk = pl.BlockSpec((None, SB, HG_HEADS, HG_DK, HG_DK), lambda i: (layer, i, 0, 0, 0))
    ss_blk = pl.BlockSpec((None, SB, SSD_HEADS * SSD_P, SSD_N), lambda i: (layer, i, 0, 0))
    return pl.pallas_call(
        _sample_state_kernel,
        grid=(n // SB,),
        in_specs=[rowblk(D_HGP), rowblk(D_SSP), hg_blk, ss_blk] + [_ANY] * len(prev),
        out_specs=[rowblk(D_HG), rowblk(D_SSD), hg_blk, ss_blk],
        out_shape=[jax.ShapeDtypeStruct((n, D_HG), F32), jax.ShapeDtypeStruct((n, D_SSD), F32),
                   jax.ShapeDtypeStruct(shg_all.shape, F32), jax.ShapeDtypeStruct(sss_all.shape, F32)],
        input_output_aliases={4 + k: 2 + k for k in range(len(prev))},
        compiler_params=pltpu.CompilerParams(dimension_semantics=("arbitrary",), vmem_limit_bytes=VMEM_LIMIT),
        name="sample_state",
    )(hgp, ssp, shg_all, sss_all, *prev)


def _sample_mid_kernel(x_ref, ya_ref, o_ref, y_ref, hgp_ref, ssp_ref, wout_ref, wq_ref, gn_ref, dx_ref, snorm_ref,
                       gpost_ref, gprex_ref, x1_ref, q_ref, *, layer):
    row = lambda ref: ref[layer:layer + 1, :]
    gz = hgp_ref[:, 3 * D_HG:]
    o = o_ref[...]
    yb = jnp.concatenate([_rms(o[:, hd * HG_DK:(hd + 1) * HG_DK], gn_ref[layer:layer + 1, hd * HG_DK:(hd + 1) * HG_DK])
                          for hd in range(HG_HEADS)], axis=1) * _silu(gz)
    y = y_ref[...] + row(dx_ref) * ssp_ref[:, 0:D_SSD]
    yc = _rms(y * _silu(ssp_ref[:, 3 * D_SSD:4 * D_SSD]), row(snorm_ref))
    mix = jnp.concatenate([ya_ref[...], yb, yc], axis=1).astype(BF16)
    x1 = x_ref[...] + _rms(_dot(mix, wout_ref[...]), row(gpost_ref))
    x1_ref[...] = x1
    q = _dot(_rms(x1, row(gprex_ref)).astype(BF16), wq_ref[...])
    q_ref[...] = _to_kv_rows(q)


def _sample_mid(x, ya, o, y, hgp, ssp, wts, layer):
    n = x.shape[0]
    args = [x, ya, o, y, hgp, ssp, wts["wout"], wts["wq"], wts["gn"], wts["dx"], wts["snorm"], wts["gpost"],
            wts["gprex"]]
    out_shape = [jax.ShapeDtypeStruct((n, D_MODEL), F32),
                 jax.ShapeDtypeStruct((n * KV_SUB, 128), F32)]
    return pl.pallas_call(
        functools.partial(_sample_mid_kernel, layer=layer), grid=(1,),
        in_specs=[_full_spec(a) for a in args[:6]] + [_layer_spec(a, layer) for a in args[6:8]]
        + [_full_spec(a) for a in args[8:]], out_specs=[_full_spec(s) for s in out_shape], out_shape=out_shape,
        compiler_params=pltpu.CompilerParams(dimension_semantics=("arbitrary",), vmem_limit_bytes=VMEM_LIMIT),
        name="sample_mid",
    )(*args)


def _lane_class_reduce(x, op):
    sh = KV_SUB
    while sh < 128:
        x = op(x, pltpu.roll(x, sh, axis=1))
        sh *= 2
    return x


def _sample_attn_kernel(x1_ref, q_ref, k_ref, v_ref, wo_ref, gpostx_ref, x2_ref, *, layer):
    lane = lax.broadcasted_iota(jnp.int32, (KV_SUB, KV_ROWS), 1)
    sub = lax.broadcasted_iota(jnp.int32, (KV_SUB, KV_ROWS), 0)
    own = ((lane & (KV_SUB - 1)) == sub).astype(F32)
    rid = lax.broadcasted_iota(jnp.int32, (SB, KV_ROWS), 0)
    t_all = jnp.zeros((SB, KV_ROWS), F32)
    for j in range(SB):
        r = _dot_nt(q_ref[j].astype(BF16), k_ref[j].astype(BF16))
        t = jnp.sum(r * own, axis=0, keepdims=True)
        t_all = jnp.where(rid == j, t, t_all)
    n_tiles = KV_ROWS // 128
    lane1 = lax.broadcasted_iota(jnp.int32, (SB, 128), 1)
    piece = (lane1 // XA_HEADS) % KV_SPLIT
    chunks = []
    for c in range(n_tiles):
        x = t_all[:, c * 128:(c + 1) * 128]
        tot = x
        for k in range(1, KV_SPLIT):
            fwd = pltpu.roll(x, 128 - k * XA_HEADS, axis=1)
            bwd = pltpu.roll(x, (KV_SPLIT - k) * XA_HEADS, axis=1)
            tot = tot + jnp.where(piece + k < KV_SPLIT, fwd, bwd)
        chunks.append(tot * (XA_HD ** -0.5))
    mx = _lane_class_reduce(functools.reduce(jnp.maximum, chunks), jnp.maximum)
    es = [jnp.exp(ch - mx) for ch in chunks]
    den = _lane_class_reduce(functools.reduce(lambda a, b: a + b, es), lambda a, b: a + b)
    p_all = jnp.concatenate([e * (1.0 / den) for e in es], axis=1)
    rid_o = lax.broadcasted_iota(jnp.int32, (SB, D_MODEL), 0)
    att = jnp.zeros((SB, D_MODEL), F32)
    for j in range(SB):
        p8 = (own * p_all[j:j + 1, :]).astype(BF16)
        o = _dot(p8, v_ref[j].astype(BF16))
        row = jnp.concatenate([o[k * XA_HEADS + hd:k * XA_HEADS + hd + 1, :]
                               for hd in range(XA_HEADS) for k in range(KV_SPLIT)], axis=1)
        att = jnp.where(rid_o == j, row, att)
    x2_ref[...] = x1_ref[...] + _rms(_dot(att.astype(BF16), wo_ref[...]), gpostx_ref[layer:layer + 1, :])


def _sample_attn(x1, q8, ck_rows, cv_rows, wts, layer):
    n = x1.shape[0]
    rowblk = pl.BlockSpec((SB, D_MODEL), lambda i: (i, 0))
    qblk = pl.BlockSpec((SB, KV_SUB, 128), lambda i: (i, 0, 0))
    kvblk = pl.BlockSpec((None, SB, KV_ROWS, 128), lambda i: (layer, i, 0, 0))
    return pl.pallas_call(
        functools.partial(_sample_attn_kernel, layer=layer),
        grid=(n // SB,),
        in_specs=[rowblk, qblk, kvblk, kvblk, _layer_spec(wts["wo"], layer), _full_spec(wts["gpostx"])],
        out_specs=rowblk,
        out_shape=jax.ShapeDtypeStruct((n, D_MODEL), F32),
        compiler_params=pltpu.CompilerParams(dimension_semantics=("arbitrary",), vmem_limit_bytes=VMEM_LIMIT),
        name="sample_attn",
    )(x1, q8, ck_rows, cv_rows, wts["wo"], wts["gpostx"])


def _sample_layer(x, ca_all, shg_all, sc_all, sss_all, ck_rows, cv_rows, wts, layer, prev):
    n = x.shape[0]
    prev_conv, prev_states = (None, None) if prev is None else (prev[:2], prev[2:])
    ya, ca_new, hgp, ssp, sc_new = _sample_pre(x, ca_all, sc_all, wts, layer, prev_conv)
    o, y, shg_new, sss_new = _sample_state(hgp, ssp, shg_all, sss_all, layer, prev_states)
    x1, q8 = _sample_mid(x, ya, o, y, hgp, ssp, wts, layer)
    x2 = _sample_attn(x1, q8.reshape(n, KV_SUB, 128), ck_rows, cv_rows, wts, layer)
    return x2, (ca_new, sc_new, shg_new, sss_new)


PROMPT_TILE = 512


def kernel(x_prompt, x_sample, mem_prompt, state_conv_a, state_hgrn, state_ssd_conv, state_ssd, cache_mem_k,
           cache_mem_v, w_in, conv_a_w, hgrn_lb, hgrn_gnorm, ssd_conv_w, ssd_conv_b, ssd_dt_bias, ssd_A_log, ssd_D,
           ssd_norm, w_out, g_pre_mix, g_post_mix, g_pre_x, g_post_x, g_mem, w_q, w_k, w_v, w_o):
    depth = w_in.shape[0]
    n = x_sample.shape[0]
    yp = x_prompt
    ys = x_sample.reshape(n, D_MODEL)
    ck_rows, cv_rows = _kv_rows_view(cache_mem_k), _kv_rows_view(cache_mem_v)
    sss_all = state_ssd.reshape(depth, n, SSD_HEADS * SSD_P, SSD_N)
    sc_all = jnp.transpose(state_ssd_conv, (0, 2, 1, 3))
    kv_rows = p_states = s_states = None
    wts =_prep_weights(w_in, conv_a_w, hgrn_lb, hgrn_gnorm, ssd_conv_w, ssd_conv_b, ssd_dt_bias, ssd_A_log, ssd_D,
                        ssd_norm, w_out, g_pre_mix, g_post_mix, g_pre_x, g_post_x, g_mem, w_q, w_k, w_v, w_o)
    for l in range(depth):
        *kv_rows, mk, mv = _memory_kv(mem_prompt, wts, l, depth, kv_rows)
        yp, *p_states = _prompt_layer(yp, mk, mv, wts, l, depth, p_states, PROMPT_TILE)
        ys, s_states = _sample_layer(ys, state_conv_a, state_hgrn, sc_all, sss_all, ck_rows, cv_rows, wts, l,
                                     s_states)
    p_ca, p_hg, p_sc, p_ss = p_states
    s_ca, s_sc, s_hg, s_ss = s_states
    return (yp, ys.reshape(x_sample.shape), p_ca, p_hg, p_sc, p_ss, _kv_from_rows(kv_rows[0]),
            _kv_from_rows(kv_rows[1]), s_ca, s_hg, jnp.transpose(s_sc, (0, 2, 1, 3)), s_ss.reshape(state_ssd.shape))
```

```python
import functools

import numpy as np
import jax
import jax.numpy as jnp
from jax import lax
from jax.experimental import pallas as pl
from jax.experimental.pallas import tpu as pltpu

F32 = jnp.float32
BF16 = jnp.bfloat16

D_MODEL = 1024
D_A = 512
CONV_A_W = 3
D_HG = 512
HG_HEADS = 4
HG_DK = 128
D_SSD = 1024
SSD_P = 64
SSD_HEADS = 16
SSD_GROUPS = 2
SSD_N = 128
SSD_CONV_W = 4
SSD_CONV_DIM = D_SSD + 2 * SSD_GROUPS * SSD_N
N_MEM = 256
XA_HEADS = 4
XA_HD = 256
EPS = 1e-6
KV_SPLIT = XA_HD // 128
KV_SUB = XA_HEADS * KV_SPLIT
KV_ROWS = N_MEM * KV_SUB

OFF_A = 0
OFF_G = 2048
OFF_SZ = 4096
OFF_XBC = 5120
OFF_DT = 6656
D_IN = 6672

CH = 128
HG_LEVELS = (1, 2, 4, 8, 16, 32, 64)
VMEM_LIMIT = 56 * 1024 * 1024


def _rms(x, g):
    ms = jnp.mean(x * x, axis=-1, keepdims=True)
    return x * lax.rsqrt(ms + EPS) * g


def _silu(x):
    return x * (1.0 / (1.0 + jnp.exp(-x)))


def _sigmoid(x):
    return 1.0 / (1.0 + jnp.exp(-x))


def _softplus(x):
    return jnp.maximum(x, 0.0) + jnp.log(1.0 + jnp.exp(-jnp.abs(x)))


def _dot(a, b):
    return jnp.dot(a, b, preferred_element_type=F32)


def _dot_nt(a, b):
    return lax.dot_general(a, b, (((1,), (1,)), ((), ())), preferred_element_type=F32)


def _dot_tn(a, b):
    return lax.dot_general(a, b, (((0,), (0,)), ((), ())), preferred_element_type=F32)


def _split3(x):
    hi = x.astype(BF16)
    r = x - hi.astype(F32)
    mid = r.astype(BF16)
    lo = (r - mid.astype(F32)).astype(BF16)
    return hi, mid, lo


def _split3_rows(x):
    return jnp.concatenate(_split3(x), axis=0)


def _split3_cols(x):
    return jnp.concatenate(_split3(x), axis=1)


@functools.lru_cache(maxsize=None)
def _consts():
    r = np.arange(CH)
    i, t = r[:, None], r[None, :]
    masks = [np.eye(CH, dtype=bool)]
    for s in HG_LEVELS:
        up = ((r // s) % 2 == 1)
        same = (i // (2 * s)) == (t // (2 * s))
        masks.append(same & up[:, None] & (~up)[None, :])
    masks = np.stack(masks).astype(np.float32)
    tril = (t <= i).astype(np.float32)
    tril3 = np.tile(tril, (1, 3))
    triu3 = np.tile(tril.T, (3, 1))
    e = (np.arange(D_SSD)[None, :] // SSD_P == np.arange(SSD_HEADS)[:, None]).astype(np.float32)
    expand3 = np.tile(e, (3, 1))
    return dict(
        masks=jnp.asarray(masks, F32),
        tril=jnp.asarray(tril, F32), tril3=jnp.asarray(tril3, BF16), triu3=jnp.asarray(triu3, BF16),
        expand3=jnp.asarray(expand3, BF16))


def _kv_rows_view(c_all):
    depth, n = c_all.shape[:2]
    c = c_all.reshape(depth, n, N_MEM, XA_HEADS, KV_SPLIT, 128)
    return jnp.transpose(c, (0, 1, 2, 4, 3, 5)).reshape(depth, n, KV_ROWS, 128)


def _kv_from_rows(r_all):
    depth, n = r_all.shape[:2]
    c = r_all.reshape(depth, n, N_MEM, KV_SPLIT, XA_HEADS, 128)
    return jnp.transpose(c, (0, 1, 2, 4, 3, 5)).reshape(depth, n, N_MEM, XA_HEADS, XA_HD)


def _to_kv_rows(x):
    pieces = [x[:, hd * XA_HD + k * 128:hd * XA_HD + (k + 1) * 128] for k in range(KV_SPLIT) for hd in range(XA_HEADS)]
    return jnp.concatenate(pieces, axis=1).reshape(x.shape[0] * KV_SUB, 128)


_ANY = pl.BlockSpec(memory_space=pl.ANY)


def _full_spec(a):
    nd = a.ndim
    return pl.BlockSpec(a.shape, lambda *_, _n=nd: (0,) * _n)


def _layer_spec(a, layer, **kw):
    nd = a.ndim
    return pl.BlockSpec((None,) + tuple(a.shape[1:]), lambda *_, _n=nd: (layer,) + (0,) * (_n - 1), **kw)


def _memkv_kernel(mem_ref, g_ref, wk_ref, wv_ref, *refs, layer):
    kr_ref, vr_ref, kb_ref, vb_ref = refs[-4:]
    m = _rms(mem_ref[0], g_ref[layer:layer + 1, :]).astype(BF16)
    for w_ref, r_ref, b_ref in ((wk_ref, kr_ref, kb_ref), (wv_ref, vr_ref, vb_ref)):
        kv = _dot(m, w_ref[...])
        b_ref[0] = kv.astype(BF16)
        r_ref[...] = _to_kv_rows(kv)


def _memory_kv(mem, wts, layer, depth, prev):
    b = mem.shape[0]
    blk = pl.BlockSpec((1, N_MEM, D_MODEL), lambda i: (i, 0, 0))
    rows_blk = pl.BlockSpec((None, None, KV_ROWS, 128), lambda i: (layer, i, 0, 0))
    rows_sds = jax.ShapeDtypeStruct((depth, b, KV_ROWS, 128), F32)
    extra, extra_specs, aliases = [], [], {}
    if prev is not None:
        extra, extra_specs, aliases = list(prev), [_ANY, _ANY], {4: 0, 5: 1}
    return pl.pallas_call(
        functools.partial(_memkv_kernel, layer=layer),
        grid=(b,),
        in_specs=[blk, _full_spec(wts["gmem"]), _layer_spec(wts["wk"], layer), _layer_spec(wts["wv"], layer)]
        + extra_specs,
        out_specs=[rows_blk, rows_blk, blk, blk],
        out_shape=[rows_sds, rows_sds] + [jax.ShapeDtypeStruct((b, N_MEM, D_MODEL), BF16)] * 2,
        input_output_aliases=aliases,
        compiler_params=pltpu.CompilerParams(dimension_semantics=("arbitrary",), vmem_limit_bytes=VMEM_LIMIT),
        name="memory_kv",
    )(mem, wts["gmem"], wts["wk"], wts["wv"], *extra)


def _hgrn_lower_bound(lb_all, layer):
    depth = lb_all.shape[0]
    rows = [lb_all[j:j + 1, :] for j in range(depth)]
    mx = functools.reduce(jnp.maximum, rows)
    ex = [jnp.exp(rw - mx) for rw in rows]
    tot = functools.reduce(lambda a, b: a + b, ex)
    acc = jnp.zeros_like(tot)
    for j in range(1, layer + 1):
        acc = acc + ex[j]
    return acc / tot


def _hgrn_level(c, f, q, k, s):
    n, w = c.shape
    if s >= 8:
        nb = n // (2 * s)
        c4, q4, k4 = (a.reshape(nb, 2, s, w) for a in (c, q, k))
        lower, upper = c4[:, 0], c4[:, 1]
        tot = lower[:, s - 1:s, :]
        w_lower = k4[:, 0] * jnp.exp(tot - lower)
        w_upper = q4[:, 1] * jnp.exp(upper)
        wv = jnp.stack([w_lower, w_upper], axis=1).reshape(n, w)
        c_next = jnp.stack([lower, upper + tot], axis=1).reshape(n, w)
        return wv, c_next
    sub = lax.broadcasted_iota(jnp.int32, (1, 8, w), 1)
    c3, f3, q3, k3 = (a.reshape(n // 8, 8, w) for a in (c, f, q, k))
    up = (sub // s) % 2 == 1
    tot = None
    for gi in reversed(range(8 // (2 * s))):
        r = gi * 2 * s + s - 1
        tg = jnp.broadcast_to(c3[:, r:r + 1, :], c3.shape)
        tot = tg if tot is None else jnp.where(sub < (gi + 1) * 2 * s, tg, tot)
    if s == 1:
        e = jnp.where(up, f3, 1.0)
    else:
        e = jnp.exp(jnp.where(up, c3, tot - c3))
    wv = jnp.where(up, q3, k3) * e
    c_next = c3 + jnp.where(up, tot, 0.0)
    return wv.reshape(n, w), c_next.reshape(n, w)


def _causal_conv(x, taps, prev_ref):
    n_taps = len(taps)
    row0 = lax.broadcasted_iota(jnp.int32, (8, x.shape[1]), 0) == 0
    prev = [prev_ref[8 - d:8 - d + 1, :] for d in range(1, n_taps)]
    acc = x * taps[0]
    for k in range(1, n_taps):
        carry = functools.reduce(lambda a, b: a + b, [taps[j] * prev[k - j - 1] for j in range(k)])
        rolled = pltpu.roll(acc, 1, axis=0)
        shifted = jnp.concatenate([jnp.where(row0, carry, rolled[0:8]), rolled[8:]], axis=0)
        acc = x * taps[k] + shifted
    return acc


def _cross_attention(q, mk, mv):
    outs = []
    for hd in range(XA_HEADS):
        sl = slice(hd * XA_HD, (hd + 1) * XA_HD)
        s = _dot_nt(q[:, sl].astype(BF16), mk[:, sl]) * (XA_HD ** -0.5)
        s = s - jnp.max(s, axis=-1, keepdims=True)
        e = jnp.exp(s)
        p = e * (1.0 / jnp.sum(e, axis=-1, keepdims=True))
        outs.append(_dot(p.astype(BF16), mv[:, sl]))
    return jnp.concatenate(outs, axis=1)


def _prompt_kernel(x_ref, mk_ref, mv_ref, winT_ref, wout_ref, wq_ref, wo_ref,
                   caw_ref, lb_ref, gn_ref, scw_ref, scb_ref, dtb_ref, dtbc_ref, al_ref, alc_ref, dx_ref,
                   snorm_ref, gpre_ref, gpost_ref, gprex_ref, gpostx_ref,
                   masks_ref, tril_ref, tril3_ref, triu3_ref, expand_ref,
                   *rest, T, layer, n_prev):
    (y_ref, ca_ref, hg_ref, sc_ref, ss_ref,
     bufa, bufc, ug_s, z_s, xbc_s, dt_s, dtT_s, mix_s, sthg, stssd) = rest[n_prev:]
    ti = pl.program_id(1)
    n_chunks = T // CH

    @pl.when(ti == 0)
    def _():
        bufa[0:8, :] = jnp.zeros((8, D_A), F32)
        bufc[0:8, :] = jnp.zeros((8, SSD_CONV_DIM), F32)
        sthg[...] = jnp.zeros(sthg.shape, F32)
        stssd[...] = jnp.zeros(stssd.shape, F32)

    x = x_ref[0]
    row = lambda ref: ref[layer:layer + 1, :]
    h = _rms(x, row(gpre_ref)).astype(BF16)

    sxbc = _dot_nt(h, winT_ref[OFF_XBC:OFF_XBC + SSD_CONV_DIM, :])
    scw = scw_ref[layer]
    xbc = _causal_conv(sxbc, [scw[k:k + 1, :] for k in range(SSD_CONV_W)], bufc) + row(scb_ref)
    xbc_s[...] = _silu(xbc)
    sc_ref[0] = sxbc[T - 3:T, :]
    bufc[0:8, :] = sxbc[T - 8:T, :]
    wdtT = winT_ref[OFF_DT:OFF_DT + SSD_HEADS, :]
    sdt = _dot_nt(h, wdtT)
    dt_s[...] = _softplus(sdt + row(dtb_ref))
    dtT = _softplus(_dot_nt(wdtT, h) + dtbc_ref[:, layer:layer + 1])
    for c in range(n_chunks):
        dtT_s[c] = dtT[:, c * CH:(c + 1) * CH]
    z_s[...] = _dot_nt(h, winT_ref[OFF_SZ:OFF_SZ + D_SSD, :])

    ua = _dot_nt(h, winT_ref[OFF_A:OFF_A + 4 * D_A, :])
    a_h, a_b, a_c, a_z = (ua[:, k * D_A:(k + 1) * D_A] for k in range(4))
    va = a_c * a_h
    caw = caw_ref[layer]
    conv = _causal_conv(va, [caw[k:k + 1, :] for k in range(CONV_A_W)], bufa)
    mix_s[:, 0:D_A] = (a_b * conv * _silu(a_z)).astype(BF16)
    ca_ref[0] = va[T - 2:T, :]
    bufa[0:8, :] = va[T - 8:T, :]

    ug_s[...] = _dot_nt(h, winT_ref[OFF_G:OFF_G + 4 * D_HG, :])


    lb = _hgrn_lower_bound(lb_ref[...], layer)
    a_row = -jnp.exp(row(al_ref))
    a_col = -jnp.exp(alc_ref[:, layer:layer + 1])
    tril = tril_ref[...]
    first_of_pair = lax.broadcasted_iota(jnp.int32, (1, 2 * SSD_P), 1) < SSD_P

    def chunk(c, carry):
        r0 = pl.multiple_of(c * CH, CH)
        rows = pl.ds(r0, CH)

        ug = ug_s[rows, :]
        gq, gf, gi, gz = (ug[:, k * D_HG:(k + 1) * D_HG] for k in range(4))
        f = lb + (1.0 - lb) * _sigmoid(gf)
        logf = jnp.log(f)
        kk = 1.0 - f
        q_b, k_b, v_b = gq.astype(BF16), kk.astype(BF16), gi.astype(BF16)
        hs = [slice(hd * HG_DK, (hd + 1) * HG_DK) for hd in range(HG_HEADS)]
        A = [masks_ref[0] * _dot_nt(q_b[:, s_], k_b[:, s_]) for s_ in hs]
        G = logf
        for li, s in enumerate(HG_LEVELS):
            w, G = _hgrn_level(G, f, gq, kk, s)
            w = w.astype(BF16)
            m = masks_ref[li + 1]
            if s < 8:
                A = [A[hd] + m * _dot_nt(w[:, hs[hd]], w[:, hs[hd]]) for hd in range(HG_HEADS)]
            else:
                nb = CH // (2 * s)
                upper = lambda a: a.reshape(nb, 2, s, a.shape[-1])[:, 1].reshape(CH // 2, a.shape[-1])
                w_up, m_up = upper(w), upper(m)
                for hd in range(HG_HEADS):
                    p_up = (m_up * _dot_nt(w_up[:, hs[hd]], w[:, hs[hd]])).reshape(nb, s, CH)
                    a4 = A[hd].reshape(nb, 2, s, CH)
                    A[hd] = jnp.stack([a4[:, 0], a4[:, 1] + p_up], axis=1).reshape(CH, CH)
        g_last = G[CH - 1:CH, :]
        qg = (gq * jnp.exp(G)).astype(BF16)
        kd = (kk * jnp.exp(g_last - G)).astype(BF16)
        dec = jnp.exp(g_last)
        o_heads = []
        for hd in range(HG_HEADS):
            s_ = hs[hd]
            st = sthg[hd]
            o = _dot_nt(qg[:, s_], st.astype(BF16)) + _dot(A[hd].astype(BF16), v_b[:, s_])
            sthg[hd] = st * dec[:, s_] + _dot_tn(v_b[:, s_], kd[:, s_])
            o_heads.append(_rms(o, gn_ref[layer:layer + 1, s_]))
        yb = jnp.concatenate(o_heads, axis=1) * _silu(gz)
        mix_s[rows, D_A:D_A + D_HG] = yb.astype(BF16)

        xbc_c = xbc_s[rows, :]
        xs = xbc_c[:, 0:D_SSD]
        Bm = xbc_c[:, D_SSD:D_SSD + SSD_GROUPS * SSD_N].astype(BF16)
        Cm = xbc_c[:, D_SSD + SSD_GROUPS * SSD_N:].astype(BF16)
        dt = dt_s[rows, :]
        dtT_c = dtT_s[c]
        cs = _dot(tril3_ref[...], _split3_rows(dt * a_row))
        csT = _dot(_split3_cols(dtT_c * a_col), triu3_ref[...])
        cs_last = cs[CH - 1:CH, :]
        w_all = jnp.concatenate([dt * jnp.exp(cs_last - cs), jnp.exp(cs), dt,
                                 jnp.broadcast_to(jnp.exp(cs_last), (8, SSD_HEADS))], axis=0)
        e_all = _dot(_split3_cols(w_all), expand_ref[...])
        e_dec, e_cs, e_dt, e_last = e_all[0:CH], e_all[CH:2 * CH], e_all[2 * CH:3 * CH], e_all[3 * CH:3 * CH + 1]
        xdt = (xs * e_dt).astype(BF16)
        xdec = (xs * e_dec).astype(BF16)
        y_groups = []
        hpg = SSD_HEADS // SSD_GROUPS
        gw = hpg * SSD_P
        for g in range(SSD_GROUPS):
            Cg = Cm[:, g * SSD_N:(g + 1) * SSD_N]
            Bg = Bm[:, g * SSD_N:(g + 1) * SSD_N]
            cb = _dot_nt(Cg, Bg) * tril
            st = stssd[g]
            gcols = slice(g * gw, (g + 1) * gw)
            y_off = _dot(Cg, st.astype(BF16)) * e_cs[:, gcols]
            stssd[g] = st * e_last[:, gcols] + _dot_tn(Bg, xdec[:, gcols])
            pair_out = []
            for pr in range(hpg // 2):
                h0 = g * hpg + 2 * pr
                ms = []
                for hh in (h0, h0 + 1):
                    diff = cs[:, hh:hh + 1] - csT[hh:hh + 1, :]
                    ms.append((cb * jnp.exp(jnp.minimum(diff, 0.0))).astype(BF16))
                both = _dot(jnp.concatenate(ms, axis=0), xdt[:, h0 * SSD_P:(h0 + 2) * SSD_P])
                pair_out.append(jnp.where(first_of_pair, both[0:CH], both[CH:2 * CH]))
            y_groups.append(y_off + jnp.concatenate(pair_out, axis=1))
        y = jnp.concatenate(y_groups, axis=1) + row(dx_ref) * xs
        yc = _rms(y * _silu(z_s[rows, :]), row(snorm_ref))
        mix_s[rows, D_A + D_HG:] = yc.astype(BF16)
        return carry

    lax.fori_loop(0, n_chunks, chunk, 0, unroll=True)

    x1 = x + _rms(_dot(mix_s[...], wout_ref[...]), row(gpost_ref))
    hx = _rms(x1, row(gprex_ref)).astype(BF16)
    q = _dot(hx, wq_ref[...])
    att = _cross_attention(q, mk_ref[0], mv_ref[0])
    y_ref[0] = x1 + _rms(_dot(att.astype(BF16), wo_ref[...]), row(gpostx_ref))

    @pl.when(ti == pl.num_programs(1) - 1)
    def _():
        for hd in range(HG_HEADS):
            hg_ref[0, hd] = sthg[hd].T
        hpg = SSD_HEADS // SSD_GROUPS
        for g in range(SSD_GROUPS):
            sg = stssd[g].T
            for hh in range(hpg):
                ss_ref[0, g * hpg + hh] = sg[hh * SSD_P:(hh + 1) * SSD_P, :]


def _prompt_layer(x, mk, mv, wts, layer, depth, prev, T):
    b, L, _ = x.shape
    prev = [] if prev is None else list(prev)
    c = _consts()
    n_chunks = T // CH
    const_names = ("masks", "tril", "tril3", "triu3", "expand3")
    consts = [c[k] for k in const_names]
    small = [wts[k] for k in ("caw", "lb", "gn", "scw", "scb", "dtb", "dtbc", "al", "alc", "dx", "snorm",
                              "gpre", "gpost", "gprex", "gpostx")]
    big = [wts[k] for k in ("winT", "wout", "wq", "wo")]

    full = lambda a: pl.BlockSpec(a.shape, lambda bi, ti, _n=a.ndim: (0,) * _n, pipeline_mode=pl.Buffered(1))
    big_spec = lambda a: _layer_spec(a, layer, pipeline_mode=pl.Buffered(1))

    in_specs = ([pl.BlockSpec((1, T, D_MODEL), lambda bi, ti: (bi, ti, 0)),
                 pl.BlockSpec((1, N_MEM, D_MODEL), lambda bi, ti: (bi, 0, 0)),
                 pl.BlockSpec((1, N_MEM, D_MODEL), lambda bi, ti: (bi, 0, 0))]
                + [big_spec(a) for a in big] + [full(a) for a in small] + [full(a) for a in consts]
                + [_ANY] * len(prev))
    n_in = len(in_specs)
    state_shapes = [(CONV_A_W - 1, D_A), (HG_HEADS, HG_DK, HG_DK), (SSD_CONV_W - 1, SSD_CONV_DIM),
                    (SSD_HEADS, SSD_P, SSD_N)]
    out_shape = ([jax.ShapeDtypeStruct((b, L, D_MODEL), F32)]
                 + [jax.ShapeDtypeStruct((depth, b) + s, F32) for s in state_shapes])
    out_specs = ([pl.BlockSpec((1, T, D_MODEL), lambda bi, ti: (bi, ti, 0))]
                 + [pl.BlockSpec((None, 1) + s, lambda bi, ti, _n=len(s): (layer, bi) + (0,) * _n)
                    for s in state_shapes])
    aliases = {n_in - len(prev) + k: 1 + k for k in range(len(prev))}
    scratch = [pltpu.VMEM((8, D_A), F32), pltpu.VMEM((8, SSD_CONV_DIM), F32),
               pltpu.VMEM((T, 4 * D_HG), F32), pltpu.VMEM((T, D_SSD), F32), pltpu.VMEM((T, SSD_CONV_DIM), F32),
               pltpu.VMEM((T, SSD_HEADS), F32), pltpu.VMEM((n_chunks, SSD_HEADS, CH), F32),
               pltpu.VMEM((T, 2 * D_MODEL), BF16),
               pltpu.VMEM((HG_HEADS, HG_DK, HG_DK), F32),
               pltpu.VMEM((SSD_GROUPS, SSD_N, (SSD_HEADS // SSD_GROUPS) * SSD_P), F32)]
    return pl.pallas_call(
        functools.partial(_prompt_kernel, T=T, layer=layer, n_prev=len(prev)),
        grid=(b, L // T),
        in_specs=in_specs, out_specs=out_specs, out_shape=out_shape, scratch_shapes=scratch,
        input_output_aliases=aliases,
        compiler_params=pltpu.CompilerParams(dimension_semantics=("arbitrary", "arbitrary"),
                                             vmem_limit_bytes=VMEM_LIMIT),
        name=f"prompt_layer{layer}",
    )(x, mk, mv, *big, *small, *consts, *prev)


def _prep_weights(w_in, conv_a_w, hgrn_lb, hgrn_gnorm, ssd_conv_w, ssd_conv_b, ssd_dt_bias, ssd_A_log, ssd_D, ssd_norm,
                  w_out, g_pre_mix, g_post_mix, g_pre_x, g_post_x, g_mem, w_q, w_k, w_v, w_o):
    return dict(
        winT=jnp.transpose(w_in, (0, 2, 1)).astype(BF16),
        wout=w_out.astype(BF16), wq=w_q.astype(BF16), wo=w_o.astype(BF16), wk=w_k.astype(BF16), wv=w_v.astype(BF16),
        caw=conv_a_w, lb=hgrn_lb, gn=hgrn_gnorm, scw=ssd_conv_w, scb=ssd_conv_b,
        dtb=ssd_dt_bias, dtbc=ssd_dt_bias.T, al=ssd_A_log, alc=ssd_A_log.T,
        dx=jnp.repeat(ssd_D, SSD_P, axis=1), snorm=ssd_norm,
        gpre=g_pre_mix, gpost=g_post_mix, gprex=g_pre_x, gpostx=g_post_x, gmem=g_mem)


SB = 8
D_HGP = 4 * D_HG
D_SSP = 4 * D_SSD + 2 * SSD_GROUPS * SSD_N


def _sample_pre_kernel(x_ref, ca_ref, sc_ref, winT_ref, caw_ref, lb_ref, scw_ref, scb_ref, dtb_ref, al_ref,
                       gpre_ref, expand_ref,
                       *rest, layer):
    ya_ref, canew_ref, hgp_ref, ssp_ref, scnew_ref = rest[-5:]
    row = lambda ref: ref[layer:layer + 1, :]
    h = _rms(x_ref[...], row(gpre_ref)).astype(BF16)
    u = _dot_nt(h, winT_ref[...])
    a_h, a_b, a_c, a_z = (u[:, OFF_A + k * D_A:OFF_A + (k + 1) * D_A] for k in range(4))
    va = a_c * a_h
    p0, p1 = ca_ref[:, 0, :], ca_ref[:, 1, :]
    caw = caw_ref[layer]
    conv = va * caw[2:3, :] + p1 * caw[1:2, :] + p0 * caw[0:1, :]
    ya_ref[...] = a_b * conv * _silu(a_z)
    canew_ref[:, 0, :] = p1
    canew_ref[:, 1, :] = va
    lb = _hgrn_lower_bound(lb_ref[...], layer)
    gq, gf, gi, gz = (u[:, OFF_G + k * D_HG:OFF_G + (k + 1) * D_HG] for k in range(4))
    hgp_ref[:, 0:D_HG] = gq
    hgp_ref[:, D_HG:2 * D_HG] = lb + (1.0 - lb) * _sigmoid(gf)
    hgp_ref[:, 2 * D_HG:3 * D_HG] = gi
    hgp_ref[:, 3 * D_HG:] = gz
    sxbc = u[:, OFF_XBC:OFF_XBC + SSD_CONV_DIM]
    q0, q1, q2 = sc_ref[0], sc_ref[1], sc_ref[2]
    scw = scw_ref[layer]
    xbc = _silu(sxbc * scw[3:4, :] + q2 * scw[2:3, :] + q1 * scw[1:2, :] + q0 * scw[0:1, :] + row(scb_ref))
    scnew_ref[0] = q1
    scnew_ref[1] = q2
    scnew_ref[2] = sxbc
    xs = xbc[:, 0:D_SSD]
    dt = _softplus(u[:, OFF_DT:OFF_DT + SSD_HEADS] + row(dtb_ref))
    dec = jnp.exp(dt * -jnp.exp(row(al_ref)))
    n = dt.shape[0]
    e_all = _dot(_split3_cols(jnp.concatenate([dt, dec], axis=0)), expand_ref[...])
    ssp_ref[:, 0:D_SSD] = xs
    ssp_ref[:, D_SSD:2 * D_SSD] = xs * e_all[0:n]
    ssp_ref[:, 2 * D_SSD:3 * D_SSD] = e_all[n:2 * n]
    ssp_ref[:, 3 * D_SSD:4 * D_SSD] = u[:, OFF_SZ:OFF_SZ + D_SSD]
    ssp_ref[:, 4 * D_SSD:] = xbc[:, D_SSD:]


def _sample_pre(x, ca_all, sc_all, wts, layer, prev):
    n = x.shape[0]
    prev = [] if prev is None else list(prev)
    args = [x, ca_all, sc_all, wts["winT"], wts["caw"], wts["lb"], wts["scw"], wts["scb"], wts["dtb"], wts["al"],
            wts["gpre"], _consts()["expand3"]]
    out_shape = [jax.ShapeDtypeStruct((n, D_A), F32), jax.ShapeDtypeStruct(ca_all.shape, F32),
                 jax.ShapeDtypeStruct((n, D_HGP), F32), jax.ShapeDtypeStruct((n, D_SSP), F32),
                 jax.ShapeDtypeStruct(sc_all.shape, F32)]
    return pl.pallas_call(
        functools.partial(_sample_pre_kernel, layer=layer),
        in_specs=[_full_spec(x)] + [_layer_spec(a, layer) for a in args[1:4]]
        + [_full_spec(a) for a in args[4:]] + [_ANY] * len(prev),
        out_specs=[_full_spec(out_shape[0]), _layer_spec(out_shape[1], layer), _full_spec(out_shape[2]),
                   _full_spec(out_shape[3]), _layer_spec(out_shape[4], layer)],
        out_shape=out_shape, grid=(1,),
        input_output_aliases=dict(zip(range(len(args), len(args) + len(prev)), (1, 4))),
        compiler_params=pltpu.CompilerParams(dimension_semantics=("arbitrary",), vmem_limit_bytes=VMEM_LIMIT),
        name=f"sample_pre{layer}",
    )(*args, *prev)


def _pad_rows_T(blk):
    w = blk.shape[1]
    return jnp.concatenate([blk, jnp.zeros((128 - blk.shape[0], w), blk.dtype)], axis=0).T


def _sample_state_kernel(hgp_ref, ssp_ref, shg_ref, sss_ref, *rest):
    o_ref, y_ref, shg_out, sss_out = rest[-4:]
    rid_hg = lax.broadcasted_iota(jnp.int32, (SB, HG_DK), 0)
    for hd in range(HG_HEADS):
        cols = slice(hd * HG_DK, (hd + 1) * HG_DK)
        q_b = hgp_ref[:, cols].astype(BF16)
        fT = _pad_rows_T(hgp_ref[:, D_HG + hd * HG_DK:D_HG + (hd + 1) * HG_DK])
        o = jnp.zeros((SB, HG_DK), F32)
        for j in range(SB):
            fcol = fT[:, j:j + 1]
            vrow = hgp_ref[j:j + 1, 2 * D_HG + hd * HG_DK:2 * D_HG + (hd + 1) * HG_DK]
            s_new = vrow + fcol * (shg_ref[j, hd] - vrow)
            shg_out[j, hd] = s_new
            o = jnp.where(rid_hg == j, _dot(q_b, s_new.astype(BF16)), o)
        o_ref[:, cols] = o
    gw = (SSD_HEADS // SSD_GROUPS) * SSD_P
    rid_ss = lax.broadcasted_iota(jnp.int32, (SB, gw), 0)
    xdtT = _pad_rows_T(ssp_ref[:, D_SSD:2 * D_SSD])
    decT = _pad_rows_T(ssp_ref[:, 2 * D_SSD:3 * D_SSD])
    for g in range(SSD_GROUPS):
        rows = slice(g * gw, (g + 1) * gw)
        c_b = ssp_ref[:, 4 * D_SSD + (SSD_GROUPS + g) * SSD_N:4 * D_SSD + (SSD_GROUPS + g + 1) * SSD_N].astype(BF16)
        y = jnp.zeros((SB, gw), F32)
        for j in range(SB):
            brow = ssp_ref[j:j + 1, 4 * D_SSD + g * SSD_N:4 * D_SSD + (g + 1) * SSD_N]
            decayed = []
            for hh in range(SSD_HEADS // SSD_GROUPS):
                r0 = g * gw + hh * SSD_P
                dec_h = jnp.broadcast_to(decT[r0:r0 + 8, j:j + 1], (8, SSD_N))
                s_old = sss_ref[j, r0:r0 + SSD_P, :].reshape(SSD_P // 8, 8, SSD_N)
                decayed.append((s_old * dec_h[None]).reshape(SSD_P, SSD_N))
            s_new = jnp.concatenate(decayed, axis=0) + xdtT[rows, j:j + 1] * brow
            sss_out[j, rows, :] = s_new
            y = jnp.where(rid_ss == j, _dot_nt(c_b, s_new.astype(BF16)), y)
        y_ref[:, rows] = y


def _sample_state(hgp, ssp, shg_all, sss_all, layer, prev):
    n = hgp.shape[0]
    prev = [] if prev is None else list(prev)
    rowblk = lambda w: pl.BlockSpec((SB, w), lambda i: (i, 0))
    hg_blk = pl.BlockSpec((None, SB, HG_HEADS, HG_DK, HG_DK), lambda i: (layer, i, 0, 0, 0))
    ss_blk = pl.BlockSpec((None, SB, SSD_HEADS * SSD_P, SSD_N), lambda i: (layer, i, 0, 0))
    return pl.pallas_call(
        _sample_state_kernel,
        grid=(n // SB,),
        in_specs=[rowblk(D_HGP), rowblk(D_SSP), hg_blk, ss_blk] + [_ANY] * len(prev),
        out_specs=[rowblk(D_HG), rowblk(D_SSD), hg_blk, ss_blk],
        out_shape=[jax.ShapeDtypeStruct((n, D_HG), F32), jax.ShapeDtypeStruct((n, D_SSD), F32),
                   jax.ShapeDtypeStruct(shg_all.shape, F32), jax.ShapeDtypeStruct(sss_all.shape, F32)],
        input_output_aliases={4 + k: 2 + k for k in range(len(prev))},
        compiler_params=pltpu.CompilerParams(dimension_semantics=("arbitrary",), vmem_limit_bytes=VMEM_LIMIT),
        name="sample_state",
    )(hgp, ssp, shg_all, sss_all, *prev)


def _sample_mid_kernel(x_ref, ya_ref, o_ref, y_ref, hgp_ref, ssp_ref, wout_ref, wq_ref, gn_ref, dx_ref, snorm_ref,
                       gpost_ref, gprex_ref, x1_ref, q_ref, *, layer):
    row = lambda ref: ref[layer:layer + 1, :]
    gz = hgp_ref[:, 3 * D_HG:]
    o = o_ref[...]
    yb = jnp.concatenate([_rms(o[:, hd * HG_DK:(hd + 1) * HG_DK], gn_ref[layer:layer + 1, hd * HG_DK:(hd + 1) * HG_DK])
                          for hd in range(HG_HEADS)], axis=1) * _silu(gz)
    y = y_ref[...] + row(dx_ref) * ssp_ref[:, 0:D_SSD]
    yc = _rms(y * _silu(ssp_ref[:, 3 * D_SSD:4 * D_SSD]), row(snorm_ref))
    mix = jnp.concatenate([ya_ref[...], yb, yc], axis=1).astype(BF16)
    x1 = x_ref[...] + _rms(_dot(mix, wout_ref[...]), row(gpost_ref))
    x1_ref[...] = x1
    q = _dot(_rms(x1, row(gprex_ref)).astype(BF16), wq_ref[...])
    q_ref[...] = _to_kv_rows(q)


def _sample_mid(x, ya, o, y, hgp, ssp, wts, layer):
    n = x.shape[0]
    args = [x, ya, o, y, hgp, ssp, wts["wout"], wts["wq"], wts["gn"], wts["dx"], wts["snorm"], wts["gpost"],
            wts["gprex"]]
    out_shape = [jax.ShapeDtypeStruct((n, D_MODEL), F32),
                 jax.ShapeDtypeStruct((n * KV_SUB, 128), F32)]
    return pl.pallas_call(
        functools.partial(_sample_mid_kernel, layer=layer), grid=(1,),
        in_specs=[_full_spec(a) for a in args[:6]] + [_layer_spec(a, layer) for a in args[6:8]]
        + [_full_spec(a) for a in args[8:]], out_specs=[_full_spec(s) for s in out_shape], out_shape=out_shape,
        compiler_params=pltpu.CompilerParams(dimension_semantics=("arbitrary",), vmem_limit_bytes=VMEM_LIMIT),
        name="sample_mid",
    )(*args)


def _lane_class_reduce(x, op):
    sh = KV_SUB
    while sh < 128:
        x = op(x, pltpu.roll(x, sh, axis=1))
        sh *= 2
    return x


def _sample_attn_kernel(x1_ref, q_ref, k_ref, v_ref, wo_ref, gpostx_ref, x2_ref, *, layer):
    lane = lax.broadcasted_iota(jnp.int32, (KV_SUB, KV_ROWS), 1)
    sub = lax.broadcasted_iota(jnp.int32, (KV_SUB, KV_ROWS), 0)
    own = ((lane & (KV_SUB - 1)) == sub).astype(F32)
    rid = lax.broadcasted_iota(jnp.int32, (SB, KV_ROWS), 0)
    t_all = jnp.zeros((SB, KV_ROWS), F32)
    for j in range(SB):
        r = _dot_nt(q_ref[j].astype(BF16), k_ref[j].astype(BF16))
        t = jnp.sum(r * own, axis=0, keepdims=True)
        t_all = jnp.where(rid == j, t, t_all)
    n_tiles = KV_ROWS // 128
    lane1 = lax.broadcasted_iota(jnp.int32, (SB, 128), 1)
    piece = (lane1 // XA_HEADS) % KV_SPLIT
    chunks = []
    for c in range(n_tiles):
        x = t_all[:, c * 128:(c + 1) * 128]
        tot = x
        for k in range(1, KV_SPLIT):
            fwd = pltpu.roll(x, 128 - k * XA_HEADS, axis=1)
            bwd = pltpu.roll(x, (KV_SPLIT - k) * XA_HEADS, axis=1)
            tot = tot + jnp.where(piece + k < KV_SPLIT, fwd, bwd)
        chunks.append(tot * (XA_HD ** -0.5))
    mx = _lane_class_reduce(functools.reduce(jnp.maximum, chunks), jnp.maximum)
    es = [jnp.exp(ch - mx) for ch in chunks]
    den = _lane_class_reduce(functools.reduce(lambda a, b: a + b, es), lambda a, b: a + b)
    p_all = jnp.concatenate([e * (1.0 / den) for e in es], axis=1)
    rid_o = lax.broadcasted_iota(jnp.int32, (SB, D_MODEL), 0)
    att = jnp.zeros((SB, D_MODEL), F32)
    for j in range(SB):
        p8 = (own * p_all[j:j + 1, :]).astype(BF16)
        o = _dot(p8, v_ref[j].astype(BF16))
        row = jnp.concatenate([o[k * XA_HEADS + hd:k * XA_HEADS + hd + 1, :]
                               for hd in range(XA_HEADS) for k in range(KV_SPLIT)], axis=1)
        att = jnp.where(rid_o == j, row, att)
    x2_ref[...] = x1_ref[...] + _rms(_dot(att.astype(BF16), wo_ref[...]), gpostx_ref[layer:layer + 1, :])


def _sample_attn(x1, q8, ck_rows, cv_rows, wts, layer):
    n = x1.shape[0]
    rowblk = pl.BlockSpec((SB, D_MODEL), lambda i: (i, 0))
    qblk = pl.BlockSpec((SB, KV_SUB, 128), lambda i: (i, 0, 0))
    kvblk = pl.BlockSpec((None, SB, KV_ROWS, 128), lambda i: (layer, i, 0, 0))
    return pl.pallas_call(
        functools.partial(_sample_attn_kernel, layer=layer),
        grid=(n // SB,),
        in_specs=[rowblk, qblk, kvblk, kvblk, _layer_spec(wts["wo"], layer), _full_spec(wts["gpostx"])],
        out_specs=rowblk,
        out_shape=jax.ShapeDtypeStruct((n, D_MODEL), F32),
        compiler_params=pltpu.CompilerParams(dimension_semantics=("arbitrary",), vmem_limit_bytes=VMEM_LIMIT),
        name="sample_attn",
    )(x1, q8, ck_rows, cv_rows, wts["wo"], wts["gpostx"])


def _sample_layer(x, ca_all, shg_all, sc_all, sss_all, ck_rows, cv_rows, wts, layer, prev):
    n = x.shape[0]
    prev_conv, prev_states = (None, None) if prev is None else (prev[:2], prev[2:])
    ya, ca_new, hgp, ssp, sc_new = _sample_pre(x, ca_all, sc_all, wts, layer, prev_conv)
    o, y, shg_new, sss_new = _sample_state(hgp, ssp, shg_all, sss_all, layer, prev_states)
    x1, q8 = _sample_mid(x, ya, o, y, hgp, ssp, wts, layer)
    x2 = _sample_attn(x1, q8.reshape(n, KV_SUB, 128), ck_rows, cv_rows, wts, layer)
    return x2, (ca_new, sc_new, shg_new, sss_new)


PROMPT_TILE = 512


def kernel(x_prompt, x_sample, mem_prompt, state_conv_a, state_hgrn, state_ssd_conv, state_ssd, cache_mem_k,
           cache_mem_v, w_in, conv_a_w, hgrn_lb, hgrn_gnorm, ssd_conv_w, ssd_conv_b, ssd_dt_bias, ssd_A_log, ssd_D,
           ssd_norm, w_out, g_pre_mix, g_post_mix, g_pre_x, g_post_x, g_mem, w_q, w_k, w_v, w_o):
    depth = w_in.shape[0]
    n = x_sample.shape[0]
    yp = x_prompt
    ys = x_sample.reshape(n, D_MODEL)
    ck_rows, cv_rows = _kv_rows_view(cache_mem_k), _kv_rows_view(cache_mem_v)
    sss_all = state_ssd.reshape(depth, n, SSD_HEADS * SSD_P, SSD_N)
    sc_all = jnp.transpose(state_ssd_conv, (0, 2, 1, 3))
    kv_rows = p_states = s_states = None
    wts =_prep_weights(w_in, conv_a_w, hgrn_lb, hgrn_gnorm, ssd_conv_w, ssd_conv_b, ssd_dt_bias, ssd_A_log, ssd_D,
                        ssd_norm, w_out, g_pre_mix, g_post_mix, g_pre_x, g_post_x, g_mem, w_q, w_k, w_v, w_o)
    for l in range(depth):
        *kv_rows, mk, mv = _memory_kv(mem_prompt, wts, l, depth, kv_rows)
        yp, *p_states = _prompt_layer(yp, mk, mv, wts, l, depth, p_states, PROMPT_TILE)
        ys, s_states = _sample_layer(ys, state_conv_a, state_hgrn, sc_all, sss_all, ck_rows, cv_rows, wts, l,
                                     s_states)
    p_ca, p_hg, p_sc, p_ss = p_states
    s_ca, s_sc, s_hg, s_ss = s_states
    return (yp, ys.reshape(x_sample.shape), p_ca, p_hg, p_sc, p_ss, _kv_from_rows(kv_rows[0]),
            _kv_from_rows(kv_rows[1]), s_ca, s_hg, jnp.transpose(s_sc, (0, 2, 1, 3)), s_ss.reshape(state_ssd.shape))
```

```python
import functools

import numpy as np
import jax
import jax.numpy as jnp
from jax import lax
from jax.experimental import pallas as pl
from jax.experimental.pallas import tpu as pltpu

F32 = jnp.float32
BF16 = jnp.bfloat16

D_MODEL = 1024
D_A = 512
CONV_A_W = 3
D_HG = 512
HG_HEADS = 4
HG_DK = 128
D_SSD = 1024
SSD_P = 64
SSD_HEADS = 16
SSD_GROUPS = 2
SSD_N = 128
SSD_CONV_W = 4
SSD_CONV_DIM = D_SSD + 2 * SSD_GROUPS * SSD_N
N_MEM = 256
XA_HEADS = 4
XA_HD = 256
EPS = 1e-6
KV_SPLIT = XA_HD // 128
KV_SUB = XA_HEADS * KV_SPLIT
KV_ROWS = N_MEM * KV_SUB

OFF_A = 0
OFF_G = 2048
OFF_SZ = 4096
OFF_XBC = 5120
OFF_DT = 6656
D_IN = 6672

CH = 128
OUT_SLAB = 256
HG_LEVELS = (1, 2, 4, 8, 16, 32, 64)
VMEM_LIMIT = 56 * 1024 * 1024


def _rms(x, g):
    ms = jnp.mean(x * x, axis=-1, keepdims=True)
    return x * lax.rsqrt(ms + EPS) * g


def _silu(x):
    return x * (1.0 / (1.0 + jnp.exp(-x)))


def _sigmoid(x):
    return 1.0 / (1.0 + jnp.exp(-x))


def _softplus(x):
    return jnp.maximum(x, 0.0) + jnp.log(1.0 + jnp.exp(-jnp.abs(x)))


def _dot(a, b):
    return jnp.dot(a, b, preferred_element_type=F32)


def _dot_nt(a, b):
    return lax.dot_general(a, b, (((1,), (1,)), ((), ())), preferred_element_type=F32)


def _dot_tn(a, b):
    return lax.dot_general(a, b, (((0,), (0,)), ((), ())), preferred_element_type=F32)


def _split3(x):
    hi = x.astype(BF16)
    r = x - hi.astype(F32)
    mid = r.astype(BF16)
    lo = (r - mid.astype(F32)).astype(BF16)
    return hi, mid, lo


def _split3_rows(x):
    return jnp.concatenate(_split3(x), axis=0)


def _split3_cols(x):
    return jnp.concatenate(_split3(x), axis=1)


@functools.lru_cache(maxsize=None)
def _consts():
    r = np.arange(CH)
    i, t = r[:, None], r[None, :]
    masks = [np.eye(CH, dtype=bool)]
    for s in HG_LEVELS:
        up = ((r // s) % 2 == 1)
        same = (i // (2 * s)) == (t // (2 * s))
        masks.append(same & up[:, None] & (~up)[None, :])
    masks = np.stack(masks).astype(np.float32)
    tril = (t <= i).astype(np.float32)
    tril3 = np.tile(tril, (1, 3))
    triu3 = np.tile(tril.T, (3, 1))
    e = (np.arange(D_SSD)[None, :] // SSD_P == np.arange(SSD_HEADS)[:, None]).astype(np.float32)
    expand3 = np.tile(e, (3, 1))
    return dict(
        masks=jnp.asarray(masks, F32),
        tril=jnp.asarray(tril, F32), tril3=jnp.asarray(tril3, BF16), triu3=jnp.asarray(triu3, BF16),
        expand3=jnp.asarray(expand3, BF16))


def _kv_rows_view(c_all):
    depth, n = c_all.shape[:2]
    c = c_all.reshape(depth, n, N_MEM, XA_HEADS, KV_SPLIT, 128)
    return jnp.transpose(c, (0, 1, 2, 4, 3, 5)).reshape(depth, n, KV_ROWS, 128)


def _kv_from_rows(r_all):
    depth, n = r_all.shape[:2]
    c = r_all.reshape(depth, n, N_MEM, KV_SPLIT, XA_HEADS, 128)
    return jnp.transpose(c, (0, 1, 2, 4, 3, 5)).reshape(depth, n, N_MEM, XA_HEADS, XA_HD)


def _to_kv_rows(x):
    pieces = [x[:, hd * XA_HD + k * 128:hd * XA_HD + (k + 1) * 128] for k in range(KV_SPLIT) for hd in range(XA_HEADS)]
    return jnp.concatenate(pieces, axis=1).reshape(x.shape[0] * KV_SUB, 128)


_ANY = pl.BlockSpec(memory_space=pl.ANY)


def _full_spec(a):
    nd = a.ndim
    return pl.BlockSpec(a.shape, lambda *_, _n=nd: (0,) * _n)


def _layer_spec(a, layer, **kw):
    nd = a.ndim
    return pl.BlockSpec((None,) + tuple(a.shape[1:]), lambda *_, _n=nd: (layer,) + (0,) * (_n - 1), **kw)


def _memkv_kernel(mem_ref, g_ref, wk_ref, wv_ref, *refs, layer):
    kr_ref, vr_ref, kb_ref, vb_ref = refs[-4:]
    m = _rms(mem_ref[0], g_ref[layer:layer + 1, :]).astype(BF16)
    for w_ref, r_ref, b_ref in ((wk_ref, kr_ref, kb_ref), (wv_ref, vr_ref, vb_ref)):
        kv = _dot(m, w_ref[...])
        b_ref[0] = kv.astype(BF16)
        r_ref[...] = _to_kv_rows(kv)


def _memory_kv(mem, wts, layer, depth, prev):
    b = mem.shape[0]
    blk = pl.BlockSpec((1, N_MEM, D_MODEL), lambda i: (i, 0, 0))
    rows_blk = pl.BlockSpec((None, None, KV_ROWS, 128), lambda i: (layer, i, 0, 0))
    rows_sds = jax.ShapeDtypeStruct((depth, b, KV_ROWS, 128), F32)
    extra, extra_specs, aliases = [], [], {}
    if prev is not None:
        extra, extra_specs, aliases = list(prev), [_ANY, _ANY], {4: 0, 5: 1}
    return pl.pallas_call(
        functools.partial(_memkv_kernel, layer=layer),
        grid=(b,),
        in_specs=[blk, _full_spec(wts["gmem"]), _layer_spec(wts["wk"], layer), _layer_spec(wts["wv"], layer)]
        + extra_specs,
        out_specs=[rows_blk, rows_blk, blk, blk],
        out_shape=[rows_sds, rows_sds] + [jax.ShapeDtypeStruct((b, N_MEM, D_MODEL), BF16)] * 2,
        input_output_aliases=aliases,
        compiler_params=pltpu.CompilerParams(dimension_semantics=("arbitrary",), vmem_limit_bytes=VMEM_LIMIT),
        name="memory_kv",
    )(mem, wts["gmem"], wts["wk"], wts["wv"], *extra)


def _hgrn_lower_bound(lb_all, layer):
    depth = lb_all.shape[0]
    rows = [lb_all[j:j + 1, :] for j in range(depth)]
    mx = functools.reduce(jnp.maximum, rows)
    ex = [jnp.exp(rw - mx) for rw in rows]
    tot = functools.reduce(lambda a, b: a + b, ex)
    acc = jnp.zeros_like(tot)
    for j in range(1, layer + 1):
        acc = acc + ex[j]
    return acc / tot


def _hgrn_level(c, f, q, k, s):
    n, w = c.shape
    if s >= 8:
        nb = n // (2 * s)
        c4, q4, k4 = (a.reshape(nb, 2, s, w) for a in (c, q, k))
        lower, upper = c4[:, 0], c4[:, 1]
        tot = lower[:, s - 1:s, :]
        w_lower = k4[:, 0] * jnp.exp(tot - lower)
        w_upper = q4[:, 1] * jnp.exp(upper)
        wv = jnp.stack([w_lower, w_upper], axis=1).reshape(n, w)
        c_next = jnp.stack([lower, upper + tot], axis=1).reshape(n, w)
        return wv, c_next
    sub = lax.broadcasted_iota(jnp.int32, (1, 8, w), 1)
    c3, f3, q3, k3 = (a.reshape(n // 8, 8, w) for a in (c, f, q, k))
    up = (sub // s) % 2 == 1
    tot = None
    for gi in reversed(range(8 // (2 * s))):
        r = gi * 2 * s + s - 1
        tg = jnp.broadcast_to(c3[:, r:r + 1, :], c3.shape)
        tot = tg if tot is None else jnp.where(sub < (gi + 1) * 2 * s, tg, tot)
    if s == 1:
        e = jnp.where(up, f3, 1.0)
    else:
        e = jnp.exp(jnp.where(up, c3, tot - c3))
    wv = jnp.where(up, q3, k3) * e
    c_next = c3 + jnp.where(up, tot, 0.0)
    return wv.reshape(n, w), c_next.reshape(n, w)


def _causal_conv(x, taps, prev_ref):
    n_taps = len(taps)
    row0 = lax.broadcasted_iota(jnp.int32, (8, x.shape[1]), 0) == 0
    prev = [prev_ref[8 - d:8 - d + 1, :] for d in range(1, n_taps)]
    acc = x * taps[0]
    for k in range(1, n_taps):
        carry = functools.reduce(lambda a, b: a + b, [taps[j] * prev[k - j - 1] for j in range(k)])
        rolled = pltpu.roll(acc, 1, axis=0)
        shifted = jnp.concatenate([jnp.where(row0, carry, rolled[0:8]), rolled[8:]], axis=0)
        acc = x * taps[k] + shifted
    return acc


def _cross_attention(q, mk, mv):
    outs = []
    for hd in range(XA_HEADS):
        sl = slice(hd * XA_HD, (hd + 1) * XA_HD)
        s = _dot_nt(q[:, sl].astype(BF16), mk[:, sl]) * (XA_HD ** -0.5)
        s = s - jnp.max(s, axis=-1, keepdims=True)
        e = jnp.exp(s)
        p = e * (1.0 / jnp.sum(e, axis=-1, keepdims=True))
        outs.append(_dot(p.astype(BF16), mv[:, sl]))
    return jnp.concatenate(outs, axis=1)


def _prompt_kernel(x_ref, mk_ref, mv_ref, winT_ref, wout_ref, wq_ref, wo_ref,
                   caw_ref, lb_ref, gn_ref, scw_ref, scb_ref, dtb_ref, dtbc_ref, al_ref, alc_ref, dx_ref,
                   snorm_ref, gpre_ref, gpost_ref, gprex_ref, gpostx_ref,
                   masks_ref, tril_ref, tril3_ref, triu3_ref, expand_ref,
                   *rest, T, layer, n_prev):
    (y_ref, ca_ref, hg_ref, sc_ref, ss_ref,
     bufa, bufc, ug_s, z_s, xbc_s, dt_s, dtT_s, mix_s, sthg, stssd) = rest[n_prev:]
    ti = pl.program_id(1)
    n_chunks = T // CH

    @pl.when(ti == 0)
    def _():
        bufa[0:8, :] = jnp.zeros((8, D_A), F32)
        bufc[0:8, :] = jnp.zeros((8, SSD_CONV_DIM), F32)
        sthg[...] = jnp.zeros(sthg.shape, F32)
        stssd[...] = jnp.zeros(stssd.shape, F32)

    x = x_ref[0]
    row = lambda ref: ref[layer:layer + 1, :]
    h = _rms(x, row(gpre_ref)).astype(BF16)

    sxbc = _dot_nt(h, winT_ref[OFF_XBC:OFF_XBC + SSD_CONV_DIM, :])
    scw = scw_ref[layer]
    xbc = _causal_conv(sxbc, [scw[k:k + 1, :] for k in range(SSD_CONV_W)], bufc) + row(scb_ref)
    xbc_s[...] = _silu(xbc)
    sc_ref[0] = sxbc[T - 3:T, :]
    bufc[0:8, :] = sxbc[T - 8:T, :]
    wdtT = winT_ref[OFF_DT:OFF_DT + SSD_HEADS, :]
    sdt = _dot_nt(h, wdtT)
    dt_s[...] = _softplus(sdt + row(dtb_ref))
    dtT = _softplus(_dot_nt(wdtT, h) + dtbc_ref[:, layer:layer + 1])
    for c in range(n_chunks):
        dtT_s[c] = dtT[:, c * CH:(c + 1) * CH]
    z_s[...] = _dot_nt(h, winT_ref[OFF_SZ:OFF_SZ + D_SSD, :])

    ua = _dot_nt(h, winT_ref[OFF_A:OFF_A + 4 * D_A, :])
    a_h, a_b, a_c, a_z = (ua[:, k * D_A:(k + 1) * D_A] for k in range(4))
    va = a_c * a_h
    caw = caw_ref[layer]
    conv = _causal_conv(va, [caw[k:k + 1, :] for k in range(CONV_A_W)], bufa)
    mix_s[:, 0:D_A] = (a_b * conv * _silu(a_z)).astype(BF16)
    ca_ref[0] = va[T - 2:T, :]
    bufa[0:8, :] = va[T - 8:T, :]

    ug_s[...] = _dot_nt(h, winT_ref[OFF_G:OFF_G + 4 * D_HG, :])


    lb = _hgrn_lower_bound(lb_ref[...], layer)
    a_row = -jnp.exp(row(al_ref))
    a_col = -jnp.exp(alc_ref[:, layer:layer + 1])
    tril = tril_ref[...]
    first_of_pair = lax.broadcasted_iota(jnp.int32, (1, 2 * SSD_P), 1) < SSD_P

    def chunk(c, carry):
        r0 = pl.multiple_of(c * CH, CH)
        rows = pl.ds(r0, CH)

        ug = ug_s[rows, :]
        gq, gf, gi, gz = (ug[:, k * D_HG:(k + 1) * D_HG] for k in range(4))
        f = lb + (1.0 - lb) * _sigmoid(gf)
        logf = jnp.log(f)
        kk = 1.0 - f
        q_b, k_b, v_b = gq.astype(BF16), kk.astype(BF16), gi.astype(BF16)
        hs = [slice(hd * HG_DK, (hd + 1) * HG_DK) for hd in range(HG_HEADS)]
        A = [masks_ref[0] * _dot_nt(q_b[:, s_], k_b[:, s_]) for s_ in hs]
        G = logf
        for li, s in enumerate(HG_LEVELS):
            w, G = _hgrn_level(G, f, gq, kk, s)
            w = w.astype(BF16)
            m = masks_ref[li + 1]
            if s < 8:
                A = [A[hd] + m * _dot_nt(w[:, hs[hd]], w[:, hs[hd]]) for hd in range(HG_HEADS)]
            else:
                nb = CH // (2 * s)
                upper = lambda a: a.reshape(nb, 2, s, a.shape[-1])[:, 1].reshape(CH // 2, a.shape[-1])
                w_up, m_up = upper(w), upper(m)
                for hd in range(HG_HEADS):
                    p_up = (m_up * _dot_nt(w_up[:, hs[hd]], w[:, hs[hd]])).reshape(nb, s, CH)
                    a4 = A[hd].reshape(nb, 2, s, CH)
                    A[hd] = jnp.stack([a4[:, 0], a4[:, 1] + p_up], axis=1).reshape(CH, CH)
        g_last = G[CH - 1:CH, :]
        qg = (gq * jnp.exp(G)).astype(BF16)
        kd = (kk * jnp.exp(g_last - G)).astype(BF16)
        dec = jnp.exp(g_last)
        o_heads = []
        for hd in range(HG_HEADS):
            s_ = hs[hd]
            st = sthg[hd]
            o = _dot_nt(qg[:, s_], st.astype(BF16)) + _dot(A[hd].astype(BF16), v_b[:, s_])
            sthg[hd] = st * dec[:, s_] + _dot_tn(v_b[:, s_], kd[:, s_])
            o_heads.append(_rms(o, gn_ref[layer:layer + 1, s_]))
        yb = jnp.concatenate(o_heads, axis=1) * _silu(gz)
        mix_s[rows, D_A:D_A + D_HG] = yb.astype(BF16)

        xbc_c = xbc_s[rows, :]
        xs = xbc_c[:, 0:D_SSD]
        Bm = xbc_c[:, D_SSD:D_SSD + SSD_GROUPS * SSD_N].astype(BF16)
        Cm = xbc_c[:, D_SSD + SSD_GROUPS * SSD_N:].astype(BF16)
        dt = dt_s[rows, :]
        dtT_c = dtT_s[c]
        cs = _dot(tril3_ref[...], _split3_rows(dt * a_row))
        csT = _dot(_split3_cols(dtT_c * a_col), triu3_ref[...])
        cs_last = cs[CH - 1:CH, :]
        w_all = jnp.concatenate([dt * jnp.exp(cs_last - cs), jnp.exp(cs), dt,
                                 jnp.broadcast_to(jnp.exp(cs_last), (8, SSD_HEADS))], axis=0)
        e_all = _dot(_split3_cols(w_all), expand_ref[...])
        e_dec, e_cs, e_dt, e_last = e_all[0:CH], e_all[CH:2 * CH], e_all[2 * CH:3 * CH], e_all[3 * CH:3 * CH + 1]
        xdt = (xs * e_dt).astype(BF16)
        xdec = (xs * e_dec).astype(BF16)
        y_groups = []
        hpg = SSD_HEADS // SSD_GROUPS
        gw = hpg * SSD_P
        for g in range(SSD_GROUPS):
            Cg = Cm[:, g * SSD_N:(g + 1) * SSD_N]
            Bg = Bm[:, g * SSD_N:(g + 1) * SSD_N]
            cb = _dot_nt(Cg, Bg) * tril
            st = stssd[g]
            gcols = slice(g * gw, (g + 1) * gw)
            y_off = _dot(Cg, st.astype(BF16)) * e_cs[:, gcols]
            stssd[g] = st * e_last[:, gcols] + _dot_tn(Bg, xdec[:, gcols])
            pair_out = []
            for pr in range(hpg // 2):
                h0 = g * hpg + 2 * pr
                ms = []
                for hh in (h0, h0 + 1):
                    diff = cs[:, hh:hh + 1] - csT[hh:hh + 1, :]
                    ms.append((cb * jnp.exp(jnp.minimum(diff, 0.0))).astype(BF16))
                both = _dot(jnp.concatenate(ms, axis=0), xdt[:, h0 * SSD_P:(h0 + 2) * SSD_P])
                pair_out.append(jnp.where(first_of_pair, both[0:CH], both[CH:2 * CH]))
            y_groups.append(y_off + jnp.concatenate(pair_out, axis=1))
        y = jnp.concatenate(y_groups, axis=1) + row(dx_ref) * xs
        yc = _rms(y * _silu(z_s[rows, :]), row(snorm_ref))
        mix_s[rows, D_A + D_HG:] = yc.astype(BF16)
        return carry

    lax.fori_loop(0, n_chunks, chunk, 0, unroll=True)

    for r0 in range(0, T, OUT_SLAB):
        rs = slice(r0, r0 + OUT_SLAB)
        x1 = x_ref[0, rs, :] + _rms(_dot(mix_s[rs, :], wout_ref[...]), row(gpost_ref))
        hx = _rms(x1, row(gprex_ref)).astype(BF16)
        q = _dot(hx, wq_ref[...])
        att = _cross_attention(q, mk_ref[0], mv_ref[0])
        y_ref[0, rs, :] = x1 + _rms(_dot(att.astype(BF16), wo_ref[...]), row(gpostx_ref))

    @pl.when(ti == pl.num_programs(1) - 1)
    def _():
        for hd in range(HG_HEADS):
            hg_ref[0, hd] = sthg[hd].T
        hpg = SSD_HEADS // SSD_GROUPS
        for g in range(SSD_GROUPS):
            sg = stssd[g].T
            for hh in range(hpg):
                ss_ref[0, g * hpg + hh] = sg[hh * SSD_P:(hh + 1) * SSD_P, :]


def _prompt_layer(x, mk, mv, wts, layer, depth, prev, T):
    b, L, _ = x.shape
    prev = [] if prev is None else list(prev)
    c = _consts()
    n_chunks = T // CH
    const_names = ("masks", "tril", "tril3", "triu3", "expand3")
    consts = [c[k] for k in const_names]
    small = [wts[k] for k in ("caw", "lb", "gn", "scw", "scb", "dtb", "dtbc", "al", "alc", "dx", "snorm",
                              "gpre", "gpost", "gprex", "gpostx")]
    big = [wts[k] for k in ("winT", "wout", "wq", "wo")]

    full = lambda a: pl.BlockSpec(a.shape, lambda bi, ti, _n=a.ndim: (0,) * _n, pipeline_mode=pl.Buffered(1))
    big_spec = lambda a: _layer_spec(a, layer, pipeline_mode=pl.Buffered(1))

    in_specs = ([pl.BlockSpec((1, T, D_MODEL), lambda bi, ti: (bi, ti, 0)),
                 pl.BlockSpec((1, N_MEM, D_MODEL), lambda bi, ti: (bi, 0, 0)),
                 pl.BlockSpec((1, N_MEM, D_MODEL), lambda bi, ti: (bi, 0, 0))]
                + [big_spec(a) for a in big] + [full(a) for a in small] + [full(a) for a in consts]
                + [_ANY] * len(prev))
    n_in = len(in_specs)
    state_shapes = [(CONV_A_W - 1, D_A), (HG_HEADS, HG_DK, HG_DK), (SSD_CONV_W - 1, SSD_CONV_DIM),
                    (SSD_HEADS, SSD_P, SSD_N)]
    out_shape = ([jax.ShapeDtypeStruct((b, L, D_MODEL), F32)]
                 + [jax.ShapeDtypeStruct((depth, b) + s, F32) for s in state_shapes])
    out_specs = ([pl.BlockSpec((1, T, D_MODEL), lambda bi, ti: (bi, ti, 0))]
                 + [pl.BlockSpec((None, 1) + s, lambda bi, ti, _n=len(s): (layer, bi) + (0,) * _n)
                    for s in state_shapes])
    aliases = {n_in - len(prev) + k: 1 + k for k in range(len(prev))}
    scratch = [pltpu.VMEM((8, D_A), F32), pltpu.VMEM((8, SSD_CONV_DIM), F32),
               pltpu.VMEM((T, 4 * D_HG), F32), pltpu.VMEM((T, D_SSD), F32), pltpu.VMEM((T, SSD_CONV_DIM), F32),
               pltpu.VMEM((T, SSD_HEADS), F32), pltpu.VMEM((n_chunks, SSD_HEADS, CH), F32),
               pltpu.VMEM((T, 2 * D_MODEL), BF16),
               pltpu.VMEM((HG_HEADS, HG_DK, HG_DK), F32),
               pltpu.VMEM((SSD_GROUPS, SSD_N, (SSD_HEADS // SSD_GROUPS) * SSD_P), F32)]
    return pl.pallas_call(
        functools.partial(_prompt_kernel, T=T, layer=layer, n_prev=len(prev)),
        grid=(b, L // T),
        in_specs=in_specs, out_specs=out_specs, out_shape=out_shape, scratch_shapes=scratch,
        input_output_aliases=aliases,
        compiler_params=pltpu.CompilerParams(dimension_semantics=("arbitrary", "arbitrary"),
                                             vmem_limit_bytes=VMEM_LIMIT),
        name=f"prompt_layer{layer}",
    )(x, mk, mv, *big, *small, *consts, *prev)


def _prep_weights(w_in, conv_a_w, hgrn_lb, hgrn_gnorm, ssd_conv_w, ssd_conv_b, ssd_dt_bias, ssd_A_log, ssd_D, ssd_norm,
                  w_out, g_pre_mix, g_post_mix, g_pre_x, g_post_x, g_mem, w_q, w_k, w_v, w_o):
    return dict(
        winT=jnp.transpose(w_in, (0, 2, 1)).astype(BF16),
        wout=w_out.astype(BF16), wq=w_q.astype(BF16), wo=w_o.astype(BF16), wk=w_k.astype(BF16), wv=w_v.astype(BF16),
        caw=conv_a_w, lb=hgrn_lb, gn=hgrn_gnorm, scw=ssd_conv_w, scb=ssd_conv_b,
        dtb=ssd_dt_bias, dtbc=ssd_dt_bias.T, al=ssd_A_log, alc=ssd_A_log.T,
        dx=jnp.repeat(ssd_D, SSD_P, axis=1), snorm=ssd_norm,
        gpre=g_pre_mix, gpost=g_post_mix, gprex=g_pre_x, gpostx=g_post_x, gmem=g_mem)


SB = 8
D_HGP = 4 * D_HG
D_SSP = 4 * D_SSD + 2 * SSD_GROUPS * SSD_N


def _sample_pre_kernel(x_ref, ca_ref, sc_ref, winT_ref, caw_ref, lb_ref, scw_ref, scb_ref, dtb_ref, al_ref,
                       gpre_ref, expand_ref,
                       *rest, layer):
    ya_ref, canew_ref, hgp_ref, ssp_ref, scnew_ref = rest[-5:]
    row = lambda ref: ref[layer:layer + 1, :]
    h = _rms(x_ref[...], row(gpre_ref)).astype(BF16)
    u = _dot_nt(h, winT_ref[...])
    a_h, a_b, a_c, a_z = (u[:, OFF_A + k * D_A:OFF_A + (k + 1) * D_A] for k in range(4))
    va = a_c * a_h
    p0, p1 = ca_ref[:, 0, :], ca_ref[:, 1, :]
    caw = caw_ref[layer]
    conv = va * caw[2:3, :] + p1 * caw[1:2, :] + p0 * caw[0:1, :]
    ya_ref[...] = a_b * conv * _silu(a_z)
    canew_ref[:, 0, :] = p1
    canew_ref[:, 1, :] = va
    lb = _hgrn_lower_bound(lb_ref[...], layer)
    gq, gf, gi, gz = (u[:, OFF_G + k * D_HG:OFF_G + (k + 1) * D_HG] for k in range(4))
    hgp_ref[:, 0:D_HG] = gq
    hgp_ref[:, D_HG:2 * D_HG] = lb + (1.0 - lb) * _sigmoid(gf)
    hgp_ref[:, 2 * D_HG:3 * D_HG] = gi
    hgp_ref[:, 3 * D_HG:] = gz
    sxbc = u[:, OFF_XBC:OFF_XBC + SSD_CONV_DIM]
    q0, q1, q2 = sc_ref[0], sc_ref[1], sc_ref[2]
    scw = scw_ref[layer]
    xbc = _silu(sxbc * scw[3:4, :] + q2 * scw[2:3, :] + q1 * scw[1:2, :] + q0 * scw[0:1, :] + row(scb_ref))
    scnew_ref[0] = q1
    scnew_ref[1] = q2
    scnew_ref[2] = sxbc
    xs = xbc[:, 0:D_SSD]
    dt = _softplus(u[:, OFF_DT:OFF_DT + SSD_HEADS] + row(dtb_ref))
    dec = jnp.exp(dt * -jnp.exp(row(al_ref)))
    n = dt.shape[0]
    e_all = _dot(_split3_cols(jnp.concatenate([dt, dec], axis=0)), expand_ref[...])
    ssp_ref[:, 0:D_SSD] = xs
    ssp_ref[:, D_SSD:2 * D_SSD] = xs * e_all[0:n]
    ssp_ref[:, 2 * D_SSD:3 * D_SSD] = e_all[n:2 * n]
    ssp_ref[:, 3 * D_SSD:4 * D_SSD] = u[:, OFF_SZ:OFF_SZ + D_SSD]
    ssp_ref[:, 4 * D_SSD:] = xbc[:, D_SSD:]


def _sample_pre(x, ca_all, sc_all, wts, layer, prev):
    n = x.shape[0]
    prev = [] if prev is None else list(prev)
    args = [x, ca_all, sc_all, wts["winT"], wts["caw"], wts["lb"], wts["scw"], wts["scb"], wts["dtb"], wts["al"],
            wts["gpre"], _consts()["expand3"]]
    out_shape = [jax.ShapeDtypeStruct((n, D_A), F32), jax.ShapeDtypeStruct(ca_all.shape, F32),
                 jax.ShapeDtypeStruct((n, D_HGP), F32), jax.ShapeDtypeStruct((n, D_SSP), F32),
                 jax.ShapeDtypeStruct(sc_all.shape, F32)]
    return pl.pallas_call(
        functools.partial(_sample_pre_kernel, layer=layer),
        in_specs=[_full_spec(x)] + [_layer_spec(a, layer) for a in args[1:4]]
        + [_full_spec(a) for a in args[4:]] + [_ANY] * len(prev),
        out_specs=[_full_spec(out_shape[0]), _layer_spec(out_shape[1], layer), _full_spec(out_shape[2]),
                   _full_spec(out_shape[3]), _layer_spec(out_shape[4], layer)],
        out_shape=out_shape, grid=(1,),
        input_output_aliases=dict(zip(range(len(args), len(args) + len(prev)), (1, 4))),
        compiler_params=pltpu.CompilerParams(dimension_semantics=("arbitrary",), vmem_limit_bytes=VMEM_LIMIT),
        name=f"sample_pre{layer}",
    )(*args, *prev)


def _pad_rows_T(blk):
    w = blk.shape[1]
    return jnp.concatenate([blk, jnp.zeros((128 - blk.shape[0], w), blk.dtype)], axis=0).T


def _sample_state_kernel(hgp_ref, ssp_ref, shg_ref, sss_ref, *rest):
    o_ref, y_ref, shg_out, sss_out = rest[-4:]
    rid_hg = lax.broadcasted_iota(jnp.int32, (SB, HG_DK), 0)
    for hd in range(HG_HEADS):
        cols = slice(hd * HG_DK, (hd + 1) * HG_DK)
        q_b = hgp_ref[:, cols].astype(BF16)
        fT = _pad_rows_T(hgp_ref[:, D_HG + hd * HG_DK:D_HG + (hd + 1) * HG_DK])
        o = jnp.zeros((SB, HG_DK), F32)
        for j in range(SB):
            fcol = fT[:, j:j + 1]
            vrow = hgp_ref[j:j + 1, 2 * D_HG + hd * HG_DK:2 * D_HG + (hd + 1) * HG_DK]
            s_new = vrow + fcol * (shg_ref[j, hd] - vrow)
            shg_out[j, hd] = s_new
            o = jnp.where(rid_hg == j, _dot(q_b, s_new.astype(BF16)), o)
        o_ref[:, cols] = o
    gw = (SSD_HEADS // SSD_GROUPS) * SSD_P
    rid_ss = lax.broadcasted_iota(jnp.int32, (SB, gw), 0)
    xdtT = _pad_rows_T(ssp_ref[:, D_SSD:2 * D_SSD])
    decT = _pad_rows_T(ssp_ref[:, 2 * D_SSD:3 * D_SSD])
    for g in range(SSD_GROUPS):
        rows = slice(g * gw, (g + 1) * gw)
        c_b = ssp_ref[:, 4 * D_SSD + (SSD_GROUPS + g) * SSD_N:4 * D_SSD + (SSD_GROUPS + g + 1) * SSD_N].astype(BF16)
        y = jnp.zeros((SB, gw), F32)
        for j in range(SB):
            brow = ssp_ref[j:j + 1, 4 * D_SSD + g * SSD_N:4 * D_SSD + (g + 1) * SSD_N]
            decayed = []
            for hh in range(SSD_HEADS // SSD_GROUPS):
                r0 = g * gw + hh * SSD_P
                dec_h = jnp.broadcast_to(decT[r0:r0 + 8, j:j + 1], (8, SSD_N))
                s_old = sss_ref[j, r0:r0 + SSD_P, :].reshape(SSD_P // 8, 8, SSD_N)
                decayed.append((s_old * dec_h[None]).reshape(SSD_P, SSD_N))
            s_new = jnp.concatenate(decayed, axis=0) + xdtT[rows, j:j + 1] * brow
            sss_out[j, rows, :] = s_new
            y = jnp.where(rid_ss == j, _dot_nt(c_b, s_new.astype(BF16)), y)
        y_ref[:, rows] = y


def _sample_state(hgp, ssp, shg_all, sss_all, layer, prev):
    n = hgp.shape[0]
    prev = [] if prev is None else list(prev)
    rowblk = lambda w: pl.BlockSpec((SB, w), lambda i: (i, 0))
    hg_blk = pl.BlockSpec((None, SB, HG_HEADS, HG_DK, HG_DK), lambda i: (layer, i, 0, 0, 0))
    ss_blk = pl.BlockSpec((None, SB, SSD_HEADS * SSD_P, SSD_N), lambda i: (layer, i, 0, 0))
    return pl.pallas_call(
        _sample_state_kernel,
        grid=(n // SB,),
        in_specs=[rowblk(D_HGP), rowblk(D_SSP), hg_blk, ss_blk] + [_ANY] * len(prev),
        out_specs=[rowblk(D_HG), rowblk(D_SSD), hg_blk, ss_blk],
        out_shape=[jax.ShapeDtypeStruct((n, D_HG), F32), jax.ShapeDtypeStruct((n, D_SSD), F32),
                   jax.ShapeDtypeStruct(shg_all.shape, F32), jax.ShapeDtypeStruct(sss_all.shape, F32)],
        input_output_aliases={4 + k: 2 + k for k in range(len(prev))},
        compiler_params=pltpu.CompilerParams(dimension_semantics=("arbitrary",), vmem_limit_bytes=VMEM_LIMIT),
        name="sample_state",
    )(hgp, ssp, shg_all, sss_all, *prev)


def _sample_mid_kernel(x_ref, ya_ref, o_ref, y_ref, hgp_ref, ssp_ref, wout_ref, wq_ref, gn_ref, dx_ref, snorm_ref,
                       gpost_ref, gprex_ref, x1_ref, q_ref, *, layer):
    row = lambda ref: ref[layer:layer + 1, :]
    gz = hgp_ref[:, 3 * D_HG:]
    o = o_ref[...]
    yb = jnp.concatenate([_rms(o[:, hd * HG_DK:(hd + 1) * HG_DK], gn_ref[layer:layer + 1, hd * HG_DK:(hd + 1) * HG_DK])
                          for hd in range(HG_HEADS)], axis=1) * _silu(gz)
    y = y_ref[...] + row(dx_ref) * ssp_ref[:, 0:D_SSD]
    yc = _rms(y * _silu(ssp_ref[:, 3 * D_SSD:4 * D_SSD]), row(snorm_ref))
    mix = jnp.concatenate([ya_ref[...], yb, yc], axis=1).astype(BF16)
    x1 = x_ref[...] + _rms(_dot(mix, wout_ref[...]), row(gpost_ref))
    x1_ref[...] = x1
    q = _dot(_rms(x1, row(gprex_ref)).astype(BF16), wq_ref[...])
    q_ref[...] = _to_kv_rows(q)


def _sample_mid(x, ya, o, y, hgp, ssp, wts, layer):
    n = x.shape[0]
    args = [x, ya, o, y, hgp, ssp, wts["wout"], wts["wq"], wts["gn"], wts["dx"], wts["snorm"], wts["gpost"],
            wts["gprex"]]
    out_shape = [jax.ShapeDtypeStruct((n, D_MODEL), F32),
                 jax.ShapeDtypeStruct((n * KV_SUB, 128), F32)]
    return pl.pallas_call(
        functools.partial(_sample_mid_kernel, layer=layer), grid=(1,),
        in_specs=[_full_spec(a) for a in args[:6]] + [_layer_spec(a, layer) for a in args[6:8]]
        + [_full_spec(a) for a in args[8:]], out_specs=[_full_spec(s) for s in out_shape], out_shape=out_shape,
        compiler_params=pltpu.CompilerParams(dimension_semantics=("arbitrary",), vmem_limit_bytes=VMEM_LIMIT),
        name="sample_mid",
    )(*args)


def _lane_class_reduce(x, op):
    sh = KV_SUB
    while sh < 128:
        x = op(x, pltpu.roll(x, sh, axis=1))
        sh *= 2
    return x


def _sample_attn_kernel(x1_ref, q_ref, k_ref, v_ref, wo_ref, gpostx_ref, x2_ref, *, layer):
    lane = lax.broadcasted_iota(jnp.int32, (KV_SUB, KV_ROWS), 1)
    sub = lax.broadcasted_iota(jnp.int32, (KV_SUB, KV_ROWS), 0)
    own = ((lane & (KV_SUB - 1)) == sub).astype(F32)
    rid = lax.broadcasted_iota(jnp.int32, (SB, KV_ROWS), 0)
    t_all = jnp.zeros((SB, KV_ROWS), F32)
    for j in range(SB):
        r = _dot_nt(q_ref[j].astype(BF16), k_ref[j].astype(BF16))
        t = jnp.sum(r * own, axis=0, keepdims=True)
        t_all = jnp.where(rid == j, t, t_all)
    n_tiles = KV_ROWS // 128
    lane1 = lax.broadcasted_iota(jnp.int32, (SB, 128), 1)
    piece = (lane1 // XA_HEADS) % KV_SPLIT
    chunks = []
    for c in range(n_tiles):
        x = t_all[:, c * 128:(c + 1) * 128]
        tot = x
        for k in range(1, KV_SPLIT):
            fwd = pltpu.roll(x, 128 - k * XA_HEADS, axis=1)
            bwd = pltpu.roll(x, (KV_SPLIT - k) * XA_HEADS, axis=1)
            tot = tot + jnp.where(piece + k < KV_SPLIT, fwd, bwd)
        chunks.append(tot * (XA_HD ** -0.5))
    mx = _lane_class_reduce(functools.reduce(jnp.maximum, chunks), jnp.maximum)
    es = [jnp.exp(ch - mx) for ch in chunks]
    den = _lane_class_reduce(functools.reduce(lambda a, b: a + b, es), lambda a, b: a + b)
    p_all = jnp.concatenate([e * (1.0 / den) for e in es], axis=1)
    rid_o = lax.broadcasted_iota(jnp.int32, (SB, D_MODEL), 0)
    att = jnp.zeros((SB, D_MODEL), F32)
    for j in range(SB):
        p8 = (own * p_all[j:j + 1, :]).astype(BF16)
        o = _dot(p8, v_ref[j].astype(BF16))
        row = jnp.concatenate([o[k * XA_HEADS + hd:k * XA_HEADS + hd + 1, :]
                               for hd in range(XA_HEADS) for k in range(KV_SPLIT)], axis=1)
        att = jnp.where(rid_o == j, row, att)
    x2_ref[...] = x1_ref[...] + _rms(_dot(att.astype(BF16), wo_ref[...]), gpostx_ref[layer:layer + 1, :])


def _sample_attn(x1, q8, ck_rows, cv_rows, wts, layer):
    n = x1.shape[0]
    rowblk = pl.BlockSpec((SB, D_MODEL), lambda i: (i, 0))
    qblk = pl.BlockSpec((SB, KV_SUB, 128), lambda i: (i, 0, 0))
    kvblk = pl.BlockSpec((None, SB, KV_ROWS, 128), lambda i: (layer, i, 0, 0))
    return pl.pallas_call(
        functools.partial(_sample_attn_kernel, layer=layer),
        grid=(n // SB,),
        in_specs=[rowblk, qblk, kvblk, kvblk, _layer_spec(wts["wo"], layer), _full_spec(wts["gpostx"])],
        out_specs=rowblk,
        out_shape=jax.ShapeDtypeStruct((n, D_MODEL), F32),
        compiler_params=pltpu.CompilerParams(dimension_semantics=("arbitrary",), vmem_limit_bytes=VMEM_LIMIT),
        name="sample_attn",
    )(x1, q8, ck_rows, cv_rows, wts["wo"], wts["gpostx"])


def _sample_layer(x, ca_all, shg_all, sc_all, sss_all, ck_rows, cv_rows, wts, layer, prev):
    n = x.shape[0]
    prev_conv, prev_states = (None, None) if prev is None else (prev[:2], prev[2:])
    ya, ca_new, hgp, ssp, sc_new = _sample_pre(x, ca_all, sc_all, wts, layer, prev_conv)
    o, y, shg_new, sss_new = _sample_state(hgp, ssp, shg_all, sss_all, layer, prev_states)
    x1, q8 = _sample_mid(x, ya, o, y, hgp, ssp, wts, layer)
    x2 = _sample_attn(x1, q8.reshape(n, KV_SUB, 128), ck_rows, cv_rows, wts, layer)
    return x2, (ca_new, sc_new, shg_new, sss_new)


PROMPT_TILE = 512


def kernel(x_prompt, x_sample, mem_prompt, state_conv_a, state_hgrn, state_ssd_conv, state_ssd, cache_mem_k,
           cache_mem_v, w_in, conv_a_w, hgrn_lb, hgrn_gnorm, ssd_conv_w, ssd_conv_b, ssd_dt_bias, ssd_A_log, ssd_D,
           ssd_norm, w_out, g_pre_mix, g_post_mix, g_pre_x, g_post_x, g_mem, w_q, w_k, w_v, w_o):
    depth = w_in.shape[0]
    n = x_sample.shape[0]
    yp = x_prompt
    ys = x_sample.reshape(n, D_MODEL)
    ck_rows, cv_rows = _kv_rows_view(cache_mem_k), _kv_rows_view(cache_mem_v)
    sss_all = state_ssd.reshape(depth, n, SSD_HEADS * SSD_P, SSD_N)
    sc_all = jnp.transpose(state_ssd_conv, (0, 2, 1, 3))
    kv_rows = p_states = s_states = None
    wts =_prep_weights(w_in, conv_a_w, hgrn_lb, hgrn_gnorm, ssd_conv_w, ssd_conv_b, ssd_dt_bias, ssd_A_log, ssd_D,
                        ssd_norm, w_out, g_pre_mix, g_post_mix, g_pre_x, g_post_x, g_mem, w_q, w_k, w_v, w_o)
    for l in range(depth):
        *kv_rows, mk, mv = _memory_kv(mem_prompt, wts, l, depth, kv_rows)
        yp, *p_states = _prompt_layer(yp, mk, mv, wts, l, depth, p_states, PROMPT_TILE)
        ys, s_states = _sample_layer(ys, state_conv_a, state_hgrn, sc_all, sss_all, ck_rows, cv_rows, wts, l,
                                     s_states)
    p_ca, p_hg, p_sc, p_ss = p_states
    s_ca, s_sc, s_hg, s_ss = s_states
    return (yp, ys.reshape(x_sample.shape), p_ca, p_hg, p_sc, p_ss, _kv_from_rows(kv_rows[0]),
            _kv_from_rows(kv_rows[1]), s_ca, s_hg, jnp.transpose(s_sc, (0, 2, 1, 3)), s_ss.reshape(state_ssd.shape))
```
